```python
import math
import jax, jax.numpy as jnp
from jax import lax
import numpy as np

D_MODEL = 1024
BATCH = 8
SEQ = 2048
DEPTH = 2
DEC_BATCH = 128
DEC_SEQ = 1
PAST_LEN = 2048
PAGE_SIZE = 128

N_MIXERS = 2
N_META = 16
SSM_GROUP = 16
N_GROUPS = D_MODEL // SSM_GROUP
STATE_P = 64
DT_MIN = 1e-3
DT_MAX = 1e-1
HEAD_DIM = 64
N_HEADS = D_MODEL // HEAD_DIM
Q_BLOCK = 128
D_FF = 2816
CONV_W = 3
EPS = 1e-6
NEG_INF = -1e30

kernel_name = "s5_fox_convffn_hybrid_step"


def rmsnorm(x, g):
    xf = x.astype(jnp.float32)
    y = xf * lax.rsqrt(jnp.mean(xf * xf, axis=-1, keepdims=True) + EPS)
    return (y * g.astype(jnp.float32)).astype(x.dtype)


def s5_discretize(a_re, a_im, b_re, b_im, log_step):
    f32 = jnp.float32
    a_re = a_re.astype(f32)
    a_im = a_im.astype(f32)
    b_re = b_re.astype(f32)
    b_im = b_im.astype(f32)
    dt = jnp.exp(log_step.astype(f32))[:, None]
    mag = jnp.exp(a_re * dt)
    ab_re = mag * jnp.cos(a_im * dt)
    ab_im = mag * jnp.sin(a_im * dt)
    den = a_re * a_re + a_im * a_im
    n_re = ab_re - 1.0
    n_im = ab_im
    f_re = (n_re * a_re + n_im * a_im) / den
    f_im = (n_im * a_re - n_re * a_im) / den
    bb_re = f_re[..., None] * b_re - f_im[..., None] * b_im
    bb_im = f_re[..., None] * b_im + f_im[..., None] * b_re
    return ab_re, ab_im, bb_re, bb_im


def s5_combine(e1, e2):
    a1r, a1i, b1r, b1i = e1
    a2r, a2i, b2r, b2i = e2
    return (a1r * a2r - a1i * a2i,
            a1r * a2i + a1i * a2r,
            a2r * b1r - a2i * b1i + b2r,
            a2r * b1i + a2i * b1r + b2i)


def s5_mixer(u, h0_re, h0_im, a_re, a_im, b_re, b_im, c_re, c_im, d_skip, log_step, w_glu):
    n, l, _ = u.shape
    ab_re, ab_im, bb_re, bb_im = s5_discretize(a_re, a_im, b_re, b_im, log_step)
    ug = u.astype(jnp.float32).reshape(n, l, N_GROUPS, SSM_GROUP)
    bu_re = jnp.einsum('nlgh,gph->nlgp', ug, bb_re)
    bu_im = jnp.einsum('nlgh,gph->nlgp', ug, bb_im)
    h0_re = h0_re.astype(jnp.float32)
    h0_im = h0_im.astype(jnp.float32)
    bu_re = bu_re.at[:, 0].add(ab_re * h0_re - ab_im * h0_im)
    bu_im = bu_im.at[:, 0].add(ab_re * h0_im + ab_im * h0_re)
    ar = jnp.broadcast_to(ab_re, bu_re.shape)
    ai = jnp.broadcast_to(ab_im, bu_im.shape)
    _, _, hr, hi = lax.associative_scan(s5_combine, (ar, ai, bu_re, bu_im), axis=1)
    y = (jnp.einsum('nlgp,ghp->nlgh', hr, c_re.astype(jnp.float32))
         - jnp.einsum('nlgp,ghp->nlgh', hi, c_im.astype(jnp.float32))
         + d_skip.astype(jnp.float32) * ug)
    y = jax.nn.gelu(y.reshape(n, l, D_MODEL)).astype(u.dtype)
    z = jnp.einsum('nld,de->nle', y, w_glu)
    out = z[..., :D_MODEL] * jax.nn.sigmoid(z[..., D_MODEL:])
    return out, hr[:, -1], hi[:, -1]


def fox_project(u, w_qkv, w_f, b_f, q_g, k_g):
    n, l, _ = u.shape
    qkv = jnp.einsum('nld,de->nle', u, w_qkv).reshape(n, l, 3, N_HEADS, HEAD_DIM)
    q = rmsnorm(qkv[:, :, 0], q_g)
    k = rmsnorm(qkv[:, :, 1], k_g)
    v = qkv[:, :, 2]
    logf = jax.nn.log_sigmoid((jnp.einsum('nld,dh->nlh', u, w_f) + b_f).astype(jnp.float32))
    return q, k, v, logf


def fox_attend(q, cq, qpos, k, v, ck, kpos):
    s = jnp.einsum('nqhe,nkhe->nhqk', q.astype(jnp.float32), k.astype(jnp.float32)) * (HEAD_DIM ** -0.5)
    s = s + jnp.transpose(cq, (0, 2, 1))[..., :, None] - jnp.transpose(ck, (0, 2, 1))[..., None, :]
    mask = kpos[None, :] <= qpos[:, None]
    s = jnp.where(mask, s, NEG_INF)
    p = jax.nn.softmax(s, axis=-1)
    o = jnp.einsum('nhqk,nkhe->nqhe', p, v.astype(jnp.float32))
    return o.astype(q.dtype)


def fox_prompt(q, k, v, logf):
    n, l = q.shape[0], q.shape[1]
    c = jnp.cumsum(logf, axis=1)
    pos = jnp.arange(l)
    o_meta = fox_attend(q[:, :N_META], c[:, :N_META], pos[:N_META],
                        k[:, :N_META], v[:, :N_META], c[:, :N_META], pos[:N_META])
    n_blk = (l - N_META) // Q_BLOCK

    def one_block(i):
        st = N_META + i * Q_BLOCK
        qb = lax.dynamic_slice_in_dim(q, st, Q_BLOCK, axis=1)
        cb = lax.dynamic_slice_in_dim(c, st, Q_BLOCK, axis=1)
        pb = st + jnp.arange(Q_BLOCK)
        return fox_attend(qb, cb, pb, k, v, c, pos)

    o = lax.map(one_block, jnp.arange(n_blk))
    o = jnp.moveaxis(o, 0, 1).reshape(n, n_blk * Q_BLOCK, N_HEADS, HEAD_DIM)
    return jnp.concatenate([o_meta, o], axis=1)


def fox_sample(q, k, v, logf, cache_k, cache_v, cache_logf, page_table):
    nb, n_pages = page_table.shape
    past = n_pages * cache_k.shape[1]
    kp = cache_k[page_table].reshape(nb, past, N_HEADS, HEAD_DIM).astype(k.dtype)
    vp = cache_v[page_table].reshape(nb, past, N_HEADS, HEAD_DIM).astype(v.dtype)
    lp = cache_logf[page_table].reshape(nb, past, N_HEADS).astype(jnp.float32)
    k_all = jnp.concatenate([kp, k], axis=1)
    v_all = jnp.concatenate([vp, v], axis=1)
    c = jnp.cumsum(jnp.concatenate([lp, logf], axis=1), axis=1)
    t = q.shape[1]
    kpos = jnp.arange(past + t)
    qpos = past + jnp.arange(t)
    return fox_attend(q, c[:, past:], qpos, k_all, v_all, c, kpos)


def conv_ffn(u, buf, w_up, w_gate, conv_w, conv_b, w_down):
    l = u.shape[1]
    h = jnp.einsum('nld,df->nlf', u, w_up)
    g = jnp.einsum('nld,df->nlf', u, w_gate)
    full = jnp.concatenate([buf.astype(h.dtype), h], axis=1)
    hc = conv_b
    for j in range(CONV_W):
        hc = hc + conv_w[j] * full[:, j:j + l]
    out = jnp.einsum('nlf,fd->nld', jax.nn.silu(hc) * g, w_down)
    return out, full[:, -(CONV_W - 1):]


def setup_inputs(seed: int = 0) -> dict:
    key = jax.random.key(seed)
    ks = jax.random.split(key, 40)
    f32 = jnp.float32
    n_pages = PAST_LEN // PAGE_SIZE
    n_used = DEC_BATCH * n_pages
    n_pool = n_used + n_used // 4
    nrm = lambda k, shape, s: jax.random.normal(k, shape, f32) * s
    x_prompt = nrm(ks[0], (BATCH, SEQ, D_MODEL), 1.0)
    x_sample = nrm(ks[1], (DEC_BATCH, DEC_SEQ, D_MODEL), 1.0)
    state_ssm_re = nrm(ks[2], (DEC_BATCH, N_GROUPS, STATE_P), 0.1)
    state_ssm_im = nrm(ks[3], (DEC_BATCH, N_GROUPS, STATE_P), 0.1)
    cache_k = nrm(ks[4], (n_pool, PAGE_SIZE, N_HEADS, HEAD_DIM), 1.0)
    cache_v = nrm(ks[5], (n_pool, PAGE_SIZE, N_HEADS, HEAD_DIM), 1.0)
    cache_logf = jax.nn.log_sigmoid(2.5 + jax.random.normal(ks[6], (n_pool, PAGE_SIZE, N_HEADS), f32))
    state_conv = nrm(ks[7], (DEPTH, DEC_BATCH, CONV_W - 1, D_FF), 1.0)
    page_table = jax.random.permutation(ks[8], n_pool)[:n_used].reshape(DEC_BATCH, n_pages).astype(jnp.int32)
    meta_tokens = nrm(ks[9], (N_META, D_MODEL), 1.0)
    norm_mix_g = 1.0 + nrm(ks[10], (DEPTH, D_MODEL), 0.01)
    norm_ffn_g = 1.0 + nrm(ks[11], (DEPTH, D_MODEL), 0.01)
    ssm_a_re = -0.5 + nrm(ks[12], (N_GROUPS, STATE_P), 0.01)
    ssm_a_im = math.pi * jnp.arange(STATE_P, dtype=f32)[None, :] + nrm(ks[13], (N_GROUPS, STATE_P), 0.01)
    ssm_b_re = nrm(ks[14], (N_GROUPS, STATE_P, SSM_GROUP), (2 * SSM_GROUP) ** -0.5)
    ssm_b_im = nrm(ks[15], (N_GROUPS, STATE_P, SSM_GROUP), (2 * SSM_GROUP) ** -0.5)
    ssm_c_re = nrm(ks[16], (N_GROUPS, SSM_GROUP, STATE_P), STATE_P ** -0.5)
    ssm_c_im = nrm(ks[17], (N_GROUPS, SSM_GROUP, STATE_P), STATE_P ** -0.5)
    ssm_d = nrm(ks[18], (N_GROUPS, SSM_GROUP), 1.0)
    ssm_log_step = jax.random.uniform(ks[19], (N_GROUPS,), f32, math.log(DT_MIN), math.log(DT_MAX))
    ssm_w_glu = nrm(ks[20], (D_MODEL, 2 * D_MODEL), D_MODEL ** -0.5)
    attn_w_qkv = nrm(ks[21], (D_MODEL, 3 * D_MODEL), D_MODEL ** -0.5)
    attn_w_f = nrm(ks[22], (D_MODEL, N_HEADS), D_MODEL ** -0.5)
    attn_b_f = jax.random.uniform(ks[23], (N_HEADS,), f32, 1.0, 4.0)
    attn_q_g = 1.0 + nrm(ks[24], (HEAD_DIM,), 0.01)
    attn_k_g = 1.0 + nrm(ks[25], (HEAD_DIM,), 0.01)
    attn_w_o = nrm(ks[26], (D_MODEL, D_MODEL), D_MODEL ** -0.5)
    ffn_w_up = nrm(ks[27], (DEPTH, D_MODEL, D_FF), D_MODEL ** -0.5)
    ffn_w_gate = nrm(ks[28], (DEPTH, D_MODEL, D_FF), D_MODEL ** -0.5)
    ffn_conv_w = nrm(ks[29], (DEPTH, CONV_W, D_FF), CONV_W ** -0.5)
    ffn_conv_b = nrm(ks[30], (DEPTH, D_FF), 0.01)
    ffn_w_down = nrm(ks[31], (DEPTH, D_FF, D_MODEL), D_FF ** -0.5)
    return {"x_prompt": x_prompt, "x_sample": x_sample,
            "state_ssm_re": state_ssm_re, "state_ssm_im": state_ssm_im,
            "cache_k": cache_k, "cache_v": cache_v, "cache_logf": cache_logf,
            "state_conv": state_conv, "page_table": page_table,
            "meta_tokens": meta_tokens, "norm_mix_g": norm_mix_g, "norm_ffn_g": norm_ffn_g,
            "ssm_a_re": ssm_a_re, "ssm_a_im": ssm_a_im, "ssm_b_re": ssm_b_re, "ssm_b_im": ssm_b_im,
            "ssm_c_re": ssm_c_re, "ssm_c_im": ssm_c_im, "ssm_d": ssm_d, "ssm_log_step": ssm_log_step,
            "ssm_w_glu": ssm_w_glu, "attn_w_qkv": attn_w_qkv, "attn_w_f": attn_w_f, "attn_b_f": attn_b_f,
            "attn_q_g": attn_q_g, "attn_k_g": attn_k_g, "attn_w_o": attn_w_o,
            "ffn_w_up": ffn_w_up, "ffn_w_gate": ffn_w_gate, "ffn_conv_w": ffn_conv_w,
            "ffn_conv_b": ffn_conv_b, "ffn_w_down": ffn_w_down}


def reference(x_prompt, x_sample, state_ssm_re, state_ssm_im, cache_k, cache_v, cache_logf,
              state_conv, page_table, meta_tokens, norm_mix_g, norm_ffn_g,
              ssm_a_re, ssm_a_im, ssm_b_re, ssm_b_im, ssm_c_re, ssm_c_im, ssm_d, ssm_log_step,
              ssm_w_glu, attn_w_qkv, attn_w_f, attn_b_f, attn_q_g, attn_k_g, attn_w_o,
              ffn_w_up, ffn_w_gate, ffn_conv_w, ffn_conv_b, ffn_w_down):
    n = x_prompt.shape[0]
    meta = jnp.broadcast_to(meta_tokens[None].astype(x_prompt.dtype), (n, N_META, D_MODEL))
    xp = jnp.concatenate([meta, x_prompt], axis=1)
    xs = x_sample
    lp = xp.shape[1]
    ls = xs.shape[1]
    conv_p = []
    conv_s = []
    for i in range(DEPTH):
        up = rmsnorm(xp, norm_mix_g[i])
        us = rmsnorm(xs, norm_mix_g[i])
        if i % N_MIXERS == 0:
            zero = jnp.zeros((n, N_GROUPS, STATE_P), jnp.float32)
            mp, ssm_re_p, ssm_im_p = s5_mixer(up, zero, zero, ssm_a_re, ssm_a_im, ssm_b_re, ssm_b_im,
                                              ssm_c_re, ssm_c_im, ssm_d, ssm_log_step, ssm_w_glu)
            ms, ssm_re_s, ssm_im_s = s5_mixer(us, state_ssm_re, state_ssm_im, ssm_a_re, ssm_a_im,
                                              ssm_b_re, ssm_b_im, ssm_c_re, ssm_c_im, ssm_d,
                                              ssm_log_step, ssm_w_glu)
        else:
            qp, k_p, v_p, lf_p = fox_project(up, attn_w_qkv, attn_w_f, attn_b_f, attn_q_g, attn_k_g)
            op = fox_prompt(qp, k_p, v_p, lf_p)
            mp = jnp.einsum('nld,de->nle', op.reshape(n, lp, D_MODEL), attn_w_o)
            qs, k_s, v_s, lf_s = fox_project(us, attn_w_qkv, attn_w_f, attn_b_f, attn_q_g, attn_k_g)
            os_ = fox_sample(qs, k_s, v_s, lf_s, cache_k, cache_v, cache_logf, page_table)
            ms = jnp.einsum('nld,de->nle', os_.reshape(xs.shape[0], ls, D_MODEL), attn_w_o)
        xp = xp + mp
        xs = xs + ms
        zbuf = jnp.zeros((n, CONV_W - 1, D_FF), xp.dtype)
        fp, cp = conv_ffn(rmsnorm(xp, norm_ffn_g[i]), zbuf, ffn_w_up[i], ffn_w_gate[i],
                          ffn_conv_w[i], ffn_conv_b[i], ffn_w_down[i])
        fs, cs = conv_ffn(rmsnorm(xs, norm_ffn_g[i]), state_conv[i], ffn_w_up[i], ffn_w_gate[i],
                          ffn_conv_w[i], ffn_conv_b[i], ffn_w_down[i])
        xp = xp + fp
        xs = xs + fs
        conv_p.append(cp)
        conv_s.append(cs)
    y_prompt = xp[:, N_META:]
    return (y_prompt, xs, ssm_re_p, ssm_im_p, ssm_re_s, ssm_im_s,
            k_p, v_p, lf_p, k_s, v_s, lf_s, jnp.stack(conv_p), jnp.stack(conv_s))
```

```python
import functools

import jax
import jax.numpy as jnp
from jax import lax
from jax.experimental import pallas as pl
from jax.experimental.pallas import tpu as pltpu

F32 = jnp.float32
BF16 = jnp.bfloat16

D_MODEL = 1024
N_BATCH = 8
SEQ = 2048
N_SAMPLE = 128
N_META = 16
SMALL_ROWS = 256
META_ROW0 = 128
N_GROUPS = 64
STATE_P = 64
SSM_GROUP = 16
N_HEADS = 16
HEAD_DIM = 64
N_PAIRS = N_HEADS // 2
D_FF = 2816
CONV_W = 3
PAGE = 128
EPS = 1e-6
NEG_INF = -1e30

LANES = 128
SUBLANES = 8
S5_CHUNK = 8
S5_SETS = D_MODEL // LANES
SET_STATE = (LANES // SSM_GROUP) * STATE_P
N_CHUNKS = SEQ // S5_CHUNK
VMEM_LIMIT = 56 * 1024 * 1024


def _cparams(sem):
    return pltpu.CompilerParams(dimension_semantics=sem, vmem_limit_bytes=VMEM_LIMIT)


def _const_spec(shape):
    nd = len(shape)
    return pl.BlockSpec(shape, lambda *_: (0,) * nd, pipeline_mode=pl.Buffered(1))


def _dot(a, b):
    return jnp.dot(a, b, preferred_element_type=F32)


def _dot_nt(a, b):
    return lax.dot_general(a, b, (((1,), (1,)), ((), ())), preferred_element_type=F32)


def _split2(x):
    hi = x.astype(BF16)
    lo = (x - hi.astype(F32)).astype(BF16)
    return hi, lo


def _split3(x):
    hi = x.astype(BF16)
    r = x - hi.astype(F32)
    mid = r.astype(BF16)
    lo = (r - mid.astype(F32)).astype(BF16)
    return hi, mid, lo


def _dot_split(parts, w):
    acc = _dot(parts[0], w)
    for p in parts[1:]:
        acc = acc + _dot(p, w)
    return acc


def _rmsnorm(x, g):
    ms = jnp.mean(x * x, axis=-1, keepdims=True)
    return x * lax.rsqrt(ms + EPS) * g


def _gelu_tanh(x):
    c = 0.7978845608028654
    return 0.5 * x * (1.0 + jnp.tanh(c * (x + 0.044715 * (x * x * x))))


def _sigmoid(x):
    return 1.0 / (1.0 + jnp.exp(-x))


def _norm_shuffle_kernel(x_ref, g_ref, o_ref):
    g = g_ref[...]
    for s in range(S5_CHUNK):
        u = _rmsnorm(x_ref[:, s * D_MODEL:(s + 1) * D_MODEL], g).astype(BF16)
        for j in range(S5_SETS):
            o_ref[j, :, s * LANES:(s + 1) * LANES] = u[:, j * LANES:(j + 1) * LANES]


def _norm_shuffle(xc, g):
    rows = xc.shape[0]
    rb = 128
    return pl.pallas_call(
        _norm_shuffle_kernel,
        grid=(rows // rb,),
        in_specs=[pl.BlockSpec((rb, S5_CHUNK * D_MODEL), lambda i: (i, 0)),
                  _const_spec((1, D_MODEL))],
        out_specs=pl.BlockSpec((S5_SETS, rb, S5_CHUNK * LANES), lambda i: (0, i, 0)),
        out_shape=jax.ShapeDtypeStruct((S5_SETS, rows, S5_CHUNK * LANES), BF16),
        compiler_params=_cparams(("parallel",)),
        name="norm_shuffle",
    )(xc, g)


def _norm_rows_kernel(x_ref, g_ref, o_ref):
    o_ref[...] = _rmsnorm(x_ref[...], g_ref[...]).astype(BF16)


def _norm_rows(x, g):
    rows = x.shape[0]
    return pl.pallas_call(
        _norm_rows_kernel,
        grid=(1,),
        in_specs=[pl.BlockSpec((rows, D_MODEL), lambda i: (0, 0)),
                  pl.BlockSpec((1, D_MODEL), lambda i: (0, 0))],
        out_specs=pl.BlockSpec((rows, D_MODEL), lambda i: (0, 0)),
        out_shape=jax.ShapeDtypeStruct((rows, D_MODEL), BF16),
        name="norm_rows",
    )(x, g)


S5_ROW_BLOCK = 512
N_SLABS = SET_STATE // LANES


def _s5_kernel(ub_ref, um_ref, us_ref, h0r_ref, h0i_ref, wt_ref, wb_ref, wc_ref, a8_ref, dt_ref,
               wb1_ref, wc1_ref, a1_ref, d1_ref,
               yb_ref, ym_ref, ys_ref, hpr_ref, hpi_ref, hsr_ref, hsi_ref, xr_scr, xi_scr):
    rows = ub_ref.shape[1]
    wb = wb_ref[0]
    wt = wt_ref[0]
    wc = wc_ref[0]
    dt = dt_ref[0]

    for r in range(rows // S5_ROW_BLOCK):
        rs = slice(r * S5_ROW_BLOCK, (r + 1) * S5_ROW_BLOCK)
        x = _dot(ub_ref[0, rs, :], wb)
        for k in range(N_SLABS):
            xr_scr[k, rs, :] = x[:, k * LANES:(k + 1) * LANES]
            xi_scr[k, rs, :] = x[:, SET_STATE + k * LANES:SET_STATE + (k + 1) * LANES]
    xm = _dot(um_ref[0], wb)

    a8r = jnp.broadcast_to(a8_ref[0, 0:1, :], (SUBLANES, SET_STATE))
    a8i = jnp.broadcast_to(a8_ref[0, 1:2, :], (SUBLANES, SET_STATE))

    hr = jnp.zeros((SUBLANES, SET_STATE), F32)
    hi = jnp.zeros((SUBLANES, SET_STATE), F32)
    hm_prev = []
    for cm in range(N_META // S5_CHUNK):
        hm_prev.append(jnp.concatenate([hr, hi], axis=1))
        xr = xm[cm * SUBLANES:(cm + 1) * SUBLANES, :SET_STATE]
        xi = xm[cm * SUBLANES:(cm + 1) * SUBLANES, SET_STATE:]
        hr, hi = a8r * hr - a8i * hi + xr, a8r * hi + a8i * hr + xi

    ar = [a8r[:, k * LANES:(k + 1) * LANES] for k in range(N_SLABS)]
    ai = [a8i[:, k * LANES:(k + 1) * LANES] for k in range(N_SLABS)]

    def step(c, carry):
        hrs, his = carry
        nr, ni = [], []
        for k in range(N_SLABS):
            idx = (k, pl.ds(c, N_BATCH, stride=N_CHUNKS), slice(None))
            xr = xr_scr[idx]
            xi = xi_scr[idx]
            xr_scr[idx] = hrs[k]
            xi_scr[idx] = his[k]
            nr.append(ar[k] * hrs[k] - ai[k] * his[k] + xr)
            ni.append(ar[k] * his[k] + ai[k] * hrs[k] + xi)
        return tuple(nr), tuple(ni)

    init = (tuple(hr[:, k * LANES:(k + 1) * LANES] for k in range(N_SLABS)),
            tuple(hi[:, k * LANES:(k + 1) * LANES] for k in range(N_SLABS)))
    hrs, his = lax.fori_loop(0, N_CHUNKS, step, init)
    hpr_ref[...] = jnp.concatenate(hrs, axis=1)
    hpi_ref[...] = jnp.concatenate(his, axis=1)

    def emit(u, hprev):
        y = _dot(u, wt) + _dot(hprev.astype(BF16), wc) + dt * u.astype(F32)
        return _gelu_tanh(y).astype(BF16)

    for r in range(rows // S5_ROW_BLOCK):
        rs = slice(r * S5_ROW_BLOCK, (r + 1) * S5_ROW_BLOCK)
        hprev = jnp.concatenate([xr_scr[k, rs, :] for k in range(N_SLABS)]
                                + [xi_scr[k, rs, :] for k in range(N_SLABS)], axis=1)
        yb_ref[0, rs, :] = emit(ub_ref[0, rs, :], hprev)
    ym_ref[0] = emit(um_ref[0], jnp.concatenate(hm_prev, axis=0))

    us = us_ref[...]
    xs = _dot(us, wb1_ref[0])
    a1r = a1_ref[0, 0:1, :]
    a1i = a1_ref[0, 1:2, :]
    h0r = h0r_ref[...]
    h0i = h0i_ref[...]
    hsr = a1r * h0r - a1i * h0i + xs[:, :SET_STATE]
    hsi = a1r * h0i + a1i * h0r + xs[:, SET_STATE:]
    hsr_ref[...] = hsr
    hsi_ref[...] = hsi
    hs = jnp.concatenate([hsr, hsi], axis=1).astype(BF16)
    ys = _dot(hs, wc1_ref[0]) + d1_ref[0] * us.astype(F32)
    ys_ref[...] = _gelu_tanh(ys).astype(BF16)


def _s5(ub, um, u_small, h0r, h0i, w):
    rows = ub.shape[1]
    cw = S5_CHUNK * LANES
    set_spec = lambda shape: pl.BlockSpec((1,) + shape, lambda j: (j, 0, 0))
    out_shapes = (
        jax.ShapeDtypeStruct((S5_SETS, rows, cw), BF16),
        jax.ShapeDtypeStruct((S5_SETS, N_META, cw), BF16),
        jax.ShapeDtypeStruct((N_SAMPLE, D_MODEL), BF16),
        jax.ShapeDtypeStruct((N_BATCH, S5_SETS * SET_STATE), F32),
        jax.ShapeDtypeStruct((N_BATCH, S5_SETS * SET_STATE), F32),
        jax.ShapeDtypeStruct((N_SAMPLE, S5_SETS * SET_STATE), F32),
        jax.ShapeDtypeStruct((N_SAMPLE, S5_SETS * SET_STATE), F32),
    )
    return pl.pallas_call(
        _s5_kernel,
        grid=(S5_SETS,),
        in_specs=[
            set_spec((rows, cw)),
            set_spec((N_META, cw)),
            pl.BlockSpec((N_SAMPLE, LANES), lambda j: (0, j)),
            pl.BlockSpec((N_SAMPLE, SET_STATE), lambda j: (0, j)),
            pl.BlockSpec((N_SAMPLE, SET_STATE), lambda j: (0, j)),
            set_spec((cw, cw)),
            set_spec((cw, 2 * SET_STATE)),
            set_spec((2 * SET_STATE, cw)),
            set_spec((2, SET_STATE)),
            set_spec((1, cw)),
            set_spec((LANES, 2 * SET_STATE)),
            set_spec((2 * SET_STATE, LANES)),
            set_spec((2, SET_STATE)),
            set_spec((1, LANES)),
        ],
        out_specs=(
            set_spec((rows, cw)),
            set_spec((N_META, cw)),
            pl.BlockSpec((N_SAMPLE, LANES), lambda j: (0, j)),
            pl.BlockSpec((N_BATCH, SET_STATE), lambda j: (0, j)),
            pl.BlockSpec((N_BATCH, SET_STATE), lambda j: (0, j)),
            pl.BlockSpec((N_SAMPLE, SET_STATE), lambda j: (0, j)),
            pl.BlockSpec((N_SAMPLE, SET_STATE), lambda j: (0, j)),
        ),
        out_shape=out_shapes,
        scratch_shapes=[pltpu.VMEM((N_SLABS, rows, LANES), F32),
                        pltpu.VMEM((N_SLABS, rows, LANES), F32)],
        compiler_params=_cparams(("parallel",)),
        name="s5_mixer",
    )(ub, um, u_small, h0r, h0i, w["wt"], w["wb"], w["wc"], w["a8"], w["dt"],
      w["wb1"], w["wc1"], w["a1"], w["d1"])


def _s5_weights(a_re, a_im, b_re, b_im, c_re, c_im, d_skip, log_step):
    hp = lax.Precision.HIGHEST
    lc = S5_CHUNK
    gs = LANES // SSM_GROUP
    a_re = a_re.astype(F32)
    a_im = a_im.astype(F32)
    dt = jnp.exp(log_step.astype(F32))[:, None]
    mag = jnp.exp(a_re * dt)
    ab_re = mag * jnp.cos(a_im * dt)
    ab_im = mag * jnp.sin(a_im * dt)
    den = a_re * a_re + a_im * a_im
    n_re = ab_re - 1.0
    n_im = ab_im
    f_re = (n_re * a_re + n_im * a_im) / den
    f_im = (n_im * a_re - n_re * a_im) / den
    b_re = b_re.astype(F32)
    b_im = b_im.astype(F32)
    bb_re = f_re[..., None] * b_re - f_im[..., None] * b_im
    bb_im = f_re[..., None] * b_im + f_im[..., None] * b_re
    pw_re = [jnp.ones_like(ab_re)]
    pw_im = [jnp.zeros_like(ab_re)]
    for _ in range(lc):
        pr, pi = pw_re[-1], pw_im[-1]
        pw_re.append(pr * ab_re - pi * ab_im)
        pw_im.append(pr * ab_im + pi * ab_re)
    pw_re = jnp.stack(pw_re)
    pw_im = jnp.stack(pw_im)
    abb_re = pw_re[:lc, :, :, None] * bb_re - pw_im[:lc, :, :, None] * bb_im
    abb_im = pw_re[:lc, :, :, None] * bb_im + pw_im[:lc, :, :, None] * bb_re
    c_re = c_re.astype(F32)
    c_im = c_im.astype(F32)
    kern = (jnp.einsum('gop,kgpi->kgio', c_re, abb_re, precision=hp)
            - jnp.einsum('gop,kgpi->kgio', c_im, abb_im, precision=hp))
    eye = jnp.eye(gs, dtype=F32)
    lag = jnp.arange(lc)[None, :] - jnp.arange(lc)[:, None]
    kt = jnp.where((lag >= 0)[:, :, None, None, None], kern[jnp.clip(lag, 0, lc - 1)], 0.0)
    kt = kt.reshape(lc, lc, S5_SETS, gs, SSM_GROUP, SSM_GROUP)
    wt = jnp.einsum('stjgio,gh->jsgitho', kt, eye).reshape(S5_SETS, lc * LANES, lc * LANES)

    def blockdiag_in(m):
        m = m.reshape(lc, S5_SETS, gs, STATE_P, SSM_GROUP)
        return jnp.einsum('sjgpi,gh->jsgihp', m, eye).reshape(S5_SETS, lc * LANES, SET_STATE)

    rev = slice(None, None, -1)
    wb = jnp.concatenate([blockdiag_in(abb_re[rev]), blockdiag_in(abb_im[rev])], axis=2)
    ca_re = c_re[None] * pw_re[1:, :, None, :] - c_im[None] * pw_im[1:, :, None, :]
    ca_im = c_re[None] * pw_im[1:, :, None, :] + c_im[None] * pw_re[1:, :, None, :]

    def blockdiag_out(m):
        m = m.reshape(lc, S5_SETS, gs, SSM_GROUP, STATE_P)
        return jnp.einsum('tjgop,gh->jgptho', m, eye).reshape(S5_SETS, SET_STATE, lc * LANES)

    wc = jnp.concatenate([blockdiag_out(ca_re), -blockdiag_out(ca_im)], axis=1)
    a8 = jnp.stack([pw_re[lc].reshape(S5_SETS, SET_STATE), pw_im[lc].reshape(S5_SETS, SET_STATE)], axis=1)
    a1 = jnp.stack([ab_re.reshape(S5_SETS, SET_STATE), ab_im.reshape(S5_SETS, SET_STATE)], axis=1)
    d1 = d_skip.astype(F32).reshape(S5_SETS, 1, LANES)
    dtile = jnp.tile(d1, (1, 1, lc))
    wb1 = jnp.concatenate([blockdiag_in(jnp.tile(bb_re[None], (lc, 1, 1, 1)))[:, :LANES],
                           blockdiag_in(jnp.tile(bb_im[None], (lc, 1, 1, 1)))[:, :LANES]], axis=2)
    c1_re = blockdiag_out(jnp.tile(c_re[None], (lc, 1, 1, 1)))[:, :, :LANES]
    c1_im = blockdiag_out(jnp.tile(c_im[None], (lc, 1, 1, 1)))[:, :, :LANES]
    wc1 = jnp.concatenate([c1_re, -c1_im], axis=1)
    return {"wt": wt.astype(BF16), "wb": wb.astype(BF16), "wc": wc.astype(BF16), "a8": a8, "dt": dtile,
            "wb1": wb1.astype(BF16), "wc1": wc1.astype(BF16), "a1": a1, "d1": d1}


def _glu_res_kernel(x_ref, y_ref, w_ref, o_ref, *, slots, nsets, sw):
    w = w_ref[...]
    for s in range(slots):
        y = jnp.concatenate([y_ref[j, :, s * sw:(s + 1) * sw] for j in range(nsets)], axis=1)
        z = _dot(y, w)
        o = z[:, :D_MODEL] * _sigmoid(z[:, D_MODEL:])
        cs = slice(s * D_MODEL, (s + 1) * D_MODEL)
        o_ref[:, cs] = x_ref[:, cs] + o


def _glu_res(x, y, w_glu, *, slots, rb):
    nsets, rows, yw = y.shape
    sw = yw // slots
    return pl.pallas_call(
        functools.partial(_glu_res_kernel, slots=slots, nsets=nsets, sw=sw),
        grid=(rows // rb,),
        in_specs=[pl.BlockSpec((rb, slots * D_MODEL), lambda i: (i, 0)),
                  pl.BlockSpec((nsets, rb, yw), lambda i: (0, i, 0)),
                  _const_spec((D_MODEL, 2 * D_MODEL))],
        out_specs=pl.BlockSpec((rb, slots * D_MODEL), lambda i: (i, 0)),
        out_shape=jax.ShapeDtypeStruct((rows, slots * D_MODEL), F32),
        compiler_params=_cparams(("parallel",)),
        name="glu_res",
    )(x, y, w_glu)


FFN_TM = 512
FFN_FC = D_FF // 2


def _conv_act(h, gt, p1, p2, cw, cb):
    hc = cb + cw[0:1, :] * p2
    hc = hc + cw[1:2, :] * p1
    hc = hc + cw[2:3, :] * h
    return (hc * _sigmoid(hc) * gt).astype(BF16)


def _ffn_kernel(*refs, proj, tiles_per_seq):
    if proj:
        (x_ref, a_ref, wa_ref, g_ref, wu_ref, wg_ref, cw_ref, cb_ref, wd_ref, init_ref,
         o_ref, hl_ref, carry_scr, act_scr) = refs
    else:
        (x_ref, g_ref, wu_ref, wg_ref, cw_ref, cb_ref, wd_ref, init_ref,
         o_ref, hl_ref, carry_scr, act_scr) = refs
    i = pl.program_id(0)

    @pl.when(i % tiles_per_seq == 0)
    def _():
        carry_scr[...] = init_ref[...]

    x = x_ref[...]
    if proj:
        a = jnp.concatenate([a_ref[p] for p in range(N_PAIRS)], axis=1)
        x = x + _dot(a, wa_ref[...])
    u = _rmsnorm(x, g_ref[...]).astype(BF16)
    tm = x.shape[0]
    row8 = lax.broadcasted_iota(jnp.int32, (SUBLANES, 1), 0)
    out = x
    for c in range(D_FF // FFN_FC):
        cs = slice(c * FFN_FC, (c + 1) * FFN_FC)
        h = _dot(u, wu_ref[:, cs])
        gt = _dot(u, wg_ref[:, cs])
        cw = cw_ref[:, cs]
        cb = cb_ref[:, cs]
        carry = carry_scr[:, cs]
        p1 = pltpu.roll(h, 1, axis=0)
        p2 = pltpu.roll(h, 2, axis=0)
        act_scr[...] = _conv_act(h, gt, p1, p2, cw, cb)
        c1 = pltpu.roll(carry, 1, axis=0)
        c2 = pltpu.roll(carry, 2, axis=0)
        p1h = jnp.where(row8 < 1, c1, p1[:SUBLANES])
        p2h = jnp.where(row8 < 2, c2, p2[:SUBLANES])
        act_scr[0:SUBLANES, :] = _conv_act(h[:SUBLANES], gt[:SUBLANES], p1h, p2h, cw, cb)
        last = h[tm - SUBLANES:, :]
        carry_scr[:, cs] = last
        hl_ref[0, :, cs] = last
        out = out + _dot(act_scr[...], wd_ref[cs, :])
    o_ref[...] = out


def _ffn(x, a, wa, g, wu, wg, cw, cb, wd, init):
    rows = x.shape[0]
    tm = FFN_TM
    proj = a is not None
    row_spec = pl.BlockSpec((tm, D_MODEL), lambda i: (i, 0))
    in_specs = [row_spec]
    args = [x]
    if proj:
        in_specs += [pl.BlockSpec((N_PAIRS, tm, LANES), lambda i: (0, i, 0)), _const_spec((D_MODEL, D_MODEL))]
        args += [a, wa]
    in_specs += [_const_spec((1, D_MODEL)), _const_spec((D_MODEL, D_FF)), _const_spec((D_MODEL, D_FF)),
                 _const_spec((SUBLANES, D_FF)), _const_spec((1, D_FF)), _const_spec((D_FF, D_MODEL)),
                 _const_spec((SUBLANES, D_FF))]
    args += [g, wu, wg, cw, cb, wd, init]
    return pl.pallas_call(
        functools.partial(_ffn_kernel, proj=proj, tiles_per_seq=SEQ // tm),
        grid=(rows // tm,),
        in_specs=in_specs,
        out_specs=(row_spec, pl.BlockSpec((1, SUBLANES, D_FF), lambda i: (i, 0, 0))),
        out_shape=(jax.ShapeDtypeStruct((rows, D_MODEL), F32),
                   jax.ShapeDtypeStruct((rows // tm, SUBLANES, D_FF), F32)),
        scratch_shapes=[pltpu.VMEM((SUBLANES, D_FF), F32), pltpu.VMEM((tm, FFN_FC), BF16)],
        compiler_params=_cparams(("arbitrary",)),
        name="ffn_proj" if proj else "ffn",
    )(*args)


def _ffn_small_kernel(*refs, proj):
    if proj:
        (x_ref, a_ref, wa_ref, g_ref, wu_ref, wg_ref, cw_ref, cb_ref, wd_ref, b0_ref, b1_ref,
         o_ref, h_ref) = refs
    else:
        (x_ref, g_ref, wu_ref, wg_ref, cw_ref, cb_ref, wd_ref, b0_ref, b1_ref, o_ref, h_ref) = refs
    x = x_ref[...]
    if proj:
        x = x + _dot(a_ref[...], wa_ref[...])
    u = _rmsnorm(x, g_ref[...]).astype(BF16)
    row = lax.broadcasted_iota(jnp.int32, (SMALL_ROWS, 1), 0)
    is_sample = row < N_SAMPLE
    meta_end = META_ROW0 + N_META
    out = x
    for c in range(D_FF // FFN_FC):
        cs = slice(c * FFN_FC, (c + 1) * FFN_FC)
        h = _dot(u, wu_ref[:, cs])
        gt = _dot(u, wg_ref[:, cs])
        p1 = jnp.where(is_sample, b1_ref[:, cs],
                       jnp.where((row >= META_ROW0 + 1) & (row < meta_end), pltpu.roll(h, 1, axis=0), 0.0))
        p2 = jnp.where(is_sample, b0_ref[:, cs],
                       jnp.where((row >= META_ROW0 + 2) & (row < meta_end), pltpu.roll(h, 2, axis=0), 0.0))
        act = _conv_act(h, gt, p1, p2, cw_ref[:, cs], cb_ref[:, cs])
        h_ref[:, cs] = h
        out = out + _dot(act, wd_ref[cs, :])
    o_ref[...] = out


def _ffn_small(x, a, wa, g, wu, wg, cw, cb, wd, b0, b1):
    proj = a is not None
    full = lambda shape: pl.BlockSpec(shape, lambda i: (0,) * len(shape))
    in_specs = [full((SMALL_ROWS, D_MODEL))]
    args = [x]
    if proj:
        in_specs += [full((SMALL_ROWS, D_MODEL)), full((D_MODEL, D_MODEL))]
        args += [a, wa]
    in_specs += [full((1, D_MODEL)), full((D_MODEL, D_FF)), full((D_MODEL, D_FF)), full((SUBLANES, D_FF)),
                 full((1, D_FF)), full((D_FF, D_MODEL)), full((SMALL_ROWS, D_FF)), full((SMALL_ROWS, D_FF))]
    args += [g, wu, wg, cw, cb, wd, b0, b1]
    return pl.pallas_call(
        functools.partial(_ffn_small_kernel, proj=proj),
        grid=(1,),
        in_specs=in_specs,
        out_specs=(full((SMALL_ROWS, D_MODEL)), full((SMALL_ROWS, D_FF))),
        out_shape=(jax.ShapeDtypeStruct((SMALL_ROWS, D_MODEL), F32),
                   jax.ShapeDtypeStruct((SMALL_ROWS, D_FF), F32)),
        compiler_params=_cparams(("arbitrary",)),
        name="ffn_small_proj" if proj else "ffn_small",
    )(*args)


def _qkv_kernel(x_ref, g_ref, w_ref, wf_ref, bf_ref, qg_ref, kg_ref, seg_ref, exp_ref, tri_ref,
                q_ref, k_ref, v_ref, kb_ref, vb_ref, lf_ref, ce_ref, ct_ref, carry_scr, *, tiles_per_seq):
    i = pl.program_id(0)

    @pl.when(i % tiles_per_seq == 0)
    def _():
        carry_scr[...] = jnp.zeros_like(carry_scr)

    u = _rmsnorm(x_ref[...], g_ref[...]).astype(BF16)
    qkv = _dot(u, w_ref[...])
    seg = seg_ref[...]
    expand = exp_ref[...]

    def head_norm(t, gain):
        ms = _dot_split(_split2(t * t), seg)
        rinv = lax.rsqrt(ms + EPS)
        return t * _dot_split(_split2(rinv), expand) * gain

    q = head_norm(qkv[:, :D_MODEL], qg_ref[...])
    k = head_norm(qkv[:, D_MODEL:2 * D_MODEL], kg_ref[...])
    v = qkv[:, 2 * D_MODEL:]
    k_ref[...] = k
    v_ref[...] = v
    qb = q.astype(BF16)
    kb = k.astype(BF16)
    vb = v.astype(BF16)
    for p in range(N_PAIRS):
        ps = slice(p * LANES, (p + 1) * LANES)
        q_ref[p] = qb[:, ps]
        kb_ref[p] = kb[:, ps]
        vb_ref[p] = vb[:, ps]

    z = _dot(u, wf_ref[...]) + bf_ref[...]
    lf = jnp.minimum(z, 0.0) - jnp.log(1.0 + jnp.exp(-jnp.abs(z)))
    lf_ref[...] = lf
    tri = tri_ref[...]
    c = carry_scr[SUBLANES - 1:SUBLANES, :]
    for part in _split3(lf):
        c = c + _dot(tri, part)
    tm = c.shape[0]
    carry_scr[...] = c[tm - SUBLANES:, :]
    ct_ref[...] = c.T
    ce = _dot_split(_split3(c), expand)
    for p in range(N_PAIRS):
        ce_ref[p] = ce[:, p * LANES:(p + 1) * LANES]


def _qkv(x, g, w_qkv, wf, bf, qg, kg, seg, expand, tri, *, tm):
    rows = x.shape[0]
    row_spec = pl.BlockSpec((tm, D_MODEL), lambda i: (i, 0))
    pair_spec = pl.BlockSpec((N_PAIRS, tm, LANES), lambda i: (0, i, 0))
    pair_shape = lambda dt: jax.ShapeDtypeStruct((N_PAIRS, rows, LANES), dt)
    return pl.pallas_call(
        functools.partial(_qkv_kernel, tiles_per_seq=max(SEQ // tm, 1)),
        grid=(rows // tm,),
        in_specs=[row_spec, _const_spec((1, D_MODEL)), _const_spec((D_MODEL, 3 * D_MODEL)),
                  _const_spec((D_MODEL, LANES)), _const_spec((1, LANES)), _const_spec((1, D_MODEL)),
                  _const_spec((1, D_MODEL)), _const_spec((D_MODEL, LANES)), _const_spec((LANES, D_MODEL)),
                  _const_spec((tm, tm))],
        out_specs=(pair_spec, row_spec, row_spec, pair_spec, pair_spec,
                   pl.BlockSpec((tm, LANES), lambda i: (i, 0)), pair_spec,
                   pl.BlockSpec((LANES, tm), lambda i: (0, i))),
        out_shape=(pair_shape(BF16), jax.ShapeDtypeStruct((rows, D_MODEL), F32),
                   jax.ShapeDtypeStruct((rows, D_MODEL), F32), pair_shape(BF16), pair_shape(BF16),
                   jax.ShapeDtypeStruct((rows, LANES), F32), pair_shape(F32),
                   jax.ShapeDtypeStruct((LANES, rows), F32)),
        scratch_shapes=[pltpu.VMEM((SUBLANES, LANES), F32)],
        compiler_params=_cparams(("arbitrary",)),
        name="qkv_proj",
    )(x, g, w_qkv, wf, bf, qg, kg, seg, expand, tri)


ATT_T = 256


def _attn_kernel(q_ref, k_ref, v_ref, ce_ref, ct_ref, km_ref, vm_ref, cmt_ref, ctot_ref, o_ref):
    lane = lax.broadcasted_iota(jnp.int32, (ATT_T, LANES), 1)
    row = lax.broadcasted_iota(jnp.int32, (ATT_T, ATT_T), 0)
    col = lax.broadcasted_iota(jnp.int32, (ATT_T, ATT_T), 1)
    km = km_ref[0]
    vm = vm_ref[0]

    def qblock(qi, _):
        r0 = pl.multiple_of(qi * ATT_T, ATT_T)
        qp = q_ref[0, pl.ds(r0, ATT_T), :]
        ce = ce_ref[0, pl.ds(r0, ATT_T), :]
        kd = k_ref[0, pl.ds(r0, ATT_T), :]
        vd = v_ref[0, pl.ds(r0, ATT_T), :]
        outs = []
        for half in range(2):
            l0 = half * HEAD_DIM
            sel = (lane >= HEAD_DIM) if half else (lane < HEAD_DIM)
            qm = jnp.where(sel, qp, jnp.zeros_like(qp))
            cq = ce[:, l0:l0 + 1]
            ctot = ctot_ref[0, 0:1, l0:l0 + 1]
            s = _dot_nt(qm, km) + ((cq + ctot) - cmt_ref[0, half:half + 1, :])
            m = jnp.max(s, axis=1, keepdims=True)
            p = jnp.exp(s - m)
            l = jnp.sum(p, axis=1, keepdims=True)
            acc = _dot(p.astype(BF16), vm)

            def update(s, vb, carry):
                m, l, acc = carry
                m_new = jnp.maximum(m, jnp.max(s, axis=1, keepdims=True))
                alpha = jnp.exp(m - m_new)
                p = jnp.exp(s - m_new)
                l = alpha * l + jnp.sum(p, axis=1, keepdims=True)
                acc = alpha * acc + _dot(p.astype(BF16), vb)
                return m_new, l, acc

            def kvstep(j, carry):
                c0 = pl.multiple_of(j * ATT_T, ATT_T)
                kb = k_ref[0, pl.ds(c0, ATT_T), :]
                vb = v_ref[0, pl.ds(c0, ATT_T), :]
                s = _dot_nt(qm, kb) + (cq - ct_ref[0, 0, j, half:half + 1, :])
                return update(s, vb, carry)

            carry = lax.fori_loop(0, qi, kvstep, (m, l, acc))
            s = _dot_nt(qm, kd) + (cq - ct_ref[0, 0, qi, half:half + 1, :])
            s = jnp.where(col <= row, s, NEG_INF)
            m, l, acc = update(s, vd, carry)
            outs.append(acc / l)
        o = jnp.where(lane < HEAD_DIM, outs[0], outs[1])
        o_ref[0, pl.ds(r0, ATT_T), :] = o.astype(BF16)
        return 0

    lax.fori_loop(0, SEQ // ATT_T, qblock, 0)


def _attn(q, kb, vb, ce, ct, km, vm, cmt, ctot):
    nblk = SEQ // ATT_T
    seq_spec = pl.BlockSpec((1, SEQ, LANES), lambda n, p: (p, n, 0))
    return pl.pallas_call(
        _attn_kernel,
        grid=(N_BATCH, N_PAIRS),
        in_specs=[seq_spec, seq_spec, seq_spec, seq_spec,
                  pl.BlockSpec((1, 1, nblk, SUBLANES, ATT_T), lambda n, p: (n, p, 0, 0, 0)),
                  pl.BlockSpec((1, N_META, LANES), lambda n, p: (p, 0, 0)),
                  pl.BlockSpec((1, N_META, LANES), lambda n, p: (p, 0, 0)),
                  pl.BlockSpec((1, SUBLANES, N_META), lambda n, p: (p, 0, 0)),
                  pl.BlockSpec((1, SUBLANES, LANES), lambda n, p: (p, 0, 0))],
        out_specs=seq_spec,
        out_shape=jax.ShapeDtypeStruct((N_PAIRS, N_BATCH * SEQ, LANES), BF16),
        compiler_params=_cparams(("parallel", "parallel")),
        name="attn_prompt",
    )(q, kb, vb, ce, ct, km, vm, cmt, ctot)


def _meta_attn_kernel(q_ref, k_ref, v_ref, ce_ref, cmt_ref, o_ref):
    lane = lax.broadcasted_iota(jnp.int32, (N_META, LANES), 1)
    row = lax.broadcasted_iota(jnp.int32, (N_META, N_META), 0)
    col = lax.broadcasted_iota(jnp.int32, (N_META, N_META), 1)
    for p in range(N_PAIRS):
        qp = q_ref[p]
        outs = []
        for half in range(2):
            l0 = half * HEAD_DIM
            sel = (lane >= HEAD_DIM) if half else (lane < HEAD_DIM)
            qm = jnp.where(sel, qp, jnp.zeros_like(qp))
            s = _dot_nt(qm, k_ref[p]) + (ce_ref[p][:, l0:l0 + 1] - cmt_ref[p, half:half + 1, :])
            s = jnp.where(col <= row, s, NEG_INF)
            m = jnp.max(s, axis=1, keepdims=True)
            e = jnp.exp(s - m)
            outs.append(_dot(e.astype(BF16), v_ref[p]) / jnp.sum(e, axis=1, keepdims=True))
        o_ref[p] = jnp.where(lane < HEAD_DIM, outs[0], outs[1]).astype(BF16)


def _meta_attention(q, k, v, ce, cmt):
    full = lambda shape: pl.BlockSpec(shape, lambda i: (0,) * len(shape))
    pm = (N_PAIRS, N_META, LANES)
    o = pl.pallas_call(
        _meta_attn_kernel,
        grid=(1,),
        in_specs=[full(pm), full(pm), full(pm), full(pm), full((N_PAIRS, SUBLANES, N_META))],
        out_specs=full(pm),
        out_shape=jax.ShapeDtypeStruct(pm, BF16),
        name="attn_meta",
    )(q, k, v, ce, cmt)
    return _pairs_to_rows(o)


DEC_PPS = 4


def _decode_kernel(pt_ref, q_ref, kn_ref, vn_ref, lfn_ref, *refs):
    del pt_ref
    k_refs = refs[:DEC_PPS]
    v_refs = refs[DEC_PPS:2 * DEC_PPS]
    lf_refs = refs[2 * DEC_PPS:3 * DEC_PPS]
    seg_ref, exp_ref, upper_ref, o_ref, m_scr, l_scr, tail_scr, acc_scr = refs[3 * DEC_PPS:]
    t = pl.program_id(1)
    seg = seg_ref[...]
    expand = exp_ref[...]
    q = q_ref[0].astype(F32)
    row8 = lax.broadcasted_iota(jnp.int32, (SUBLANES, 1), 0)

    def head_scores(kq):
        return _dot_split(_split2(kq), seg)

    def expand16(x):
        return _dot_split(_split2(x), expand)

    @pl.when(t == 0)
    def _():
        s_new = head_scores(jnp.broadcast_to(kn_ref[0] * q, (SUBLANES, D_MODEL)))
        m_scr[...] = s_new
        l_scr[...] = jnp.ones_like(s_new)
        tail_scr[...] = jnp.broadcast_to(lfn_ref[0][:, :N_HEADS], (SUBLANES, N_HEADS))
        acc_scr[...] = jnp.where(row8 == 0, jnp.broadcast_to(vn_ref[0], (SUBLANES, D_MODEL)), 0.0)

    for i in range(DEC_PPS):
        kp = k_refs[i][0]
        vp = v_refs[i][0]
        lf = lf_refs[i][0]
        m_old = m_scr[...]
        tail = tail_scr[...]
        s = head_scores(kp * q) + tail[0:1, :]
        for part in _split3(lf):
            s = s + _dot(upper_ref[...], part)
        m_new = jnp.maximum(m_old, jnp.max(s, axis=0, keepdims=True))
        alpha = jnp.exp(m_old - m_new)
        p = jnp.exp(s - m_new[0:1, :])
        l_scr[...] = alpha * l_scr[...] + jnp.sum(p, axis=0, keepdims=True)
        m_scr[...] = m_new
        tail_scr[...] = tail + jnp.sum(lf, axis=0, keepdims=True)
        pv = vp * _dot(p.astype(BF16), expand)
        pv8 = pv[0:SUBLANES]
        for r in range(1, PAGE // SUBLANES):
            pv8 = pv8 + pv[r * SUBLANES:(r + 1) * SUBLANES]
        acc_scr[...] = acc_scr[...] * expand16(alpha) + pv8

    @pl.when(t == pl.num_programs(1) - 1)
    def _():
        o = jnp.sum(acc_scr[...], axis=0, keepdims=True) / expand16(l_scr[...])[0:1, :]
        o_ref[0] = o


def _decode(page_table, q, kn, vn, lfn, cache_k, cache_v, cache_lf, seg16, exp16, upper):
    n_pages = page_table.shape[1]
    steps = n_pages // DEC_PPS

    def page_map(i):
        return lambda b, t, pt: (pt[b, n_pages - 1 - (t * DEC_PPS + i)], 0, 0)

    tok = lambda w: pl.BlockSpec((1, 1, w), lambda b, t, pt: (b, 0, 0))
    const = lambda shape: pl.BlockSpec(shape, lambda b, t, pt: (0,) * len(shape))
    in_specs = [tok(D_MODEL), tok(D_MODEL), tok(D_MODEL), tok(LANES)]
    in_specs += [pl.BlockSpec((1, PAGE, D_MODEL), page_map(i)) for i in range(DEC_PPS)]
    in_specs += [pl.BlockSpec((1, PAGE, D_MODEL), page_map(i)) for i in range(DEC_PPS)]
    in_specs += [pl.BlockSpec((1, PAGE, N_HEADS), page_map(i)) for i in range(DEC_PPS)]
    in_specs += [const((D_MODEL, N_HEADS)), const((N_HEADS, D_MODEL)), const((PAGE, PAGE))]
    grid_spec = pltpu.PrefetchScalarGridSpec(
        num_scalar_prefetch=1,
        grid=(N_SAMPLE, steps),
        in_specs=in_specs,
        out_specs=tok(D_MODEL),
        scratch_shapes=[pltpu.VMEM((SUBLANES, N_HEADS), F32), pltpu.VMEM((SUBLANES, N_HEADS), F32),
                        pltpu.VMEM((SUBLANES, N_HEADS), F32), pltpu.VMEM((SUBLANES, D_MODEL), F32)],
    )
    return pl.pallas_call(
        _decode_kernel,
        grid_spec=grid_spec,
        out_shape=jax.ShapeDtypeStruct((N_SAMPLE, 1, D_MODEL), F32),
        compiler_params=_cparams(("parallel", "arbitrary")),
        name="attn_decode",
    )(page_table, q, kn, vn, lfn, *([cache_k] * DEC_PPS), *([cache_v] * DEC_PPS), *([cache_lf] * DEC_PPS),
      seg16, exp16, upper)


def _pairs_to_rows(xp):
    return jnp.transpose(xp, (1, 0, 2)).reshape(xp.shape[1], D_MODEL)


def kernel(x_prompt, x_sample, state_ssm_re, state_ssm_im, cache_k, cache_v, cache_logf, state_conv, page_table, meta_tokens, norm_mix_g, norm_ffn_g, ssm_a_re, ssm_a_im, ssm_b_re, ssm_b_im, ssm_c_re, ssm_c_im, ssm_d, ssm_log_step, ssm_w_glu, attn_w_qkv, attn_w_f, attn_b_f, attn_q_g, attn_k_g, attn_w_o, ffn_w_up, ffn_w_gate, ffn_conv_w, ffn_conv_b, ffn_w_down):
    n_body = N_BATCH * SEQ
    xb = x_prompt.reshape(n_body, D_MODEL)
    xs = jnp.concatenate([x_sample.reshape(N_SAMPLE, D_MODEL), meta_tokens.astype(F32),
                          jnp.zeros((SMALL_ROWS - N_SAMPLE - N_META, D_MODEL), F32)], axis=0)
    meta_rows = slice(META_ROW0, META_ROW0 + N_META)

    g_mix = norm_mix_g.astype(F32).reshape(2, 1, D_MODEL)
    g_ffn = norm_ffn_g.astype(F32).reshape(2, 1, D_MODEL)
    w_glu = ssm_w_glu.astype(BF16)
    w_up = ffn_w_up.astype(BF16)
    w_gate = ffn_w_gate.astype(BF16)
    w_down = ffn_w_down.astype(BF16)
    conv_w = jnp.pad(ffn_conv_w.astype(F32), ((0, 0), (0, SUBLANES - CONV_W), (0, 0)))
    conv_b = ffn_conv_b.astype(F32).reshape(2, 1, D_FF)
    zeros_init = jnp.zeros((SUBLANES, D_FF), F32)

    s5w = _s5_weights(ssm_a_re, ssm_a_im, ssm_b_re, ssm_b_im, ssm_c_re, ssm_c_im, ssm_d, ssm_log_step)
    ub = _norm_shuffle(xb.reshape(n_body // S5_CHUNK, S5_CHUNK * D_MODEL), g_mix[0])
    u_small = _norm_rows(xs, g_mix[0])
    um = u_small[meta_rows].reshape(N_META // S5_CHUNK, S5_CHUNK, S5_SETS, LANES)
    um = jnp.transpose(um, (2, 0, 1, 3)).reshape(S5_SETS, N_META // S5_CHUNK, 1, S5_CHUNK * LANES)
    um = jnp.broadcast_to(um, (S5_SETS, N_META // S5_CHUNK, N_BATCH, S5_CHUNK * LANES))
    um = um.reshape(S5_SETS, N_META, S5_CHUNK * LANES)
    yb, ym, ys, hpr, hpi, hsr, hsi = _s5(
        ub, um, u_small, state_ssm_re.astype(F32).reshape(N_SAMPLE, -1),
        state_ssm_im.astype(F32).reshape(N_SAMPLE, -1), s5w)
    x1b = _glu_res(xb.reshape(n_body // S5_CHUNK, S5_CHUNK * D_MODEL), yb, w_glu, slots=S5_CHUNK, rb=128)
    x1b = x1b.reshape(n_body, D_MODEL)
    ym_tok = ym.reshape(S5_SETS, N_META // S5_CHUNK, N_BATCH, S5_CHUNK, LANES)[:, :, 0]
    ym_tok = jnp.transpose(ym_tok, (1, 2, 0, 3)).reshape(N_META, D_MODEL)
    y_small = jnp.concatenate([ys, ym_tok, jnp.zeros((SMALL_ROWS - N_SAMPLE - N_META, D_MODEL), BF16)], axis=0)
    x1s = _glu_res(xs, y_small[None], w_glu, slots=1, rb=SMALL_ROWS)

    def conv_bufs(i):
        sc = state_conv[i].astype(F32)
        pad = ((0, SMALL_ROWS - N_SAMPLE), (0, 0))
        return jnp.pad(sc[:, 0], pad), jnp.pad(sc[:, 1], pad)

    def ffn_layer(i, xb_in, xs_in, ab, a_small, wa):
        b0, b1 = conv_bufs(i)
        xs_out, h_small = _ffn_small(xs_in, a_small, wa, g_ffn[i], w_up[i], w_gate[i], conv_w[i], conv_b[i],
                                     w_down[i], b0, b1)
        init = jnp.concatenate([jnp.zeros((SUBLANES - 2, D_FF), F32),
                                h_small[META_ROW0 + N_META - 2:META_ROW0 + N_META]], axis=0)
        xb_out, hl = _ffn(xb_in, ab, wa, g_ffn[i], w_up[i], w_gate[i], conv_w[i], conv_b[i], w_down[i], init)
        tps = SEQ // FFN_TM
        conv_p = hl.reshape(N_BATCH, tps, SUBLANES, D_FF)[:, tps - 1, SUBLANES - 2:, :]
        conv_s = jnp.stack([state_conv[i][:, 1].astype(F32), h_small[:N_SAMPLE]], axis=1)
        return xb_out, xs_out, conv_p, conv_s

    x2b, x2s, conv_p0, conv_s0 = ffn_layer(0, x1b, x1s, None, None, None)

    w_qkv = attn_w_qkv.astype(BF16)
    wf = jnp.pad(attn_w_f.astype(BF16), ((0, 0), (0, LANES - N_HEADS)))
    bf = jnp.pad(attn_b_f.astype(F32), (0, LANES - N_HEADS)).reshape(1, LANES)
    qg = (jnp.tile(attn_q_g.astype(F32), N_HEADS) * (HEAD_DIM ** -0.5)).reshape(1, D_MODEL)
    kg = jnp.tile(attn_k_g.astype(F32), N_HEADS).reshape(1, D_MODEL)
    head_of_lane = jnp.arange(D_MODEL) // HEAD_DIM
    onehot = (head_of_lane[:, None] == jnp.arange(LANES)[None, :])
    seg = (onehot.astype(F32) / HEAD_DIM).astype(BF16)
    expand = onehot.T.astype(BF16)
    tri_body = (jnp.arange(FFN_TM)[:, None] >= jnp.arange(FFN_TM)[None, :]).astype(BF16)
    r = jnp.arange(SMALL_ROWS)
    tri_small = ((r[:, None] >= r[None, :]) & (r[None, :] >= META_ROW0) & (r[:, None] < META_ROW0 + N_META))
    tri_small = tri_small.astype(BF16)

    qb_, kb_f32, vb_f32, kbb, vbb, lfb, ceb, ctb = _qkv(x2b, g_mix[1], w_qkv, wf, bf, qg, kg, seg, expand,
                                                        tri_body, tm=FFN_TM)
    qs_, ks_f32, vs_f32, kbs, vbs, lfs, ces, cts = _qkv(x2s, g_mix[1], w_qkv, wf, bf, qg, kg, seg, expand,
                                                        tri_small, tm=SMALL_ROWS)

    nblk = SEQ // ATT_T
    ct = ctb[:N_HEADS].reshape(N_PAIRS, 2, N_BATCH, nblk, ATT_T)
    ct = jnp.transpose(ct, (2, 0, 3, 1, 4))
    ct = jnp.pad(ct, ((0, 0), (0, 0), (0, 0), (0, SUBLANES - 2), (0, 0)))
    cmt = cts[:N_HEADS, meta_rows].reshape(N_PAIRS, 2, N_META)
    cmt = jnp.pad(cmt, ((0, 0), (0, SUBLANES - 2), (0, 0)))
    ctot = jnp.broadcast_to(ces[:, META_ROW0 + N_META - 1:META_ROW0 + N_META, :], (N_PAIRS, SUBLANES, LANES))
    ob = _attn(qb_, kbb, vbb, ceb, ct, kbs[:, meta_rows], vbs[:, meta_rows], cmt, ctot)

    o_meta = _meta_attention(qs_[:, meta_rows], kbs[:, meta_rows], vbs[:, meta_rows], ces[:, meta_rows], cmt)

    q_tok = _pairs_to_rows(qs_)[:N_SAMPLE].reshape(N_SAMPLE, 1, D_MODEL)
    seg16 = onehot[:, :N_HEADS].astype(BF16)
    exp16 = onehot[:, :N_HEADS].T.astype(BF16)
    upper = (jnp.arange(PAGE)[None, :] > jnp.arange(PAGE)[:, None]).astype(BF16)
    n_pool = cache_k.shape[0]
    o_dec = _decode(page_table.astype(jnp.int32), q_tok,
                    ks_f32[:N_SAMPLE].reshape(N_SAMPLE, 1, D_MODEL), vs_f32[:N_SAMPLE].reshape(N_SAMPLE, 1, D_MODEL),
                    lfs[:N_SAMPLE].reshape(N_SAMPLE, 1, LANES),
                    cache_k.reshape(n_pool, PAGE, D_MODEL), cache_v.reshape(n_pool, PAGE, D_MODEL),
                    cache_logf.astype(F32), seg16, exp16, upper)
    o_small = jnp.concatenate([o_dec.reshape(N_SAMPLE, D_MODEL).astype(BF16), o_meta,
                               jnp.zeros((SMALL_ROWS - N_SAMPLE - N_META, D_MODEL), BF16)], axis=0)

    w_o = attn_w_o.astype(BF16)
    x4b, x4s, conv_p1, conv_s1 = ffn_layer(1, x2b, x2s, ob, o_small, w_o)

    def with_meta(small, body, width):
        m = jnp.broadcast_to(small[meta_rows][None], (N_BATCH, N_META, width))
        return jnp.concatenate([m, body.reshape(N_BATCH, SEQ, width)], axis=1)

    k_p = with_meta(ks_f32, kb_f32, D_MODEL).reshape(N_BATCH, SEQ + N_META, N_HEADS, HEAD_DIM)
    v_p = with_meta(vs_f32, vb_f32, D_MODEL).reshape(N_BATCH, SEQ + N_META, N_HEADS, HEAD_DIM)
    lf_p = with_meta(lfs[:, :N_HEADS], lfb[:, :N_HEADS], N_HEADS)
    return (x4b.reshape(N_BATCH, SEQ, D_MODEL),
            x4s[:N_SAMPLE].reshape(N_SAMPLE, 1, D_MODEL),
            hpr.reshape(N_BATCH, N_GROUPS, STATE_P), hpi.reshape(N_BATCH, N_GROUPS, STATE_P),
            hsr.reshape(N_SAMPLE, N_GROUPS, STATE_P), hsi.reshape(N_SAMPLE, N_GROUPS, STATE_P),
            k_p, v_p, lf_p,
            ks_f32[:N_SAMPLE].reshape(N_SAMPLE, 1, N_HEADS, HEAD_DIM),
            vs_f32[:N_SAMPLE].reshape(N_SAMPLE, 1, N_HEADS, HEAD_DIM),
            lfs[:N_SAMPLE, :N_HEADS].reshape(N_SAMPLE, 1, N_HEADS),
            jnp.stack([conv_p0, conv_p1]), jnp.stack([conv_s0, conv_s1]))
```

```python
import functools

import jax
import jax.numpy as jnp
from jax import lax
from jax.experimental import pallas as pl
from jax.experimental.pallas import tpu as pltpu

F32 = jnp.float32
BF16 = jnp.bfloat16

D_MODEL = 1024
N_BATCH = 8
SEQ = 2048
N_SAMPLE = 128
N_META = 16
SMALL_ROWS = 256
META_ROW0 = 128
N_GROUPS = 64
STATE_P = 64
SSM_GROUP = 16
N_HEADS = 16
HEAD_DIM = 64
N_PAIRS = N_HEADS // 2
D_FF = 2816
CONV_W = 3
PAGE = 128
EPS = 1e-6
NEG_INF = -1e30

LANES = 128
SUBLANES = 8
S5_CHUNK = 8
S5_SETS = D_MODEL // LANES
SET_STATE = (LANES // SSM_GROUP) * STATE_P
N_CHUNKS = SEQ // S5_CHUNK
VMEM_LIMIT = 56 * 1024 * 1024


def _cparams(sem):
    return pltpu.CompilerParams(dimension_semantics=sem, vmem_limit_bytes=VMEM_LIMIT)


def _const_spec(shape):
    nd = len(shape)
    return pl.BlockSpec(shape, lambda *_: (0,) * nd, pipeline_mode=pl.Buffered(1))


def _full_spec(shape):
    nd = len(shape)
    return pl.BlockSpec(shape, lambda *_: (0,) * nd)


def _dot(a, b):
    return jnp.dot(a, b, preferred_element_type=F32)


def _dot_f32(a, b):
    return jnp.dot(a, b, preferred_element_type=F32, precision=lax.Precision.HIGHEST)


def _split2(x):
    hi = x.astype(BF16)
    lo = (x - hi.astype(F32)).astype(BF16)
    return hi, lo


def _split3(x):
    hi = x.astype(BF16)
    r = x - hi.astype(F32)
    mid = r.astype(BF16)
    lo = (r - mid.astype(F32)).astype(BF16)
    return hi, mid, lo


def _dot_split(parts, w):
    acc = _dot(parts[0], w)
    for p in parts[1:]:
        acc = acc + _dot(p, w)
    return acc


def _rmsnorm(x, g):
    ms = jnp.mean(x * x, axis=-1, keepdims=True)
    return x * lax.rsqrt(ms + EPS) * g


def _gelu_tanh(x):
    c = 0.7978845608028654
    return 0.5 * x * (1.0 + jnp.tanh(c * (x + 0.044715 * (x * x * x))))


def _sigmoid(x):
    return 1.0 / (1.0 + jnp.exp(-x))


def _norm_shuffle_kernel(x_ref, g_ref, o_ref, scr):
    rb = o_ref.shape[1]
    u = _rmsnorm(x_ref[...], g_ref[...])
    for j in range(S5_SETS):
        scr[j] = u[:, j * LANES:(j + 1) * LANES]
    for j in range(S5_SETS):
        for s in range(S5_CHUNK):
            o_ref[j, :, s * LANES:(s + 1) * LANES] = scr[j, pl.ds(s, rb, stride=S5_CHUNK), :].astype(BF16)


def _norm_shuffle(x, g):
    rows = x.shape[0] // S5_CHUNK
    rb = 128
    return pl.pallas_call(
        _norm_shuffle_kernel,
        grid=(rows // rb,),
        in_specs=[pl.BlockSpec((rb * S5_CHUNK, D_MODEL), lambda i: (i, 0)),
                  _const_spec((1, D_MODEL))],
        out_specs=pl.BlockSpec((S5_SETS, rb, S5_CHUNK * LANES), lambda i: (0, i, 0)),
        out_shape=jax.ShapeDtypeStruct((S5_SETS, rows, S5_CHUNK * LANES), BF16),
        scratch_shapes=[pltpu.VMEM((S5_SETS, rb * S5_CHUNK, LANES), F32)],
        compiler_params=_cparams(("parallel",)),
        name="norm_shuffle",
    )(x, g)


def _norm_rows_kernel(x_ref, g_ref, o_ref):
    o_ref[...] = _rmsnorm(x_ref[...], g_ref[...]).astype(BF16)


def _norm_rows(x, g):
    rows = x.shape[0]
    return pl.pallas_call(
        _norm_rows_kernel,
        grid=(1,),
        in_specs=[_full_spec((rows, D_MODEL)), _full_spec((1, D_MODEL))],
        out_specs=_full_spec((rows, D_MODEL)),
        out_shape=jax.ShapeDtypeStruct((rows, D_MODEL), BF16),
        name="norm_rows",
    )(x, g)


S5_ROW_BLOCK = 512
N_SLABS = SET_STATE // LANES


def _s5_kernel(ub_ref, um_ref, us_ref, h0r_ref, h0i_ref, wt_ref, wb_ref, wc_ref, a8_ref, dt_ref,
               wb1_ref, wc1_ref, a1_ref, d1_ref,
               yb_ref, ym_ref, ys_ref, hpr_ref, hpi_ref, hsr_ref, hsi_ref, xr_scr, xi_scr):
    rows = ub_ref.shape[1]
    wb = wb_ref[0]
    wt = wt_ref[0]
    wc = wc_ref[0]
    dt = dt_ref[0]

    for r in range(rows // S5_ROW_BLOCK):
        rs = slice(r * S5_ROW_BLOCK, (r + 1) * S5_ROW_BLOCK)
        x = _dot(ub_ref[0, rs, :], wb)
        for k in range(N_SLABS):
            xr_scr[k, rs, :] = x[:, k * LANES:(k + 1) * LANES]
            xi_scr[k, rs, :] = x[:, SET_STATE + k * LANES:SET_STATE + (k + 1) * LANES]
    xm = _dot(um_ref[0], wb)

    a8r = jnp.broadcast_to(a8_ref[0, 0:1, :], (SUBLANES, SET_STATE))
    a8i = jnp.broadcast_to(a8_ref[0, 1:2, :], (SUBLANES, SET_STATE))

    hr = jnp.zeros((SUBLANES, SET_STATE), F32)
    hi = jnp.zeros((SUBLANES, SET_STATE), F32)
    hm_prev = []
    for cm in range(N_META // S5_CHUNK):
        hm_prev.append(jnp.concatenate([hr, hi], axis=1))
        xr = xm[cm * SUBLANES:(cm + 1) * SUBLANES, :SET_STATE]
        xi = xm[cm * SUBLANES:(cm + 1) * SUBLANES, SET_STATE:]
        hr, hi = a8r * hr - a8i * hi + xr, a8r * hi + a8i * hr + xi

    ar = [a8r[:, k * LANES:(k + 1) * LANES] for k in range(N_SLABS)]
    ai = [a8i[:, k * LANES:(k + 1) * LANES] for k in range(N_SLABS)]

    def step(c, carry):
        hrs, his = carry
        nr, ni = [], []
        for k in range(N_SLABS):
            idx = (k, pl.ds(c, N_BATCH, stride=N_CHUNKS), slice(None))
            xr = xr_scr[idx]
            xi = xi_scr[idx]
            xr_scr[idx] = hrs[k]
            xi_scr[idx] = his[k]
            nr.append(ar[k] * hrs[k] - ai[k] * his[k] + xr)
            ni.append(ar[k] * his[k] + ai[k] * hrs[k] + xi)
        return tuple(nr), tuple(ni)

    init = (tuple(hr[:, k * LANES:(k + 1) * LANES] for k in range(N_SLABS)),
            tuple(hi[:, k * LANES:(k + 1) * LANES] for k in range(N_SLABS)))
    hrs, his = lax.fori_loop(0, N_CHUNKS, step, init)
    hpr_ref[...] = jnp.concatenate(hrs, axis=1)
    hpi_ref[...] = jnp.concatenate(his, axis=1)

    def emit(u, hprev):
        y = _dot(u, wt) + _dot(hprev.astype(BF16), wc) + dt * u.astype(F32)
        return _gelu_tanh(y).astype(BF16)

    for r in range(rows // S5_ROW_BLOCK):
        rs = slice(r * S5_ROW_BLOCK, (r + 1) * S5_ROW_BLOCK)
        hprev = jnp.concatenate([xr_scr[k, rs, :] for k in range(N_SLABS)]
                                + [xi_scr[k, rs, :] for k in range(N_SLABS)], axis=1)
        yb_ref[0, rs, :] = emit(ub_ref[0, rs, :], hprev)
    ym_ref[0] = emit(um_ref[0], jnp.concatenate(hm_prev, axis=0))

    us = us_ref[...]
    xs = _dot(us, wb1_ref[0])
    a1r = a1_ref[0, 0:1, :]
    a1i = a1_ref[0, 1:2, :]
    h0r = h0r_ref[...]
    h0i = h0i_ref[...]
    hsr = a1r * h0r - a1i * h0i + xs[:, :SET_STATE]
    hsi = a1r * h0i + a1i * h0r + xs[:, SET_STATE:]
    hsr_ref[...] = hsr
    hsi_ref[...] = hsi
    hs = jnp.concatenate([hsr, hsi], axis=1).astype(BF16)
    ys = _dot(hs, wc1_ref[0]) + d1_ref[0] * us.astype(F32)
    ys_ref[...] = _gelu_tanh(ys).astype(BF16)


def _set_spec(shape):
    return pl.BlockSpec((1,) + shape, lambda j: (j,) + (0,) * len(shape))


def _s5(ub, um, u_small, h0r, h0i, w):
    rows = ub.shape[1]
    cw = S5_CHUNK * LANES
    out_shapes = (
        jax.ShapeDtypeStruct((S5_SETS, rows, cw), BF16),
        jax.ShapeDtypeStruct((S5_SETS, N_META, cw), BF16),
        jax.ShapeDtypeStruct((N_SAMPLE, D_MODEL), BF16),
        jax.ShapeDtypeStruct((N_BATCH, S5_SETS * SET_STATE), F32),
        jax.ShapeDtypeStruct((N_BATCH, S5_SETS * SET_STATE), F32),
        jax.ShapeDtypeStruct((N_SAMPLE, S5_SETS * SET_STATE), F32),
        jax.ShapeDtypeStruct((N_SAMPLE, S5_SETS * SET_STATE), F32),
    )
    return pl.pallas_call(
        _s5_kernel,
        grid=(S5_SETS,),
        in_specs=[
            _set_spec((rows, cw)),
            _set_spec((N_META, cw)),
            pl.BlockSpec((N_SAMPLE, LANES), lambda j: (0, j)),
            pl.BlockSpec((N_SAMPLE, SET_STATE), lambda j: (0, j)),
            pl.BlockSpec((N_SAMPLE, SET_STATE), lambda j: (0, j)),
            _set_spec((cw, cw)),
            _set_spec((cw, 2 * SET_STATE)),
            _set_spec((2 * SET_STATE, cw)),
            _set_spec((2, SET_STATE)),
            _set_spec((1, cw)),
            _set_spec((LANES, 2 * SET_STATE)),
            _set_spec((2 * SET_STATE, LANES)),
            _set_spec((2, SET_STATE)),
            _set_spec((1, LANES)),
        ],
        out_specs=(
            _set_spec((rows, cw)),
            _set_spec((N_META, cw)),
            pl.BlockSpec((N_SAMPLE, LANES), lambda j: (0, j)),
            pl.BlockSpec((N_BATCH, SET_STATE), lambda j: (0, j)),
            pl.BlockSpec((N_BATCH, SET_STATE), lambda j: (0, j)),
            pl.BlockSpec((N_SAMPLE, SET_STATE), lambda j: (0, j)),
            pl.BlockSpec((N_SAMPLE, SET_STATE), lambda j: (0, j)),
        ),
        out_shape=out_shapes,
        scratch_shapes=[pltpu.VMEM((N_SLABS, rows, LANES), F32),
                        pltpu.VMEM((N_SLABS, rows, LANES), F32)],
        compiler_params=_cparams(("parallel",)),
        name="s5_mixer",
    )(ub, um, u_small, h0r, h0i, w["wt"], w["wb"], w["wc"], w["a8"], w["dt"],
      w["wb1"], w["wc1"], w["a1"], w["d1"])


def _s5_discretize(a_re, a_im, log_step):
    dt = jnp.exp(log_step)
    mag = jnp.exp(a_re * dt)
    ab_re = mag * jnp.cos(a_im * dt)
    ab_im = mag * jnp.sin(a_im * dt)
    den = a_re * a_re + a_im * a_im
    n_re = ab_re - 1.0
    n_im = ab_im
    f_re = (n_re * a_re + n_im * a_im) / den
    f_im = (n_im * a_re - n_re * a_im) / den
    return ab_re, ab_im, f_re, f_im


def _s5_weights_kernel(arow_ref, acr_ref, aci_ref, acl_ref, br_ref, bi_ref, cr_ref, ci_ref,
                       wt_ref, wb_ref, wc_ref, a8_ref, wb1_ref, wc1_ref, a1_ref):
    arow = arow_ref[0]
    abr, abi, fr, fi = _s5_discretize(arow[0:1], arow[1:2], arow[2:3])
    abr_c, abi_c, _, _ = _s5_discretize(acr_ref[0], aci_ref[0], acl_ref[0])
    br = br_ref[0]
    bi = bi_ref[0]
    b0r = fr * br - fi * bi
    b0i = fr * bi + fi * br
    cr = cr_ref[0]
    ci = ci_ref[0]
    wt_ref[0] = jnp.zeros(wt_ref.shape[1:], BF16)
    pr = jnp.ones_like(abr)
    pi = jnp.zeros_like(abr)
    pcr, pci = abr_c, abi_c
    for k in range(S5_CHUNK):
        bkr = pr * b0r - pi * b0i
        bki = pr * b0i + pi * b0r
        s = S5_CHUNK - 1 - k
        wb_ref[0, s * LANES:(s + 1) * LANES, :SET_STATE] = bkr.astype(BF16)
        wb_ref[0, s * LANES:(s + 1) * LANES, SET_STATE:] = bki.astype(BF16)
        kern = (_dot_f32(bkr, cr) - _dot_f32(bki, ci)).astype(BF16)
        for s2 in range(S5_CHUNK - k):
            t2 = s2 + k
            wt_ref[0, s2 * LANES:(s2 + 1) * LANES, t2 * LANES:(t2 + 1) * LANES] = kern
        wc_ref[0, :SET_STATE, k * LANES:(k + 1) * LANES] = (pcr * cr - pci * ci).astype(BF16)
        wc_ref[0, SET_STATE:, k * LANES:(k + 1) * LANES] = (-(pcr * ci + pci * cr)).astype(BF16)
        pr, pi = pr * abr - pi * abi, pr * abi + pi * abr
        pcr, pci = pcr * abr_c - pci * abi_c, pcr * abi_c + pci * abr_c
    a8_ref[0, 0:1, :] = pr
    a8_ref[0, 1:2, :] = pi
    a1_ref[0, 0:1, :] = abr
    a1_ref[0, 1:2, :] = abi
    wb1_ref[0, :, :SET_STATE] = b0r.astype(BF16)
    wb1_ref[0, :, SET_STATE:] = b0i.astype(BF16)
    wc1_ref[0, :SET_STATE, :] = cr.astype(BF16)
    wc1_ref[0, SET_STATE:, :] = (-ci).astype(BF16)


def _s5_weights(a_re, a_im, b_re, b_im, c_re, c_im, d_skip, log_step):
    gs = LANES // SSM_GROUP
    cw = S5_CHUNK * LANES
    eye = jnp.eye(gs, dtype=F32)
    per_state = lambda v: v.astype(F32).reshape(S5_SETS, SET_STATE)
    a_re_s, a_im_s = per_state(a_re), per_state(a_im)
    ls_s = per_state(jnp.broadcast_to(log_step[:, None], (N_GROUPS, STATE_P)))
    arow = jnp.stack([a_re_s, a_im_s, ls_s] + [jnp.zeros_like(ls_s)] * (SUBLANES - 3), axis=1)
    col = lambda v: jnp.broadcast_to(v[:, :, None], (S5_SETS, SET_STATE, LANES))

    def blockdiag_b(b):
        b = b.astype(F32).reshape(S5_SETS, gs, STATE_P, SSM_GROUP)
        return jnp.einsum('jgpi,gh->jgihp', b, eye).reshape(S5_SETS, LANES, SET_STATE)

    def blockdiag_c(c):
        c = c.astype(F32).reshape(S5_SETS, gs, SSM_GROUP, STATE_P)
        return jnp.einsum('jgop,gh->jgpho', c, eye).reshape(S5_SETS, SET_STATE, LANES)

    out_shapes = (
        jax.ShapeDtypeStruct((S5_SETS, cw, cw), BF16),
        jax.ShapeDtypeStruct((S5_SETS, cw, 2 * SET_STATE), BF16),
        jax.ShapeDtypeStruct((S5_SETS, 2 * SET_STATE, cw), BF16),
        jax.ShapeDtypeStruct((S5_SETS, 2, SET_STATE), F32),
        jax.ShapeDtypeStruct((S5_SETS, LANES, 2 * SET_STATE), BF16),
        jax.ShapeDtypeStruct((S5_SETS, 2 * SET_STATE, LANES), BF16),
        jax.ShapeDtypeStruct((S5_SETS, 2, SET_STATE), F32),
    )
    wt, wb, wc, a8, wb1, wc1, a1 = pl.pallas_call(
        _s5_weights_kernel,
        grid=(S5_SETS,),
        in_specs=[_set_spec((SUBLANES, SET_STATE)), _set_spec((SET_STATE, LANES)), _set_spec((SET_STATE, LANES)),
                  _set_spec((SET_STATE, LANES)), _set_spec((LANES, SET_STATE)), _set_spec((LANES, SET_STATE)),
                  _set_spec((SET_STATE, LANES)), _set_spec((SET_STATE, LANES))],
        out_specs=tuple(_set_spec(s.shape[1:]) for s in out_shapes),
        out_shape=out_shapes,
        compiler_params=_cparams(("parallel",)),
        name="s5_weights",
    )(arow, col(a_re_s), col(a_im_s), col(ls_s), blockdiag_b(b_re), blockdiag_b(b_im),
      blockdiag_c(c_re), blockdiag_c(c_im))
    d1 = d_skip.astype(F32).reshape(S5_SETS, 1, LANES)
    return {"wt": wt, "wb": wb, "wc": wc, "a8": a8, "dt": jnp.tile(d1, (1, 1, S5_CHUNK)),
            "wb1": wb1, "wc1": wc1, "a1": a1, "d1": d1}


def _glu(y, w):
    z = _dot(y, w)
    return z[:, :D_MODEL] * _sigmoid(z[:, D_MODEL:])


def _glu_res_kernel(x_ref, y_ref, w_ref, o_ref, scr):
    rb = y_ref.shape[1]
    w = w_ref[...]
    for s in range(S5_CHUNK):
        y = jnp.concatenate([y_ref[j, :, s * LANES:(s + 1) * LANES] for j in range(S5_SETS)], axis=1)
        o = _glu(y, w)
        for j in range(S5_SETS):
            scr[j, pl.ds(s, rb, stride=S5_CHUNK), :] = o[:, j * LANES:(j + 1) * LANES]
    for j in range(S5_SETS):
        cs = slice(j * LANES, (j + 1) * LANES)
        o_ref[:, cs] = x_ref[:, cs] + scr[j]


def _glu_res(x, y, w_glu):
    rows = y.shape[1]
    rb = 128
    tok_spec = pl.BlockSpec((rb * S5_CHUNK, D_MODEL), lambda i: (i, 0))
    return pl.pallas_call(
        _glu_res_kernel,
        grid=(rows // rb,),
        in_specs=[tok_spec, pl.BlockSpec((S5_SETS, rb, S5_CHUNK * LANES), lambda i: (0, i, 0)),
                  _const_spec((D_MODEL, 2 * D_MODEL))],
        out_specs=tok_spec,
        out_shape=jax.ShapeDtypeStruct(x.shape, F32),
        scratch_shapes=[pltpu.VMEM((S5_SETS, rb * S5_CHUNK, LANES), F32)],
        compiler_params=_cparams(("parallel",)),
        name="glu_res",
    )(x, y, w_glu)


def _glu_res_rows_kernel(x_ref, y_ref, w_ref, o_ref):
    o_ref[...] = x_ref[...] + _glu(y_ref[...], w_ref[...])


def _glu_res_rows(x, y, w_glu):
    rows = x.shape[0]
    return pl.pallas_call(
        _glu_res_rows_kernel,
        grid=(1,),
        in_specs=[_full_spec((rows, D_MODEL)), _full_spec((rows, D_MODEL)), _full_spec((D_MODEL, 2 * D_MODEL))],
        out_specs=_full_spec((rows, D_MODEL)),
        out_shape=jax.ShapeDtypeStruct((rows, D_MODEL), F32),
        name="glu_res_rows",
    )(x, y, w_glu)


FFN_TM = 512
FFN_FC = D_FF // 2


def _conv_act(h, gt, p1, p2, cw, cb):
    hc = cb + cw[0:1, :] * p2
    hc = hc + cw[1:2, :] * p1
    hc = hc + cw[2:3, :] * h
    return (hc * _sigmoid(hc) * gt).astype(BF16)


def _ffn_kernel(*refs, proj, tiles_per_seq):
    if proj:
        (x_ref, a_ref, wa_ref, g_ref, wu_ref, wg_ref, cw_ref, cb_ref, wd_ref, init_ref,
         o_ref, hl_ref, carry_scr, act_scr) = refs
    else:
        (x_ref, g_ref, wu_ref, wg_ref, cw_ref, cb_ref, wd_ref, init_ref,
         o_ref, hl_ref, carry_scr, act_scr) = refs
    i = pl.program_id(0)

    @pl.when(i % tiles_per_seq == 0)
    def _():
        carry_scr[...] = init_ref[...]

    x = x_ref[...]
    if proj:
        a = jnp.concatenate([a_ref[p] for p in range(N_PAIRS)], axis=1)
        x = x + _dot(a, wa_ref[...])
    u = _rmsnorm(x, g_ref[...]).astype(BF16)
    tm = x.shape[0]
    row8 = lax.broadcasted_iota(jnp.int32, (SUBLANES, 1), 0)
    out = x
    for c in range(D_FF // FFN_FC):
        cs = slice(c * FFN_FC, (c + 1) * FFN_FC)
        h = _dot(u, wu_ref[:, cs])
        gt = _dot(u, wg_ref[:, cs])
        cw = cw_ref[:, cs]
        cb = cb_ref[:, cs]
        carry = carry_scr[:, cs]
        p1 = pltpu.roll(h, 1, axis=0)
        p2 = pltpu.roll(h, 2, axis=0)
        act_scr[...] = _conv_act(h, gt, p1, p2, cw, cb)
        c1 = pltpu.roll(carry, 1, axis=0)
        c2 = pltpu.roll(carry, 2, axis=0)
        p1h = jnp.where(row8 < 1, c1, p1[:SUBLANES])
        p2h = jnp.where(row8 < 2, c2, p2[:SUBLANES])
        act_scr[0:SUBLANES, :] = _conv_act(h[:SUBLANES], gt[:SUBLANES], p1h, p2h, cw, cb)
        last = h[tm - SUBLANES:, :]
        carry_scr[:, cs] = last
        hl_ref[0, :, cs] = last
        out = out + _dot(act_scr[...], wd_ref[cs, :])
    o_ref[...] = out


def _ffn(x, a, wa, g, wu, wg, cw, cb, wd, init):
    rows = x.shape[0]
    tm = FFN_TM
    proj = a is not None
    row_spec = pl.BlockSpec((tm, D_MODEL), lambda i: (i, 0))
    in_specs = [row_spec]
    args = [x]
    if proj:
        in_specs += [pl.BlockSpec((N_PAIRS, tm, LANES), lambda i: (0, i, 0)), _const_spec((D_MODEL, D_MODEL))]
        args += [a, wa]
    in_specs += [_const_spec((1, D_MODEL)), _const_spec((D_MODEL, D_FF)), _const_spec((D_MODEL, D_FF)),
                 _const_spec((SUBLANES, D_FF)), _const_spec((1, D_FF)), _const_spec((D_FF, D_MODEL)),
                 _const_spec((SUBLANES, D_FF))]
    args += [g, wu, wg, cw, cb, wd, init]
    return pl.pallas_call(
        functools.partial(_ffn_kernel, proj=proj, tiles_per_seq=SEQ // tm),
        grid=(rows // tm,),
        in_specs=in_specs,
        out_specs=(row_spec, pl.BlockSpec((1, SUBLANES, D_FF), lambda i: (i, 0, 0))),
        out_shape=(jax.ShapeDtypeStruct((rows, D_MODEL), F32),
                   jax.ShapeDtypeStruct((rows // tm, SUBLANES, D_FF), F32)),
        scratch_shapes=[pltpu.VMEM((SUBLANES, D_FF), F32), pltpu.VMEM((tm, FFN_FC), BF16)],
        compiler_params=_cparams(("arbitrary",)),
        name="ffn_proj" if proj else "ffn",
    )(*args)


def _ffn_small_kernel(*refs, proj):
    if proj:
        (x_ref, a_ref, wa_ref, g_ref, wu_ref, wg_ref, cw_ref, cb_ref, wd_ref, b0_ref, b1_ref,
         o_ref, h_ref) = refs
    else:
        (x_ref, g_ref, wu_ref, wg_ref, cw_ref, cb_ref, wd_ref, b0_ref, b1_ref, o_ref, h_ref) = refs
    x = x_ref[...]
    if proj:
        x = x + _dot(a_ref[...], wa_ref[...])
    u = _rmsnorm(x, g_ref[...]).astype(BF16)
    row = lax.broadcasted_iota(jnp.int32, (SMALL_ROWS, 1), 0)
    is_sample = row < N_SAMPLE
    meta_end = META_ROW0 + N_META
    out = x
    for c in range(D_FF // FFN_FC):
        cs = slice(c * FFN_FC, (c + 1) * FFN_FC)
        h = _dot(u, wu_ref[:, cs])
        gt = _dot(u, wg_ref[:, cs])
        p1 = jnp.where(is_sample, b1_ref[:, cs],
                       jnp.where((row >= META_ROW0 + 1) & (row < meta_end), pltpu.roll(h, 1, axis=0), 0.0))
        p2 = jnp.where(is_sample, b0_ref[:, cs],
                       jnp.where((row >= META_ROW0 + 2) & (row < meta_end), pltpu.roll(h, 2, axis=0), 0.0))
        act = _conv_act(h, gt, p1, p2, cw_ref[:, cs], cb_ref[:, cs])
        h_ref[:, cs] = h
        out = out + _dot(act, wd_ref[cs, :])
    o_ref[...] = out


def _ffn_small(x, a, wa, g, wu, wg, cw, cb, wd, b0, b1):
    proj = a is not None
    in_specs = [_full_spec((SMALL_ROWS, D_MODEL))]
    args = [x]
    if proj:
        in_specs += [_full_spec((SMALL_ROWS, D_MODEL)), _full_spec((D_MODEL, D_MODEL))]
        args += [a, wa]
    in_specs += [_full_spec((1, D_MODEL)), _full_spec((D_MODEL, D_FF)), _full_spec((D_MODEL, D_FF)),
                 _full_spec((SUBLANES, D_FF)), _full_spec((1, D_FF)), _full_spec((D_FF, D_MODEL)),
                 _full_spec((SMALL_ROWS, D_FF)), _full_spec((SMALL_ROWS, D_FF))]
    args += [g, wu, wg, cw, cb, wd, b0, b1]
    return pl.pallas_call(
        functools.partial(_ffn_small_kernel, proj=proj),
        grid=(1,),
        in_specs=in_specs,
        out_specs=(_full_spec((SMALL_ROWS, D_MODEL)), _full_spec((SMALL_ROWS, D_FF))),
        out_shape=(jax.ShapeDtypeStruct((SMALL_ROWS, D_MODEL), F32),
                   jax.ShapeDtypeStruct((SMALL_ROWS, D_FF), F32)),
        compiler_params=_cparams(("arbitrary",)),
        name="ffn_small_proj" if proj else "ffn_small",
    )(*args)


ATT_T = 256


def _qkv_kernel(x_ref, g_ref, w_ref, wf_ref, bf_ref, qg_ref, kg_ref, seg_ref, exp_ref, tri_ref,
                q_ref, k_ref, v_ref, kt_ref, vb_ref, lf_ref, ce_ref, ct_ref, carry_scr, *, tiles_per_seq):
    i = pl.program_id(0)

    @pl.when(i % tiles_per_seq == 0)
    def _():
        carry_scr[...] = jnp.zeros_like(carry_scr)

    u = _rmsnorm(x_ref[...], g_ref[...]).astype(BF16)
    qkv = _dot(u, w_ref[...])
    seg = seg_ref[...]
    expand = exp_ref[...]

    def head_norm(t, gain):
        ms = _dot_split(_split2(t * t), seg)
        rinv = lax.rsqrt(ms + EPS)
        return t * _dot_split(_split2(rinv), expand) * gain

    q = head_norm(qkv[:, :D_MODEL], qg_ref[...])
    k = head_norm(qkv[:, D_MODEL:2 * D_MODEL], kg_ref[...])
    v = qkv[:, 2 * D_MODEL:]
    k_ref[...] = k
    v_ref[...] = v
    qb = q.astype(BF16)
    vb = v.astype(BF16)
    tm = k.shape[0]
    for p in range(N_PAIRS):
        ps = slice(p * LANES, (p + 1) * LANES)
        q_ref[p] = qb[:, ps]
        vb_ref[p] = vb[:, ps]
        for b in range(tm // ATT_T):
            kt_ref[p, b] = k[b * ATT_T:(b + 1) * ATT_T, ps].T.astype(BF16)

    z = _dot(u, wf_ref[...]) + bf_ref[...]
    lf = jnp.minimum(z, 0.0) - jnp.log(1.0 + jnp.exp(-jnp.abs(z)))
    lf_ref[...] = lf
    tri = tri_ref[...]
    c = carry_scr[SUBLANES - 1:SUBLANES, :]
    for part in _split3(lf):
        c = c + _dot(tri, part)
    carry_scr[...] = c[tm - SUBLANES:, :]
    ct_ref[...] = c.T
    ce = _dot_split(_split3(c), expand)
    for p in range(N_PAIRS):
        ce_ref[p] = ce[:, p * LANES:(p + 1) * LANES]


def _qkv(x, g, w_qkv, wf, bf, qg, kg, seg, expand, tri, *, tm):
    rows = x.shape[0]
    row_spec = pl.BlockSpec((tm, D_MODEL), lambda i: (i, 0))
    pair_spec = pl.BlockSpec((N_PAIRS, tm, LANES), lambda i: (0, i, 0))
    pair_shape = lambda dt: jax.ShapeDtypeStruct((N_PAIRS, rows, LANES), dt)
    return pl.pallas_call(
        functools.partial(_qkv_kernel, tiles_per_seq=max(SEQ // tm, 1)),
        grid=(rows // tm,),
        in_specs=[row_spec, _const_spec((1, D_MODEL)), _const_spec((D_MODEL, 3 * D_MODEL)),
                  _const_spec((D_MODEL, LANES)), _const_spec((1, LANES)), _const_spec((1, D_MODEL)),
                  _const_spec((1, D_MODEL)), _const_spec((D_MODEL, LANES)), _const_spec((LANES, D_MODEL)),
                  _const_spec((tm, tm))],
        out_specs=(pair_spec, row_spec, row_spec,
                   pl.BlockSpec((N_PAIRS, tm // ATT_T, LANES, ATT_T), lambda i: (0, i, 0, 0)),
                   pair_spec, pl.BlockSpec((tm, LANES), lambda i: (i, 0)), pair_spec,
                   pl.BlockSpec((LANES, tm), lambda i: (0, i))),
        out_shape=(pair_shape(BF16), jax.ShapeDtypeStruct((rows, D_MODEL), F32),
                   jax.ShapeDtypeStruct((rows, D_MODEL), F32),
                   jax.ShapeDtypeStruct((N_PAIRS, rows // ATT_T, LANES, ATT_T), BF16),
                   pair_shape(BF16), jax.ShapeDtypeStruct((rows, LANES), F32), pair_shape(F32),
                   jax.ShapeDtypeStruct((LANES, rows), F32)),
        scratch_shapes=[pltpu.VMEM((SUBLANES, LANES), F32)],
        compiler_params=_cparams(("arbitrary",)),
        name="qkv_proj",
    )(x, g, w_qkv, wf, bf, qg, kg, seg, expand, tri)


def _head_masks(rows):
    lane = lax.broadcasted_iota(jnp.int32, (rows, LANES), 1)
    return lane < HEAD_DIM, lane >= HEAD_DIM


def _attn_kernel(q_ref, kt_ref, v_ref, ce_ref, ct_ref, kmt_ref, vm_ref, cmt_ref, ctot_ref, o_ref):
    sel = _head_masks(ATT_T)
    row = lax.broadcasted_iota(jnp.int32, (ATT_T, ATT_T), 0)
    col = lax.broadcasted_iota(jnp.int32, (ATT_T, ATT_T), 1)
    kmt = kmt_ref[0]
    vm = vm_ref[0]

    def update(s, vb, m, l, acc):
        m_new = jnp.maximum(m, jnp.max(s, axis=1, keepdims=True))
        alpha = jnp.exp(m - m_new)
        p = jnp.exp(s - m_new)
        l = alpha * l + jnp.sum(p, axis=1, keepdims=True)
        acc = alpha * acc + _dot(p.astype(BF16), vb)
        return m_new, l, acc

    def qblock(qi, _):
        r0 = pl.multiple_of(qi * ATT_T, ATT_T)
        qp = q_ref[0, pl.ds(r0, ATT_T), :]
        ce = ce_ref[0, pl.ds(r0, ATT_T), :]
        qm = [jnp.where(sel[h], qp, jnp.zeros_like(qp)) for h in range(2)]
        cq = [ce[:, h * HEAD_DIM:h * HEAD_DIM + 1] for h in range(2)]
        state = []
        for h in range(2):
            ctot = ctot_ref[0, 0:1, h * HEAD_DIM:h * HEAD_DIM + 1]
            s = _dot(qm[h], kmt) + ((cq[h] + ctot) - cmt_ref[0, h:h + 1, :])
            m = jnp.max(s, axis=1, keepdims=True)
            p = jnp.exp(s - m)
            state += [m, jnp.sum(p, axis=1, keepdims=True), _dot(p.astype(BF16), vm)]

        def kvstep(j, st):
            c0 = pl.multiple_of(j * ATT_T, ATT_T)
            kt = kt_ref[0, j]
            vb = v_ref[0, pl.ds(c0, ATT_T), :]
            ctj = ct_ref[0, 0, j]
            new = []
            for h in range(2):
                s = _dot(qm[h], kt) + (cq[h] - ctj[h:h + 1, :])
                new += update(s, vb, *st[3 * h:3 * h + 3])
            return tuple(new)

        st = lax.fori_loop(0, qi, kvstep, tuple(state))
        kt = kt_ref[0, qi]
        vd = v_ref[0, pl.ds(r0, ATT_T), :]
        ctd = ct_ref[0, 0, qi]
        outs = []
        for h in range(2):
            s = _dot(qm[h], kt) + (cq[h] - ctd[h:h + 1, :])
            s = jnp.where(col <= row, s, NEG_INF)
            m, l, acc = update(s, vd, *st[3 * h:3 * h + 3])
            outs.append(acc / l)
        o_ref[0, pl.ds(r0, ATT_T), :] = jnp.where(sel[0], outs[0], outs[1]).astype(BF16)
        return 0

    lax.fori_loop(0, SEQ // ATT_T, qblock, 0)


def _attn(q, kt, vb, ce, ct, kmt, vm, cmt, ctot):
    nblk = SEQ // ATT_T
    seq_spec = pl.BlockSpec((1, SEQ, LANES), lambda n, p: (p, n, 0))
    return pl.pallas_call(
        _attn_kernel,
        grid=(N_BATCH, N_PAIRS),
        in_specs=[seq_spec,
                  pl.BlockSpec((1, nblk, LANES, ATT_T), lambda n, p: (p, n, 0, 0)),
                  seq_spec, seq_spec,
                  pl.BlockSpec((1, 1, nblk, SUBLANES, ATT_T), lambda n, p: (n, p, 0, 0, 0)),
                  pl.BlockSpec((1, LANES, N_META), lambda n, p: (p, 0, 0)),
                  pl.BlockSpec((1, N_META, LANES), lambda n, p: (p, 0, 0)),
                  pl.BlockSpec((1, SUBLANES, N_META), lambda n, p: (p, 0, 0)),
                  pl.BlockSpec((1, SUBLANES, LANES), lambda n, p: (p, 0, 0))],
        out_specs=seq_spec,
        out_shape=jax.ShapeDtypeStruct((N_PAIRS, N_BATCH * SEQ, LANES), BF16),
        compiler_params=_cparams(("parallel", "parallel")),
        name="attn_prompt",
    )(q, kt, vb, ce, ct, kmt, vm, cmt, ctot)


def _meta_attn_kernel(q_ref, kt_ref, v_ref, ce_ref, cmt_ref, o_ref):
    sel = _head_masks(N_META)
    row = lax.broadcasted_iota(jnp.int32, (N_META, N_META), 0)
    col = lax.broadcasted_iota(jnp.int32, (N_META, N_META), 1)
    for p in range(N_PAIRS):
        qp = q_ref[p]
        outs = []
        for h in range(2):
            l0 = h * HEAD_DIM
            qm = jnp.where(sel[h], qp, jnp.zeros_like(qp))
            s = _dot(qm, kt_ref[p]) + (ce_ref[p][:, l0:l0 + 1] - cmt_ref[p, h:h + 1, :])
            s = jnp.where(col <= row, s, NEG_INF)
            m = jnp.max(s, axis=1, keepdims=True)
            e = jnp.exp(s - m)
            outs.append(_dot(e.astype(BF16), v_ref[p]) / jnp.sum(e, axis=1, keepdims=True))
        o_ref[p] = jnp.where(sel[0], outs[0], outs[1]).astype(BF16)


def _meta_attention(q, kt, v, ce, cmt):
    pm = (N_PAIRS, N_META, LANES)
    o = pl.pallas_call(
        _meta_attn_kernel,
        grid=(1,),
        in_specs=[_full_spec(pm), _full_spec((N_PAIRS, LANES, N_META)), _full_spec(pm), _full_spec(pm),
                  _full_spec((N_PAIRS, SUBLANES, N_META))],
        out_specs=_full_spec(pm),
        out_shape=jax.ShapeDtypeStruct(pm, BF16),
        name="attn_meta",
    )(q, kt, v, ce, cmt)
    return _pairs_to_rows(o)


DEC_PPS = 8


def _per_head_rows(x):
    return jnp.broadcast_to(x[:, None, :], (N_HEADS, HEAD_DIM, LANES)).reshape(D_MODEL, LANES)


def _decode_kernel(pt_ref, qt_ref, *refs):
    del pt_ref
    k_refs = refs[:DEC_PPS]
    v_refs = refs[DEC_PPS:2 * DEC_PPS]
    lf_refs = refs[2 * DEC_PPS:3 * DEC_PPS]
    low_ref, o_ref, m_ref, l_ref, qc_scr, m_scr, l_scr, tail_scr, acc_scr = refs[3 * DEC_PPS:]
    b = pl.program_id(0)
    t = pl.program_id(1)
    mine = lax.broadcasted_iota(jnp.int32, (1, LANES), 1) == b

    @pl.when((b == 0) & (t == 0))
    def _():
        o_ref[...] = jnp.zeros_like(o_ref)
        m_ref[...] = jnp.zeros_like(m_ref)
        l_ref[...] = jnp.zeros_like(l_ref)

    @pl.when(t == 0)
    def _():
        qcol = jnp.sum(jnp.where(mine, qt_ref[...], 0.0), axis=1, keepdims=True)
        qc_scr[...] = jnp.broadcast_to(qcol, (D_MODEL, LANES))
        m_scr[...] = jnp.full(m_scr.shape, NEG_INF, F32)
        l_scr[...] = jnp.zeros_like(l_scr)
        tail_scr[...] = jnp.zeros_like(tail_scr)
        acc_scr[...] = jnp.zeros_like(acc_scr)

    low = low_ref[...]
    for i in range(DEC_PPS):
        kq = k_refs[i][0] * qc_scr[...]
        s = jnp.sum(kq.reshape(N_HEADS, HEAD_DIM, LANES), axis=1)
        lf = lf_refs[i][0]
        s = s + tail_scr[...]
        for part in _split3(lf):
            s = s + _dot(part, low)
        m_old = m_scr[...]
        m_new = jnp.maximum(m_old, jnp.max(s, axis=1, keepdims=True))
        alpha = jnp.exp(m_old - m_new)
        p = jnp.exp(s - m_new)
        l_scr[...] = alpha * l_scr[...] + jnp.sum(p, axis=1, keepdims=True)
        m_scr[...] = m_new
        tail_scr[...] = tail_scr[...] + jnp.sum(lf, axis=1, keepdims=True)
        acc_scr[...] = acc_scr[...] * _per_head_rows(alpha) + v_refs[i][0] * _per_head_rows(p)

    @pl.when(t == pl.num_programs(1) - 1)
    def _():
        o_ref[...] = jnp.where(mine, jnp.sum(acc_scr[...], axis=1, keepdims=True), o_ref[...])
        m_ref[...] = jnp.where(mine, m_scr[...], m_ref[...])
        l_ref[...] = jnp.where(mine, l_scr[...], l_ref[...])


def _decode(page_table, qt, cache_kt, cache_vt, cache_lft, low):
    n_pages = page_table.shape[1]
    steps = n_pages // DEC_PPS

    def page_map(i):
        return lambda b, t, pt: (pt[b, n_pages - 1 - (t * DEC_PPS + i)], 0, 0)

    const = lambda shape: pl.BlockSpec(shape, lambda b, t, pt: (0,) * len(shape))
    in_specs = [const((D_MODEL, N_SAMPLE))]
    in_specs += [pl.BlockSpec((1, D_MODEL, PAGE), page_map(i)) for i in range(DEC_PPS)]
    in_specs += [pl.BlockSpec((1, D_MODEL, PAGE), page_map(i)) for i in range(DEC_PPS)]
    in_specs += [pl.BlockSpec((1, N_HEADS, PAGE), page_map(i)) for i in range(DEC_PPS)]
    in_specs += [const((PAGE, PAGE))]
    stat = lambda: pltpu.VMEM((N_HEADS, LANES), F32)
    grid_spec = pltpu.PrefetchScalarGridSpec(
        num_scalar_prefetch=1,
        grid=(N_SAMPLE, steps),
        in_specs=in_specs,
        out_specs=(const((D_MODEL, N_SAMPLE)), const((N_HEADS, N_SAMPLE)), const((N_HEADS, N_SAMPLE))),
        scratch_shapes=[pltpu.VMEM((D_MODEL, LANES), F32), stat(), stat(), stat(),
                        pltpu.VMEM((D_MODEL, LANES), F32)],
    )
    return pl.pallas_call(
        _decode_kernel,
        grid_spec=grid_spec,
        out_shape=(jax.ShapeDtypeStruct((D_MODEL, N_SAMPLE), F32),
                   jax.ShapeDtypeStruct((N_HEADS, N_SAMPLE), F32),
                   jax.ShapeDtypeStruct((N_HEADS, N_SAMPLE), F32)),
        compiler_params=_cparams(("arbitrary", "arbitrary")),
        name="attn_decode",
    )(page_table, qt, *([cache_kt] * DEC_PPS), *([cache_vt] * DEC_PPS), *([cache_lft] * DEC_PPS), low)


def _decode_combine_kernel(acc_ref, m_ref, l_ref, q_ref, kn_ref, vn_ref, lfn_ref, seg_ref, exp_ref, o_ref):
    expand = exp_ref[...]
    ex = lambda x: _dot_split(_split2(x), expand)
    s_new = _dot_split(_split2(q_ref[...].astype(F32) * kn_ref[...]), seg_ref[...])
    m_c = m_ref[...] + lfn_ref[...]
    m_all = jnp.maximum(m_c, s_new)
    w_c = jnp.exp(m_c - m_all)
    w_n = jnp.exp(s_new - m_all)
    den = l_ref[...] * w_c + w_n
    o = (acc_ref[...] * ex(w_c) + vn_ref[...] * ex(w_n)) / ex(den)
    o_ref[...] = o.astype(BF16)


def _decode_combine(acc, m, l, q, kn, vn, lfn, seg, expand):
    shapes = [(N_SAMPLE, D_MODEL), (N_SAMPLE, LANES), (N_SAMPLE, LANES), (N_SAMPLE, D_MODEL),
              (N_SAMPLE, D_MODEL), (N_SAMPLE, D_MODEL), (N_SAMPLE, LANES), (D_MODEL, LANES), (LANES, D_MODEL)]
    return pl.pallas_call(
        _decode_combine_kernel,
        grid=(1,),
        in_specs=[_full_spec(s) for s in shapes],
        out_specs=_full_spec((N_SAMPLE, D_MODEL)),
        out_shape=jax.ShapeDtypeStruct((N_SAMPLE, D_MODEL), BF16),
        name="attn_decode_combine",
    )(acc, m, l, q, kn, vn, lfn, seg, expand)


def _pairs_to_rows(xp):
    return jnp.transpose(xp, (1, 0, 2)).reshape(xp.shape[1], D_MODEL)


def kernel(x_prompt, x_sample, state_ssm_re, state_ssm_im, cache_k, cache_v, cache_logf, state_conv, page_table, meta_tokens, norm_mix_g, norm_ffn_g, ssm_a_re, ssm_a_im, ssm_b_re, ssm_b_im, ssm_c_re, ssm_c_im, ssm_d, ssm_log_step, ssm_w_glu, attn_w_qkv, attn_w_f, attn_b_f, attn_q_g, attn_k_g, attn_w_o, ffn_w_up, ffn_w_gate, ffn_conv_w, ffn_conv_b, ffn_w_down):
    n_body = N_BATCH * SEQ
    xb = x_prompt.reshape(n_body, D_MODEL)
    pad_rows = SMALL_ROWS - N_SAMPLE - N_META
    xs = jnp.concatenate([x_sample.reshape(N_SAMPLE, D_MODEL), meta_tokens.astype(F32),
                          jnp.zeros((pad_rows, D_MODEL), F32)], axis=0)
    meta_rows = slice(META_ROW0, META_ROW0 + N_META)

    g_mix = norm_mix_g.astype(F32).reshape(2, 1, D_MODEL)
    g_ffn = norm_ffn_g.astype(F32).reshape(2, 1, D_MODEL)
    w_glu = ssm_w_glu.astype(BF16)
    w_up = ffn_w_up.astype(BF16)
    w_gate = ffn_w_gate.astype(BF16)
    w_down = ffn_w_down.astype(BF16)
    conv_w = jnp.pad(ffn_conv_w.astype(F32), ((0, 0), (0, SUBLANES - CONV_W), (0, 0)))
    conv_b = ffn_conv_b.astype(F32).reshape(2, 1, D_FF)

    s5w = _s5_weights(ssm_a_re, ssm_a_im, ssm_b_re, ssm_b_im, ssm_c_re, ssm_c_im, ssm_d, ssm_log_step)
    ub = _norm_shuffle(xb, g_mix[0])
    u_small = _norm_rows(xs, g_mix[0])
    n_mc = N_META // S5_CHUNK
    um = u_small[meta_rows].reshape(n_mc, S5_CHUNK, S5_SETS, LANES)
    um = jnp.transpose(um, (2, 0, 1, 3)).reshape(S5_SETS, n_mc, 1, S5_CHUNK * LANES)
    um = jnp.broadcast_to(um, (S5_SETS, n_mc, N_BATCH, S5_CHUNK * LANES)).reshape(S5_SETS, N_META, S5_CHUNK * LANES)
    yb, ym, ys, hpr, hpi, hsr, hsi = _s5(
        ub, um, u_small, state_ssm_re.astype(F32).reshape(N_SAMPLE, -1),
        state_ssm_im.astype(F32).reshape(N_SAMPLE, -1), s5w)
    x1b = _glu_res(xb, yb, w_glu)
    ym_tok = ym.reshape(S5_SETS, n_mc, N_BATCH, S5_CHUNK, LANES)[:, :, 0]
    ym_tok = jnp.transpose(ym_tok, (1, 2, 0, 3)).reshape(N_META, D_MODEL)
    y_small = jnp.concatenate([ys, ym_tok, jnp.zeros((pad_rows, D_MODEL), BF16)], axis=0)
    x1s = _glu_res_rows(xs, y_small, w_glu)

    def ffn_layer(i, xb_in, xs_in, ab, a_small, wa):
        sc = state_conv[i].astype(F32)
        pad = ((0, SMALL_ROWS - N_SAMPLE), (0, 0))
        xs_out, h_small = _ffn_small(xs_in, a_small, wa, g_ffn[i], w_up[i], w_gate[i], conv_w[i], conv_b[i],
                                     w_down[i], jnp.pad(sc[:, 0], pad), jnp.pad(sc[:, 1], pad))
        init = jnp.concatenate([jnp.zeros((SUBLANES - 2, D_FF), F32),
                                h_small[META_ROW0 + N_META - 2:META_ROW0 + N_META]], axis=0)
        xb_out, hl = _ffn(xb_in, ab, wa, g_ffn[i], w_up[i], w_gate[i], conv_w[i], conv_b[i], w_down[i], init)
        tps = SEQ // FFN_TM
        conv_p = hl.reshape(N_BATCH, tps, SUBLANES, D_FF)[:, tps - 1, SUBLANES - 2:, :]
        conv_s = jnp.stack([sc[:, 1], h_small[:N_SAMPLE]], axis=1)
        return xb_out, xs_out, conv_p, conv_s

    x2b, x2s, conv_p0, conv_s0 = ffn_layer(0, x1b, x1s, None, None, None)

    w_qkv = attn_w_qkv.astype(BF16)
    wf = jnp.pad(attn_w_f.astype(BF16), ((0, 0), (0, LANES - N_HEADS)))
    bf = jnp.pad(attn_b_f.astype(F32), (0, LANES - N_HEADS)).reshape(1, LANES)
    qg = (jnp.tile(attn_q_g.astype(F32), N_HEADS) * (HEAD_DIM ** -0.5)).reshape(1, D_MODEL)
    kg = jnp.tile(attn_k_g.astype(F32), N_HEADS).reshape(1, D_MODEL)
    head_of_lane = jnp.arange(D_MODEL) // HEAD_DIM
    onehot = (head_of_lane[:, None] == jnp.arange(LANES)[None, :])
    seg = (onehot.astype(F32) / HEAD_DIM).astype(BF16)
    expand = onehot.T.astype(BF16)
    tri_body = (jnp.arange(FFN_TM)[:, None] >= jnp.arange(FFN_TM)[None, :]).astype(BF16)
    r = jnp.arange(SMALL_ROWS)
    tri_small = ((r[:, None] >= r[None, :]) & (r[None, :] >= META_ROW0) & (r[:, None] < META_ROW0 + N_META))
    tri_small = tri_small.astype(BF16)

    qb_, kb_f32, vb_f32, ktb, vbb, lfb, ceb, ctb = _qkv(x2b, g_mix[1], w_qkv, wf, bf, qg, kg, seg, expand,
                                                        tri_body, tm=FFN_TM)
    qs_, ks_f32, vs_f32, kts, vbs, lfs, ces, cts = _qkv(x2s, g_mix[1], w_qkv, wf, bf, qg, kg, seg, expand,
                                                        tri_small, tm=SMALL_ROWS)

    nblk = SEQ // ATT_T
    ct = ctb[:N_HEADS].reshape(N_PAIRS, 2, N_BATCH, nblk, ATT_T)
    ct = jnp.transpose(ct, (2, 0, 3, 1, 4))
    ct = jnp.pad(ct, ((0, 0), (0, 0), (0, 0), (0, SUBLANES - 2), (0, 0)))
    cmt = cts[:N_HEADS, meta_rows].reshape(N_PAIRS, 2, N_META)
    cmt = jnp.pad(cmt, ((0, 0), (0, SUBLANES - 2), (0, 0)))
    ctot = jnp.broadcast_to(ces[:, META_ROW0 + N_META - 1:META_ROW0 + N_META, :], (N_PAIRS, SUBLANES, LANES))
    kmt = kts[:, 0, :, meta_rows]
    ob = _attn(qb_, ktb, vbb, ceb, ct, kmt, vbs[:, meta_rows], cmt, ctot)
    o_meta = _meta_attention(qs_[:, meta_rows], kmt, vbs[:, meta_rows], ces[:, meta_rows], cmt)

    n_pool = cache_k.shape[0]
    cache_kt = jnp.transpose(cache_k, (0, 2, 3, 1)).reshape(n_pool, D_MODEL, PAGE)
    cache_vt = jnp.transpose(cache_v, (0, 2, 3, 1)).reshape(n_pool, D_MODEL, PAGE)
    cache_lft = jnp.transpose(cache_logf.astype(F32), (0, 2, 1))
    q_tok = _pairs_to_rows(qs_)[:N_SAMPLE]
    low = (jnp.arange(PAGE)[:, None] > jnp.arange(PAGE)[None, :]).astype(BF16)
    acc_t, m_t, l_t = _decode(page_table.astype(jnp.int32), q_tok.astype(F32).T, cache_kt, cache_vt, cache_lft, low)
    pad_heads = lambda x, v: jnp.pad(x.T, ((0, 0), (0, LANES - N_HEADS)), constant_values=v)
    o_dec = _decode_combine(acc_t.T, pad_heads(m_t, 0.0), pad_heads(l_t, 1.0), q_tok, ks_f32[:N_SAMPLE],
                            vs_f32[:N_SAMPLE], lfs[:N_SAMPLE], (onehot.astype(BF16)), expand)
    o_small = jnp.concatenate([o_dec, o_meta, jnp.zeros((pad_rows, D_MODEL), BF16)], axis=0)

    w_o = attn_w_o.astype(BF16)
    x4b, x4s, conv_p1, conv_s1 = ffn_layer(1, x2b, x2s, ob, o_small, w_o)

    def with_meta(small, body, width):
        m = jnp.broadcast_to(small[meta_rows][None], (N_BATCH, N_META, width))
        return jnp.concatenate([m, body.reshape(N_BATCH, SEQ, width)], axis=1)

    k_p = with_meta(ks_f32, kb_f32, D_MODEL).reshape(N_BATCH, SEQ + N_META, N_HEADS, HEAD_DIM)
    v_p = with_meta(vs_f32, vb_f32, D_MODEL).reshape(N_BATCH, SEQ + N_META, N_HEADS, HEAD_DIM)
    lf_p = with_meta(lfs[:, :N_HEADS], lfb[:, :N_HEADS], N_HEADS)
    return (x4b.reshape(N_BATCH, SEQ, D_MODEL),
            x4s[:N_SAMPLE].reshape(N_SAMPLE, 1, D_MODEL),
            hpr.reshape(N_BATCH, N_GROUPS, STATE_P), hpi.reshape(N_BATCH, N_GROUPS, STATE_P),
            hsr.reshape(N_SAMPLE, N_GROUPS, STATE_P), hsi.reshape(N_SAMPLE, N_GROUPS, STATE_P),
            k_p, v_p, lf_p,
            ks_f32[:N_SAMPLE].reshape(N_SAMPLE, 1, N_HEADS, HEAD_DIM),
            vs_f32[:N_SAMPLE].reshape(N_SAMPLE, 1, N_HEADS, HEAD_DIM),
            lfs[:N_SAMPLE, :N_HEADS].reshape(N_SAMPLE, 1, N_HEADS),
            jnp.stack([conv_p0, conv_p1]), jnp.stack([conv_s0, conv_s1]))
```

```python
import functools

import jax
import jax.numpy as jnp
from jax import lax
from jax.experimental import pallas as pl
from jax.experimental.pallas import tpu as pltpu

F32 = jnp.float32
BF16 = jnp.bfloat16

D_MODEL = 1024
N_BATCH = 8
SEQ = 2048
N_SAMPLE = 128
N_META = 16
SMALL_ROWS = 256
META_ROW0 = 128
N_GROUPS = 64
STATE_P = 64
SSM_GROUP = 16
N_HEADS = 16
HEAD_DIM = 64
N_PAIRS = N_HEADS // 2
D_FF = 2816
CONV_W = 3
PAGE = 128
EPS = 1e-6
NEG_INF = -1e30

LANES = 128
SUBLANES = 8
S5_CHUNK = 8
S5_SETS = D_MODEL // LANES
SET_STATE = (LANES // SSM_GROUP) * STATE_P
N_CHUNKS = SEQ // S5_CHUNK
VMEM_LIMIT = 56 * 1024 * 1024


def _cparams(sem):
    return pltpu.CompilerParams(dimension_semantics=sem, vmem_limit_bytes=VMEM_LIMIT)


def _const_spec(shape):
    nd = len(shape)
    return pl.BlockSpec(shape, lambda *_: (0,) * nd, pipeline_mode=pl.Buffered(1))


def _full_spec(shape):
    nd = len(shape)
    return pl.BlockSpec(shape, lambda *_: (0,) * nd)


def _dot(a, b):
    return jnp.dot(a, b, preferred_element_type=F32)


def _dot_f32(a, b):
    return jnp.dot(a, b, preferred_element_type=F32, precision=lax.Precision.HIGHEST)


def _split2(x):
    hi = x.astype(BF16)
    lo = (x - hi.astype(F32)).astype(BF16)
    return hi, lo


def _split3(x):
    hi = x.astype(BF16)
    r = x - hi.astype(F32)
    mid = r.astype(BF16)
    lo = (r - mid.astype(F32)).astype(BF16)
    return hi, mid, lo


def _dot_split(parts, w):
    acc = _dot(parts[0], w)
    for p in parts[1:]:
        acc = acc + _dot(p, w)
    return acc


def _rmsnorm(x, g):
    ms = jnp.mean(x * x, axis=-1, keepdims=True)
    return x * lax.rsqrt(ms + EPS) * g


def _gelu_tanh(x):
    c = 0.7978845608028654
    return 0.5 * x * (1.0 + jnp.tanh(c * (x + 0.044715 * (x * x * x))))


def _sigmoid(x):
    return 1.0 / (1.0 + jnp.exp(-x))


def _norm_shuffle_kernel(x_ref, g_ref, o_ref, scr):
    rb = o_ref.shape[1]
    u = _rmsnorm(x_ref[...], g_ref[...])
    for j in range(S5_SETS):
        scr[j] = u[:, j * LANES:(j + 1) * LANES]
    for j in range(S5_SETS):
        for s in range(S5_CHUNK):
            o_ref[j, :, s * LANES:(s + 1) * LANES] = scr[j, pl.ds(s, rb, stride=S5_CHUNK), :].astype(BF16)


def _norm_shuffle(x, g):
    rows = x.shape[0] // S5_CHUNK
    rb = 128
    return pl.pallas_call(
        _norm_shuffle_kernel,
        grid=(rows // rb,),
        in_specs=[pl.BlockSpec((rb * S5_CHUNK, D_MODEL), lambda i: (i, 0)),
                  _const_spec((1, D_MODEL))],
        out_specs=pl.BlockSpec((S5_SETS, rb, S5_CHUNK * LANES), lambda i: (0, i, 0)),
        out_shape=jax.ShapeDtypeStruct((S5_SETS, rows, S5_CHUNK * LANES), BF16),
        scratch_shapes=[pltpu.VMEM((S5_SETS, rb * S5_CHUNK, LANES), F32)],
        compiler_params=_cparams(("parallel",)),
        name="norm_shuffle",
    )(x, g)


def _norm_rows_kernel(x_ref, g_ref, o_ref):
    o_ref[...] = _rmsnorm(x_ref[...], g_ref[...]).astype(BF16)


def _norm_rows(x, g):
    rows = x.shape[0]
    return pl.pallas_call(
        _norm_rows_kernel,
        grid=(1,),
        in_specs=[_full_spec((rows, D_MODEL)), _full_spec((1, D_MODEL))],
        out_specs=_full_spec((rows, D_MODEL)),
        out_shape=jax.ShapeDtypeStruct((rows, D_MODEL), BF16),
        name="norm_rows",
    )(x, g)


S5_ROW_BLOCK = 512
N_SLABS = SET_STATE // LANES
SCAN_PITCH = N_CHUNKS + SUBLANES


def _s5_kernel(ub_ref, um_ref, us_ref, h0r_ref, h0i_ref, wt_ref, wb_ref, wc_ref, a8_ref, dt_ref,
               wb1_ref, wc1_ref, a1_ref, d1_ref,
               yb_ref, ym_ref, ys_ref, hpr_ref, hpi_ref, hsr_ref, hsi_ref, xr_scr, xi_scr):
    rows = ub_ref.shape[1]
    wb = wb_ref[0]
    wt = wt_ref[0]
    wc = wc_ref[0]
    dt = dt_ref[0]

    seq_per_block = S5_ROW_BLOCK // N_CHUNKS

    def scratch_rows(n):
        return slice(n * SCAN_PITCH, n * SCAN_PITCH + N_CHUNKS)

    for r in range(rows // S5_ROW_BLOCK):
        x = _dot(ub_ref[0, r * S5_ROW_BLOCK:(r + 1) * S5_ROW_BLOCK, :], wb)
        for q in range(seq_per_block):
            xs = x[q * N_CHUNKS:(q + 1) * N_CHUNKS]
            ss = scratch_rows(r * seq_per_block + q)
            for k in range(N_SLABS):
                xr_scr[k, ss, :] = xs[:, k * LANES:(k + 1) * LANES]
                xi_scr[k, ss, :] = xs[:, SET_STATE + k * LANES:SET_STATE + (k + 1) * LANES]
    xm = _dot(um_ref[0], wb)

    a8r = jnp.broadcast_to(a8_ref[0, 0:1, :], (SUBLANES, SET_STATE))
    a8i = jnp.broadcast_to(a8_ref[0, 1:2, :], (SUBLANES, SET_STATE))

    hr = jnp.zeros((SUBLANES, SET_STATE), F32)
    hi = jnp.zeros((SUBLANES, SET_STATE), F32)
    hm_prev = []
    for cm in range(N_META // S5_CHUNK):
        hm_prev.append(jnp.concatenate([hr, hi], axis=1))
        xr = xm[cm * SUBLANES:(cm + 1) * SUBLANES, :SET_STATE]
        xi = xm[cm * SUBLANES:(cm + 1) * SUBLANES, SET_STATE:]
        hr, hi = a8r * hr - a8i * hi + xr, a8r * hi + a8i * hr + xi

    ar = [a8r[:, k * LANES:(k + 1) * LANES] for k in range(N_SLABS)]
    ai = [a8i[:, k * LANES:(k + 1) * LANES] for k in range(N_SLABS)]

    def step(c, carry):
        hrs, his = carry
        nr, ni = [], []
        for k in range(N_SLABS):
            idx = (k, pl.ds(c, N_BATCH, stride=SCAN_PITCH), slice(None))
            xr = xr_scr[idx]
            xi = xi_scr[idx]
            xr_scr[idx] = hrs[k]
            xi_scr[idx] = his[k]
            nr.append(ar[k] * hrs[k] - ai[k] * his[k] + xr)
            ni.append(ar[k] * his[k] + ai[k] * hrs[k] + xi)
        return tuple(nr), tuple(ni)

    init = (tuple(hr[:, k * LANES:(k + 1) * LANES] for k in range(N_SLABS)),
            tuple(hi[:, k * LANES:(k + 1) * LANES] for k in range(N_SLABS)))
    hrs, his = lax.fori_loop(0, N_CHUNKS, step, init)
    hpr_ref[...] = jnp.concatenate(hrs, axis=1)
    hpi_ref[...] = jnp.concatenate(his, axis=1)

    def emit(u, hprev):
        y = _dot(u, wt) + _dot(hprev.astype(BF16), wc) + dt * u.astype(F32)
        return _gelu_tanh(y).astype(BF16)

    for r in range(rows // S5_ROW_BLOCK):
        rs = slice(r * S5_ROW_BLOCK, (r + 1) * S5_ROW_BLOCK)
        hprev = jnp.concatenate(
            [jnp.concatenate([xr_scr[k, scratch_rows(r * seq_per_block + q), :] for k in range(N_SLABS)]
                             + [xi_scr[k, scratch_rows(r * seq_per_block + q), :] for k in range(N_SLABS)], axis=1)
             for q in range(seq_per_block)], axis=0)
        yb_ref[0, rs, :] = emit(ub_ref[0, rs, :], hprev)
    ym_ref[0] = emit(um_ref[0], jnp.concatenate(hm_prev, axis=0))

    us = us_ref[...]
    xs = _dot(us, wb1_ref[0])
    a1r = a1_ref[0, 0:1, :]
    a1i = a1_ref[0, 1:2, :]
    h0r = h0r_ref[...]
    h0i = h0i_ref[...]
    hsr = a1r * h0r - a1i * h0i + xs[:, :SET_STATE]
    hsi = a1r * h0i + a1i * h0r + xs[:, SET_STATE:]
    hsr_ref[...] = hsr
    hsi_ref[...] = hsi
    hs = jnp.concatenate([hsr, hsi], axis=1).astype(BF16)
    ys = _dot(hs, wc1_ref[0]) + d1_ref[0] * us.astype(F32)
    ys_ref[...] = _gelu_tanh(ys).astype(BF16)


def _set_spec(shape):
    return pl.BlockSpec((1,) + shape, lambda j: (j,) + (0,) * len(shape))


def _s5(ub, um, u_small, h0r, h0i, w):
    rows = ub.shape[1]
    cw = S5_CHUNK * LANES
    out_shapes = (
        jax.ShapeDtypeStruct((S5_SETS, rows, cw), BF16),
        jax.ShapeDtypeStruct((S5_SETS, N_META, cw), BF16),
        jax.ShapeDtypeStruct((N_SAMPLE, D_MODEL), BF16),
        jax.ShapeDtypeStruct((N_BATCH, S5_SETS * SET_STATE), F32),
        jax.ShapeDtypeStruct((N_BATCH, S5_SETS * SET_STATE), F32),
        jax.ShapeDtypeStruct((N_SAMPLE, S5_SETS * SET_STATE), F32),
        jax.ShapeDtypeStruct((N_SAMPLE, S5_SETS * SET_STATE), F32),
    )
    return pl.pallas_call(
        _s5_kernel,
        grid=(S5_SETS,),
        in_specs=[
            _set_spec((rows, cw)),
            _set_spec((N_META, cw)),
            pl.BlockSpec((N_SAMPLE, LANES), lambda j: (0, j)),
            pl.BlockSpec((N_SAMPLE, SET_STATE), lambda j: (0, j)),
            pl.BlockSpec((N_SAMPLE, SET_STATE), lambda j: (0, j)),
            _set_spec((cw, cw)),
            _set_spec((cw, 2 * SET_STATE)),
            _set_spec((2 * SET_STATE, cw)),
            _set_spec((2, SET_STATE)),
            _set_spec((1, cw)),
            _set_spec((LANES, 2 * SET_STATE)),
            _set_spec((2 * SET_STATE, LANES)),
            _set_spec((2, SET_STATE)),
            _set_spec((1, LANES)),
        ],
        out_specs=(
            _set_spec((rows, cw)),
            _set_spec((N_META, cw)),
            pl.BlockSpec((N_SAMPLE, LANES), lambda j: (0, j)),
            pl.BlockSpec((N_BATCH, SET_STATE), lambda j: (0, j)),
            pl.BlockSpec((N_BATCH, SET_STATE), lambda j: (0, j)),
            pl.BlockSpec((N_SAMPLE, SET_STATE), lambda j: (0, j)),
            pl.BlockSpec((N_SAMPLE, SET_STATE), lambda j: (0, j)),
        ),
        out_shape=out_shapes,
        scratch_shapes=[pltpu.VMEM((N_SLABS, N_BATCH * SCAN_PITCH, LANES), F32),
                        pltpu.VMEM((N_SLABS, N_BATCH * SCAN_PITCH, LANES), F32)],
        compiler_params=_cparams(("parallel",)),
        name="s5_mixer",
    )(ub, um, u_small, h0r, h0i, w["wt"], w["wb"], w["wc"], w["a8"], w["dt"],
      w["wb1"], w["wc1"], w["a1"], w["d1"])


def _s5_discretize(a_re, a_im, log_step):
    dt = jnp.exp(log_step)
    mag = jnp.exp(a_re * dt)
    ab_re = mag * jnp.cos(a_im * dt)
    ab_im = mag * jnp.sin(a_im * dt)
    den = a_re * a_re + a_im * a_im
    n_re = ab_re - 1.0
    n_im = ab_im
    f_re = (n_re * a_re + n_im * a_im) / den
    f_im = (n_im * a_re - n_re * a_im) / den
    return ab_re, ab_im, f_re, f_im


def _s5_weights_kernel(arow_ref, acr_ref, aci_ref, acl_ref, br_ref, bi_ref, cr_ref, ci_ref,
                       wt_ref, wb_ref, wc_ref, a8_ref, wb1_ref, wc1_ref, a1_ref):
    arow = arow_ref[0]
    abr, abi, fr, fi = _s5_discretize(arow[0:1], arow[1:2], arow[2:3])
    abr_c, abi_c, _, _ = _s5_discretize(acr_ref[0], aci_ref[0], acl_ref[0])
    br = br_ref[0]
    bi = bi_ref[0]
    b0r = fr * br - fi * bi
    b0i = fr * bi + fi * br
    cr = cr_ref[0]
    ci = ci_ref[0]
    wt_ref[0] = jnp.zeros(wt_ref.shape[1:], BF16)
    pr = jnp.ones_like(abr)
    pi = jnp.zeros_like(abr)
    pcr, pci = abr_c, abi_c
    for k in range(S5_CHUNK):
        bkr = pr * b0r - pi * b0i
        bki = pr * b0i + pi * b0r
        s = S5_CHUNK - 1 - k
        wb_ref[0, s * LANES:(s + 1) * LANES, :SET_STATE] = bkr.astype(BF16)
        wb_ref[0, s * LANES:(s + 1) * LANES, SET_STATE:] = bki.astype(BF16)
        kern = (_dot_f32(bkr, cr) - _dot_f32(bki, ci)).astype(BF16)
        for s2 in range(S5_CHUNK - k):
            t2 = s2 + k
            wt_ref[0, s2 * LANES:(s2 + 1) * LANES, t2 * LANES:(t2 + 1) * LANES] = kern
        wc_ref[0, :SET_STATE, k * LANES:(k + 1) * LANES] = (pcr * cr - pci * ci).astype(BF16)
        wc_ref[0, SET_STATE:, k * LANES:(k + 1) * LANES] = (-(pcr * ci + pci * cr)).astype(BF16)
        pr, pi = pr * abr - pi * abi, pr * abi + pi * abr
        pcr, pci = pcr * abr_c - pci * abi_c, pcr * abi_c + pci * abr_c
    a8_ref[0, 0:1, :] = pr
    a8_ref[0, 1:2, :] = pi
    a1_ref[0, 0:1, :] = abr
    a1_ref[0, 1:2, :] = abi
    wb1_ref[0, :, :SET_STATE] = b0r.astype(BF16)
    wb1_ref[0, :, SET_STATE:] = b0i.astype(BF16)
    wc1_ref[0, :SET_STATE, :] = cr.astype(BF16)
    wc1_ref[0, SET_STATE:, :] = (-ci).astype(BF16)


def _s5_weights(a_re, a_im, b_re, b_im, c_re, c_im, d_skip, log_step):
    gs = LANES // SSM_GROUP
    cw = S5_CHUNK * LANES
    eye = jnp.eye(gs, dtype=F32)
    per_state = lambda v: v.astype(F32).reshape(S5_SETS, SET_STATE)
    a_re_s, a_im_s = per_state(a_re), per_state(a_im)
    ls_s = per_state(jnp.broadcast_to(log_step[:, None], (N_GROUPS, STATE_P)))
    arow = jnp.stack([a_re_s, a_im_s, ls_s] + [jnp.zeros_like(ls_s)] * (SUBLANES - 3), axis=1)
    col = lambda v: jnp.broadcast_to(v[:, :, None], (S5_SETS, SET_STATE, LANES))

    def blockdiag_b(b):
        b = b.astype(F32).reshape(S5_SETS, gs, STATE_P, SSM_GROUP)
        return jnp.einsum('jgpi,gh->jgihp', b, eye).reshape(S5_SETS, LANES, SET_STATE)

    def blockdiag_c(c):
        c = c.astype(F32).reshape(S5_SETS, gs, SSM_GROUP, STATE_P)
        return jnp.einsum('jgop,gh->jgpho', c, eye).reshape(S5_SETS, SET_STATE, LANES)

    out_shapes = (
        jax.ShapeDtypeStruct((S5_SETS, cw, cw), BF16),
        jax.ShapeDtypeStruct((S5_SETS, cw, 2 * SET_STATE), BF16),
        jax.ShapeDtypeStruct((S5_SETS, 2 * SET_STATE, cw), BF16),
        jax.ShapeDtypeStruct((S5_SETS, 2, SET_STATE), F32),
        jax.ShapeDtypeStruct((S5_SETS, LANES, 2 * SET_STATE), BF16),
        jax.ShapeDtypeStruct((S5_SETS, 2 * SET_STATE, LANES), BF16),
        jax.ShapeDtypeStruct((S5_SETS, 2, SET_STATE), F32),
    )
    wt, wb, wc, a8, wb1, wc1, a1 = pl.pallas_call(
        _s5_weights_kernel,
        grid=(S5_SETS,),
        in_specs=[_set_spec((SUBLANES, SET_STATE)), _set_spec((SET_STATE, LANES)), _set_spec((SET_STATE, LANES)),
                  _set_spec((SET_STATE, LANES)), _set_spec((LANES, SET_STATE)), _set_spec((LANES, SET_STATE)),
                  _set_spec((SET_STATE, LANES)), _set_spec((SET_STATE, LANES))],
        out_specs=tuple(_set_spec(s.shape[1:]) for s in out_shapes),
        out_shape=out_shapes,
        compiler_params=_cparams(("parallel",)),
        name="s5_weights",
    )(arow, col(a_re_s), col(a_im_s), col(ls_s), blockdiag_b(b_re), blockdiag_b(b_im),
      blockdiag_c(c_re), blockdiag_c(c_im))
    d1 = d_skip.astype(F32).reshape(S5_SETS, 1, LANES)
    return {"wt": wt, "wb": wb, "wc": wc, "a8": a8, "dt": jnp.tile(d1, (1, 1, S5_CHUNK)),
            "wb1": wb1, "wc1": wc1, "a1": a1, "d1": d1}


def _glu(y, w):
    z = _dot(y, w)
    return z[:, :D_MODEL] * _sigmoid(z[:, D_MODEL:])


def _glu_res_kernel(x_ref, y_ref, w_ref, o_ref, scr):
    rb = y_ref.shape[1]
    w = w_ref[...]
    for s in range(S5_CHUNK):
        y = jnp.concatenate([y_ref[j, :, s * LANES:(s + 1) * LANES] for j in range(S5_SETS)], axis=1)
        o = _glu(y, w)
        for j in range(S5_SETS):
            scr[j, pl.ds(s, rb, stride=S5_CHUNK), :] = o[:, j * LANES:(j + 1) * LANES]
    for j in range(S5_SETS):
        cs = slice(j * LANES, (j + 1) * LANES)
        o_ref[:, cs] = x_ref[:, cs] + scr[j]


def _glu_res(x, y, w_glu):
    rows = y.shape[1]
    rb = 128
    tok_spec = pl.BlockSpec((rb * S5_CHUNK, D_MODEL), lambda i: (i, 0))
    return pl.pallas_call(
        _glu_res_kernel,
        grid=(rows // rb,),
        in_specs=[tok_spec, pl.BlockSpec((S5_SETS, rb, S5_CHUNK * LANES), lambda i: (0, i, 0)),
                  _const_spec((D_MODEL, 2 * D_MODEL))],
        out_specs=tok_spec,
        out_shape=jax.ShapeDtypeStruct(x.shape, F32),
        scratch_shapes=[pltpu.VMEM((S5_SETS, rb * S5_CHUNK, LANES), F32)],
        compiler_params=_cparams(("parallel",)),
        name="glu_res",
    )(x, y, w_glu)


def _glu_res_rows_kernel(x_ref, y_ref, w_ref, o_ref):
    o_ref[...] = x_ref[...] + _glu(y_ref[...], w_ref[...])


def _glu_res_rows(x, y, w_glu):
    rows = x.shape[0]
    return pl.pallas_call(
        _glu_res_rows_kernel,
        grid=(1,),
        in_specs=[_full_spec((rows, D_MODEL)), _full_spec((rows, D_MODEL)), _full_spec((D_MODEL, 2 * D_MODEL))],
        out_specs=_full_spec((rows, D_MODEL)),
        out_shape=jax.ShapeDtypeStruct((rows, D_MODEL), F32),
        name="glu_res_rows",
    )(x, y, w_glu)


FFN_TM = 512
FFN_FC = D_FF // 2


def _conv_act(h, gt, p1, p2, cw, cb):
    hc = cb + cw[0:1, :] * p2
    hc = hc + cw[1:2, :] * p1
    hc = hc + cw[2:3, :] * h
    return (hc * _sigmoid(hc) * gt).astype(BF16)


def _ffn_kernel(*refs, proj, tiles_per_seq):
    if proj:
        (x_ref, a_ref, wa_ref, g_ref, wu_ref, wg_ref, cw_ref, cb_ref, wd_ref, init_ref,
         o_ref, hl_ref, carry_scr, act_scr) = refs
    else:
        (x_ref, g_ref, wu_ref, wg_ref, cw_ref, cb_ref, wd_ref, init_ref,
         o_ref, hl_ref, carry_scr, act_scr) = refs
    i = pl.program_id(0)

    @pl.when(i % tiles_per_seq == 0)
    def _():
        carry_scr[...] = init_ref[...]

    x = x_ref[...]
    if proj:
        a = jnp.concatenate([a_ref[p] for p in range(N_PAIRS)], axis=1)
        x = x + _dot(a, wa_ref[...])
    u = _rmsnorm(x, g_ref[...]).astype(BF16)
    tm = x.shape[0]
    row8 = lax.broadcasted_iota(jnp.int32, (SUBLANES, 1), 0)
    out = x
    for c in range(D_FF // FFN_FC):
        cs = slice(c * FFN_FC, (c + 1) * FFN_FC)
        h = _dot(u, wu_ref[:, cs])
        gt = _dot(u, wg_ref[:, cs])
        cw = cw_ref[:, cs]
        cb = cb_ref[:, cs]
        carry = carry_scr[:, cs]
        p1 = pltpu.roll(h, 1, axis=0)
        p2 = pltpu.roll(h, 2, axis=0)
        act_scr[...] = _conv_act(h, gt, p1, p2, cw, cb)
        c1 = pltpu.roll(carry, 1, axis=0)
        c2 = pltpu.roll(carry, 2, axis=0)
        p1h = jnp.where(row8 < 1, c1, p1[:SUBLANES])
        p2h = jnp.where(row8 < 2, c2, p2[:SUBLANES])
        act_scr[0:SUBLANES, :] = _conv_act(h[:SUBLANES], gt[:SUBLANES], p1h, p2h, cw, cb)
        last = h[tm - SUBLANES:, :]
        carry_scr[:, cs] = last
        hl_ref[0, :, cs] = last
        out = out + _dot(act_scr[...], wd_ref[cs, :])
    o_ref[...] = out


def _ffn(x, a, wa, g, wu, wg, cw, cb, wd, init):
    rows = x.shape[0]
    tm = FFN_TM
    proj = a is not None
    row_spec = pl.BlockSpec((tm, D_MODEL), lambda i: (i, 0))
    in_specs = [row_spec]
    args = [x]
    if proj:
        in_specs += [pl.BlockSpec((N_PAIRS, tm, LANES), lambda i: (0, i, 0)), _const_spec((D_MODEL, D_MODEL))]
        args += [a, wa]
    in_specs += [_const_spec((1, D_MODEL)), _const_spec((D_MODEL, D_FF)), _const_spec((D_MODEL, D_FF)),
                 _const_spec((SUBLANES, D_FF)), _const_spec((1, D_FF)), _const_spec((D_FF, D_MODEL)),
                 _const_spec((SUBLANES, D_FF))]
    args += [g, wu, wg, cw, cb, wd, init]
    return pl.pallas_call(
        functools.partial(_ffn_kernel, proj=proj, tiles_per_seq=SEQ // tm),
        grid=(rows // tm,),
        in_specs=in_specs,
        out_specs=(row_spec, pl.BlockSpec((1, SUBLANES, D_FF), lambda i: (i, 0, 0))),
        out_shape=(jax.ShapeDtypeStruct((rows, D_MODEL), F32),
                   jax.ShapeDtypeStruct((rows // tm, SUBLANES, D_FF), F32)),
        scratch_shapes=[pltpu.VMEM((SUBLANES, D_FF), F32), pltpu.VMEM((tm, FFN_FC), BF16)],
        compiler_params=_cparams(("arbitrary",)),
        name="ffn_proj" if proj else "ffn",
    )(*args)


def _ffn_small_kernel(*refs, proj):
    if proj:
        (x_ref, a_ref, wa_ref, g_ref, wu_ref, wg_ref, cw_ref, cb_ref, wd_ref, b0_ref, b1_ref,
         o_ref, h_ref) = refs
    else:
        (x_ref, g_ref, wu_ref, wg_ref, cw_ref, cb_ref, wd_ref, b0_ref, b1_ref, o_ref, h_ref) = refs
    x = x_ref[...]
    if proj:
        x = x + _dot(a_ref[...], wa_ref[...])
    u = _rmsnorm(x, g_ref[...]).astype(BF16)
    row = lax.broadcasted_iota(jnp.int32, (SMALL_ROWS, 1), 0)
    is_sample = row < N_SAMPLE
    meta_end = META_ROW0 + N_META
    out = x
    for c in range(D_FF // FFN_FC):
        cs = slice(c * FFN_FC, (c + 1) * FFN_FC)
        h = _dot(u, wu_ref[:, cs])
        gt = _dot(u, wg_ref[:, cs])
        p1 = jnp.where(is_sample, b1_ref[:, cs],
                       jnp.where((row >= META_ROW0 + 1) & (row < meta_end), pltpu.roll(h, 1, axis=0), 0.0))
        p2 = jnp.where(is_sample, b0_ref[:, cs],
                       jnp.where((row >= META_ROW0 + 2) & (row < meta_end), pltpu.roll(h, 2, axis=0), 0.0))
        act = _conv_act(h, gt, p1, p2, cw_ref[:, cs], cb_ref[:, cs])
        h_ref[:, cs] = h
        out = out + _dot(act, wd_ref[cs, :])
    o_ref[...] = out


def _ffn_small(x, a, wa, g, wu, wg, cw, cb, wd, b0, b1):
    proj = a is not None
    in_specs = [_full_spec((SMALL_ROWS, D_MODEL))]
    args = [x]
    if proj:
        in_specs += [_full_spec((SMALL_ROWS, D_MODEL)), _full_spec((D_MODEL, D_MODEL))]
        args += [a, wa]
    in_specs += [_full_spec((1, D_MODEL)), _full_spec((D_MODEL, D_FF)), _full_spec((D_MODEL, D_FF)),
                 _full_spec((SUBLANES, D_FF)), _full_spec((1, D_FF)), _full_spec((D_FF, D_MODEL)),
                 _full_spec((SMALL_ROWS, D_FF)), _full_spec((SMALL_ROWS, D_FF))]
    args += [g, wu, wg, cw, cb, wd, b0, b1]
    return pl.pallas_call(
        functools.partial(_ffn_small_kernel, proj=proj),
        grid=(1,),
        in_specs=in_specs,
        out_specs=(_full_spec((SMALL_ROWS, D_MODEL)), _full_spec((SMALL_ROWS, D_FF))),
        out_shape=(jax.ShapeDtypeStruct((SMALL_ROWS, D_MODEL), F32),
                   jax.ShapeDtypeStruct((SMALL_ROWS, D_FF), F32)),
        compiler_params=_cparams(("arbitrary",)),
        name="ffn_small_proj" if proj else "ffn_small",
    )(*args)


ATT_T = 256
ATT_HPS = 4


def _qkv_kernel(x_ref, g_ref, w_ref, wf_ref, bf_ref, qg_ref, kg_ref, seg_ref, exp_ref, swp_ref, tri_ref, cinit_ref,
                k_ref, v_ref, qt_ref, ka_ref, vt_ref, lf_ref, ct_ref, carry_scr, *, tiles_per_seq):
    i = pl.program_id(0)

    @pl.when(i % tiles_per_seq == 0)
    def _():
        carry_scr[...] = jnp.broadcast_to(cinit_ref[...], carry_scr.shape)

    u = _rmsnorm(x_ref[...], g_ref[...]).astype(BF16)
    qkv = _dot(u, w_ref[...])
    seg = seg_ref[...]
    expand = exp_ref[...]

    def head_norm(t, gain):
        ms = _dot_split(_split2(t * t), seg)
        rinv = lax.rsqrt(ms + EPS)
        return t * _dot_split(_split2(rinv), expand) * gain

    q = head_norm(qkv[:, :D_MODEL], qg_ref[...])
    k = head_norm(qkv[:, D_MODEL:2 * D_MODEL], kg_ref[...])
    v = qkv[:, 2 * D_MODEL:]
    k_ref[...] = k
    v_ref[...] = v
    tm = k.shape[0]

    z = _dot(u, wf_ref[...]) + bf_ref[...]
    lf = jnp.minimum(z, 0.0) - jnp.log(1.0 + jnp.exp(-jnp.abs(z)))
    lf_ref[...] = lf
    tri = tri_ref[...]
    c = carry_scr[SUBLANES - 1:SUBLANES, :]
    for part in _split3(lf):
        c = c + _dot(tri, part)
    carry_scr[...] = c[tm - SUBLANES:, :]
    ct_ref[...] = c.T

    swp = swp_ref[...]
    parts = [_dot(part, swp) for part in _split3(-c)]
    l64 = lax.broadcasted_iota(jnp.int32, (1, D_MODEL), 1) % HEAD_DIM
    k_bias = jnp.where(l64 == 0, parts[0], jnp.where(l64 == 1, parts[1], jnp.where(l64 == 2, parts[2], 0.0)))
    q_ones = (l64 < len(parts)).astype(F32)
    lane = lax.broadcasted_iota(jnp.int32, (1, LANES), 1)
    for p in range(N_PAIRS):
        ps = slice(p * LANES, (p + 1) * LANES)
        for e in range(2):
            h = 2 * p + e
            own = (lane >= HEAD_DIM) if e else (lane < HEAD_DIM)
            ka_ref[h] = jnp.where(own, k[:, ps], k_bias[:, ps]).astype(BF16)
            qs = jnp.where(own, q[:, ps], q_ones[:, ps])
            vs = jnp.where(own, v[:, ps], 1.0)
            for b in range(tm // ATT_T):
                bs = slice(b * ATT_T, (b + 1) * ATT_T)
                qt_ref[h, b] = qs[bs].T.astype(BF16)
                vt_ref[h, b] = vs[bs].T.astype(BF16)


def _qkv(x, g, w_qkv, wf, bf, qg, kg, seg, expand, swp, tri, cinit, *, tm):
    rows = x.shape[0]
    row_spec = pl.BlockSpec((tm, D_MODEL), lambda i: (i, 0))
    t_spec = pl.BlockSpec((N_HEADS, tm // ATT_T, LANES, ATT_T), lambda i: (0, i, 0, 0))
    t_shape = jax.ShapeDtypeStruct((N_HEADS, rows // ATT_T, LANES, ATT_T), BF16)
    return pl.pallas_call(
        functools.partial(_qkv_kernel, tiles_per_seq=max(SEQ // tm, 1)),
        grid=(rows // tm,),
        in_specs=[row_spec, _const_spec((1, D_MODEL)), _const_spec((D_MODEL, 3 * D_MODEL)),
                  _const_spec((D_MODEL, LANES)), _const_spec((1, LANES)), _const_spec((1, D_MODEL)),
                  _const_spec((1, D_MODEL)), _const_spec((D_MODEL, LANES)), _const_spec((LANES, D_MODEL)),
                  _const_spec((LANES, D_MODEL)), _const_spec((tm, tm)), _const_spec((1, LANES))],
        out_specs=(row_spec, row_spec, t_spec,
                   pl.BlockSpec((N_HEADS, tm, LANES), lambda i: (0, i, 0)), t_spec,
                   pl.BlockSpec((tm, LANES), lambda i: (i, 0)),
                   pl.BlockSpec((LANES, tm), lambda i: (0, i))),
        out_shape=(jax.ShapeDtypeStruct((rows, D_MODEL), F32), jax.ShapeDtypeStruct((rows, D_MODEL), F32),
                   t_shape, jax.ShapeDtypeStruct((N_HEADS, rows, LANES), BF16), t_shape,
                   jax.ShapeDtypeStruct((rows, LANES), F32), jax.ShapeDtypeStruct((LANES, rows), F32)),
        scratch_shapes=[pltpu.VMEM((SUBLANES, LANES), F32)],
        compiler_params=_cparams(("arbitrary",)),
        name="qkv_proj",
    )(x, g, w_qkv, wf, bf, qg, kg, seg, expand, swp, tri, cinit)


def _softmax_step(s0, vt, cq, m, acc):
    m_new = jnp.maximum(m, jnp.max(s0, axis=0, keepdims=True) + cq)
    p = jnp.exp(s0 - (m_new - cq))
    acc = jnp.exp(m - m_new) * acc + _dot(vt, p.astype(BF16))
    return m_new, acc


def _attn_finish(acc_even, acc_odd):
    o_t = jnp.concatenate([acc_even[:HEAD_DIM] / acc_even[HEAD_DIM:],
                           acc_odd[HEAD_DIM:] / acc_odd[:HEAD_DIM]], axis=0)
    return o_t.T.astype(BF16)


def _attn_kernel(qt_ref, ka_ref, vt_ref, ct_ref, kam_ref, vtm_ref, o_ref):
    key = lax.broadcasted_iota(jnp.int32, (ATT_T, ATT_T), 0)
    qry = lax.broadcasted_iota(jnp.int32, (ATT_T, ATT_T), 1)

    def qblock(qi, _):
        r0 = pl.multiple_of(qi * ATT_T, ATT_T)
        qts = [qt_ref[h, qi] for h in range(ATT_HPS)]
        cqs = [ct_ref[0, h // 2, qi][h % 2:h % 2 + 1, :] for h in range(ATT_HPS)]

        def scores(h, blk):
            c0 = pl.multiple_of(blk * ATT_T, ATT_T)
            return _dot(ka_ref[h, pl.ds(c0, ATT_T), :], qts[h])

        state = []
        for h in range(ATT_HPS):
            s0 = _dot(kam_ref[h], qts[h])
            m = jnp.max(s0, axis=0, keepdims=True) + cqs[h]
            p = jnp.exp(s0 - (m - cqs[h]))
            state += [m, _dot(vtm_ref[h], p.astype(BF16)), scores(h, 0)]

        def kvstep(j, st):
            new = []
            for h in range(ATT_HPS):
                m, acc, s_cur = st[3 * h:3 * h + 3]
                s_next = scores(h, j + 1)
                m, acc = _softmax_step(s_cur, vt_ref[h, j], cqs[h], m, acc)
                new += [m, acc, s_next]
            return tuple(new)

        st = lax.fori_loop(0, qi, kvstep, tuple(state))
        accs = []
        for h in range(ATT_HPS):
            m, acc, s_diag = st[3 * h:3 * h + 3]
            s_diag = jnp.where(key <= qry, s_diag, NEG_INF)
            accs.append(_softmax_step(s_diag, vt_ref[h, qi], cqs[h], m, acc)[1])
        for p in range(ATT_HPS // 2):
            o_ref[p, pl.ds(r0, ATT_T), :] = _attn_finish(accs[2 * p], accs[2 * p + 1])
        return 0

    lax.fori_loop(0, SEQ // ATT_T, qblock, 0)


def _attn(qt, ka, vt, ct, kam, vtm):
    nblk = SEQ // ATT_T
    hps = ATT_HPS
    t_spec = pl.BlockSpec((hps, nblk, LANES, ATT_T), lambda n, g: (g, n, 0, 0))
    return pl.pallas_call(
        _attn_kernel,
        grid=(N_BATCH, N_HEADS // hps),
        in_specs=[t_spec,
                  pl.BlockSpec((hps, SEQ, LANES), lambda n, g: (g, n, 0)),
                  t_spec,
                  pl.BlockSpec((1, hps // 2, nblk, SUBLANES, ATT_T), lambda n, g: (n, g, 0, 0, 0)),
                  pl.BlockSpec((hps, N_META, LANES), lambda n, g: (g, 0, 0)),
                  pl.BlockSpec((hps, LANES, N_META), lambda n, g: (g, 0, 0))],
        out_specs=pl.BlockSpec((hps // 2, SEQ, LANES), lambda n, g: (g, n, 0)),
        out_shape=jax.ShapeDtypeStruct((N_PAIRS, N_BATCH * SEQ, LANES), BF16),
        compiler_params=_cparams(("parallel", "parallel")),
        name="attn_prompt",
    )(qt, ka, vt, ct, kam, vtm)


def _meta_attn_kernel(qtm_ref, kam_ref, vtm_ref, cmt_ref, o_ref):
    key = lax.broadcasted_iota(jnp.int32, (N_META, N_META), 0)
    qry = lax.broadcasted_iota(jnp.int32, (N_META, N_META), 1)
    for p in range(N_PAIRS):
        accs = []
        for e in range(2):
            h = 2 * p + e
            cq = cmt_ref[p, e:e + 1, :]
            s0 = jnp.where(key <= qry, _dot(kam_ref[h], qtm_ref[h]), NEG_INF)
            m = jnp.max(s0, axis=0, keepdims=True) + cq
            pr = jnp.exp(s0 - (m - cq))
            accs.append(_dot(vtm_ref[h], pr.astype(BF16)))
        o_ref[p] = _attn_finish(*accs)


def _meta_attention(qtm, kam, vtm, cmt):
    pm = (N_PAIRS, N_META, LANES)
    t_shape = (N_HEADS, LANES, N_META)
    o = pl.pallas_call(
        _meta_attn_kernel,
        grid=(1,),
        in_specs=[_full_spec(t_shape), _full_spec((N_HEADS, N_META, LANES)), _full_spec(t_shape),
                  _full_spec((N_PAIRS, SUBLANES, N_META))],
        out_specs=_full_spec(pm),
        out_shape=jax.ShapeDtypeStruct(pm, BF16),
        name="attn_meta",
    )(qtm, kam, vtm, cmt)
    return _pairs_to_rows(o)


DEC_PPS = 8


def _decode_kernel(pt_ref, qt_ref, *refs):
    del pt_ref
    k_refs = refs[:DEC_PPS]
    v_refs = refs[DEC_PPS:2 * DEC_PPS]
    lf_refs = refs[2 * DEC_PPS:3 * DEC_PPS]
    low_ref, o_ref, m_ref, l_ref, qc_scr, m_scr, l_scr, tail_scr, acc_scr = refs[3 * DEC_PPS:]
    b = pl.program_id(0)
    t = pl.program_id(1)
    mine = lax.broadcasted_iota(jnp.int32, (1, LANES), 1) == b

    @pl.when((b == 0) & (t == 0))
    def _():
        o_ref[...] = jnp.zeros_like(o_ref)
        m_ref[...] = jnp.zeros_like(m_ref)
        l_ref[...] = jnp.zeros_like(l_ref)

    @pl.when(t == 0)
    def _():
        qcol = jnp.sum(jnp.where(mine, qt_ref[...], 0.0), axis=1, keepdims=True)
        qc_scr[...] = jnp.broadcast_to(qcol, (D_MODEL, LANES))
        m_scr[...] = jnp.full(m_scr.shape, NEG_INF, F32)
        l_scr[...] = jnp.zeros_like(l_scr)
        tail_scr[...] = jnp.zeros_like(tail_scr)
        acc_scr[...] = jnp.zeros_like(acc_scr)

    low = low_ref[...]
    qc = qc_scr[...]
    tail = tail_scr[...]
    scores = []
    for i in range(DEC_PPS):
        kq = k_refs[i][0] * qc
        s = jnp.sum(kq.reshape(N_HEADS, HEAD_DIM, LANES), axis=1)
        lf = lf_refs[i][0]
        s = s + tail
        for part in _split3(lf):
            s = s + _dot(part, low)
        scores.append(s)
        tail = tail + jnp.sum(lf, axis=1, keepdims=True)
    tail_scr[...] = tail
    m_old = m_scr[...]
    s_max = scores[0]
    for s in scores[1:]:
        s_max = jnp.maximum(s_max, s)
    m_new = jnp.maximum(m_old, jnp.max(s_max, axis=1, keepdims=True))
    alpha = jnp.exp(m_old - m_new)
    probs = [jnp.exp(s - m_new) for s in scores]
    p_sum = probs[0]
    for p in probs[1:]:
        p_sum = p_sum + p
    l_scr[...] = alpha * l_scr[...] + jnp.sum(p_sum, axis=1, keepdims=True)
    m_scr[...] = m_new
    for h in range(N_HEADS):
        rows = slice(h * HEAD_DIM, (h + 1) * HEAD_DIM)
        acc = acc_scr[rows, :] * alpha[h:h + 1, :]
        for i in range(DEC_PPS):
            acc = acc + v_refs[i][0, rows, :] * probs[i][h:h + 1, :]
        acc_scr[rows, :] = acc

    @pl.when(t == pl.num_programs(1) - 1)
    def _():
        o_ref[...] = jnp.where(mine, jnp.sum(acc_scr[...], axis=1, keepdims=True), o_ref[...])
        m_ref[...] = jnp.where(mine, m_scr[...], m_ref[...])
        l_ref[...] = jnp.where(mine, l_scr[...], l_ref[...])


def _decode(page_table, qt, cache_kt, cache_vt, cache_lft, low):
    n_pages = page_table.shape[1]
    steps = n_pages // DEC_PPS

    def page_map(i):
        return lambda b, t, pt: (pt[b, n_pages - 1 - (t * DEC_PPS + i)], 0, 0)

    const = lambda shape: pl.BlockSpec(shape, lambda b, t, pt: (0,) * len(shape))
    in_specs = [const((D_MODEL, N_SAMPLE))]
    in_specs += [pl.BlockSpec((1, D_MODEL, PAGE), page_map(i)) for i in range(DEC_PPS)]
    in_specs += [pl.BlockSpec((1, D_MODEL, PAGE), page_map(i)) for i in range(DEC_PPS)]
    in_specs += [pl.BlockSpec((1, N_HEADS, PAGE), page_map(i)) for i in range(DEC_PPS)]
    in_specs += [const((PAGE, PAGE))]
    stat = lambda: pltpu.VMEM((N_HEADS, LANES), F32)
    grid_spec = pltpu.PrefetchScalarGridSpec(
        num_scalar_prefetch=1,
        grid=(N_SAMPLE, steps),
        in_specs=in_specs,
        out_specs=(const((D_MODEL, N_SAMPLE)), const((N_HEADS, N_SAMPLE)), const((N_HEADS, N_SAMPLE))),
        scratch_shapes=[pltpu.VMEM((D_MODEL, LANES), F32), stat(), stat(), stat(),
                        pltpu.VMEM((D_MODEL, LANES), F32)],
    )
    return pl.pallas_call(
        _decode_kernel,
        grid_spec=grid_spec,
        out_shape=(jax.ShapeDtypeStruct((D_MODEL, N_SAMPLE), F32),
                   jax.ShapeDtypeStruct((N_HEADS, N_SAMPLE), F32),
                   jax.ShapeDtypeStruct((N_HEADS, N_SAMPLE), F32)),
        compiler_params=_cparams(("arbitrary", "arbitrary")),
        name="attn_decode",
    )(page_table, qt, *([cache_kt] * DEC_PPS), *([cache_vt] * DEC_PPS), *([cache_lft] * DEC_PPS), low)


def _decode_combine_kernel(acc_ref, m_ref, l_ref, q_ref, kn_ref, vn_ref, lfn_ref, seg_ref, exp_ref, o_ref):
    expand = exp_ref[...]
    ex = lambda x: _dot_split(_split2(x), expand)
    s_new = _dot_split(_split2(q_ref[...].astype(F32) * kn_ref[...]), seg_ref[...])
    m_c = m_ref[...] + lfn_ref[...]
    m_all = jnp.maximum(m_c, s_new)
    w_c = jnp.exp(m_c - m_all)
    w_n = jnp.exp(s_new - m_all)
    den = l_ref[...] * w_c + w_n
    o = (acc_ref[...] * ex(w_c) + vn_ref[...] * ex(w_n)) / ex(den)
    o_ref[...] = o.astype(BF16)


def _decode_combine(acc, m, l, q, kn, vn, lfn, seg, expand):
    shapes = [(N_SAMPLE, D_MODEL), (N_SAMPLE, LANES), (N_SAMPLE, LANES), (N_SAMPLE, D_MODEL),
              (N_SAMPLE, D_MODEL), (N_SAMPLE, D_MODEL), (N_SAMPLE, LANES), (D_MODEL, LANES), (LANES, D_MODEL)]
    return pl.pallas_call(
        _decode_combine_kernel,
        grid=(1,),
        in_specs=[_full_spec(s) for s in shapes],
        out_specs=_full_spec((N_SAMPLE, D_MODEL)),
        out_shape=jax.ShapeDtypeStruct((N_SAMPLE, D_MODEL), BF16),
        name="attn_decode_combine",
    )(acc, m, l, q, kn, vn, lfn, seg, expand)


def _pairs_to_rows(xp):
    return jnp.transpose(xp, (1, 0, 2)).reshape(xp.shape[1], D_MODEL)


def kernel(x_prompt, x_sample, state_ssm_re, state_ssm_im, cache_k, cache_v, cache_logf, state_conv, page_table, meta_tokens, norm_mix_g, norm_ffn_g, ssm_a_re, ssm_a_im, ssm_b_re, ssm_b_im, ssm_c_re, ssm_c_im, ssm_d, ssm_log_step, ssm_w_glu, attn_w_qkv, attn_w_f, attn_b_f, attn_q_g, attn_k_g, attn_w_o, ffn_w_up, ffn_w_gate, ffn_conv_w, ffn_conv_b, ffn_w_down):
    n_body = N_BATCH * SEQ
    xb = x_prompt.reshape(n_body, D_MODEL)
    pad_rows = SMALL_ROWS - N_SAMPLE - N_META
    xs = jnp.concatenate([x_sample.reshape(N_SAMPLE, D_MODEL), meta_tokens.astype(F32),
                          jnp.zeros((pad_rows, D_MODEL), F32)], axis=0)
    meta_rows = slice(META_ROW0, META_ROW0 + N_META)

    g_mix = norm_mix_g.astype(F32).reshape(2, 1, D_MODEL)
    g_ffn = norm_ffn_g.astype(F32).reshape(2, 1, D_MODEL)
    w_glu = ssm_w_glu.astype(BF16)
    w_up = ffn_w_up.astype(BF16)
    w_gate = ffn_w_gate.astype(BF16)
    w_down = ffn_w_down.astype(BF16)
    conv_w = jnp.pad(ffn_conv_w.astype(F32), ((0, 0), (0, SUBLANES - CONV_W), (0, 0)))
    conv_b = ffn_conv_b.astype(F32).reshape(2, 1, D_FF)

    s5w = _s5_weights(ssm_a_re, ssm_a_im, ssm_b_re, ssm_b_im, ssm_c_re, ssm_c_im, ssm_d, ssm_log_step)
    ub = _norm_shuffle(xb, g_mix[0])
    u_small = _norm_rows(xs, g_mix[0])
    n_mc = N_META // S5_CHUNK
    um = u_small[meta_rows].reshape(n_mc, S5_CHUNK, S5_SETS, LANES)
    um = jnp.transpose(um, (2, 0, 1, 3)).reshape(S5_SETS, n_mc, 1, S5_CHUNK * LANES)
    um = jnp.broadcast_to(um, (S5_SETS, n_mc, N_BATCH, S5_CHUNK * LANES)).reshape(S5_SETS, N_META, S5_CHUNK * LANES)
    yb, ym, ys, hpr, hpi, hsr, hsi = _s5(
        ub, um, u_small, state_ssm_re.astype(F32).reshape(N_SAMPLE, -1),
        state_ssm_im.astype(F32).reshape(N_SAMPLE, -1), s5w)
    x1b = _glu_res(xb, yb, w_glu)
    ym_tok = ym.reshape(S5_SETS, n_mc, N_BATCH, S5_CHUNK, LANES)[:, :, 0]
    ym_tok = jnp.transpose(ym_tok, (1, 2, 0, 3)).reshape(N_META, D_MODEL)
    y_small = jnp.concatenate([ys, ym_tok, jnp.zeros((pad_rows, D_MODEL), BF16)], axis=0)
    x1s = _glu_res_rows(xs, y_small, w_glu)

    def ffn_layer(i, xb_in, xs_in, ab, a_small, wa):
        sc = state_conv[i].astype(F32)
        pad = ((0, SMALL_ROWS - N_SAMPLE), (0, 0))
        xs_out, h_small = _ffn_small(xs_in, a_small, wa, g_ffn[i], w_up[i], w_gate[i], conv_w[i], conv_b[i],
                                     w_down[i], jnp.pad(sc[:, 0], pad), jnp.pad(sc[:, 1], pad))
        init = jnp.concatenate([jnp.zeros((SUBLANES - 2, D_FF), F32),
                                h_small[META_ROW0 + N_META - 2:META_ROW0 + N_META]], axis=0)
        xb_out, hl = _ffn(xb_in, ab, wa, g_ffn[i], w_up[i], w_gate[i], conv_w[i], conv_b[i], w_down[i], init)
        tps = SEQ // FFN_TM
        conv_p = hl.reshape(N_BATCH, tps, SUBLANES, D_FF)[:, tps - 1, SUBLANES - 2:, :]
        conv_s = jnp.stack([sc[:, 1], h_small[:N_SAMPLE]], axis=1)
        return xb_out, xs_out, conv_p, conv_s

    x2b, x2s, conv_p0, conv_s0 = ffn_layer(0, x1b, x1s, None, None, None)

    w_qkv = attn_w_qkv.astype(BF16)
    wf = jnp.pad(attn_w_f.astype(BF16), ((0, 0), (0, LANES - N_HEADS)))
    bf = jnp.pad(attn_b_f.astype(F32), (0, LANES - N_HEADS)).reshape(1, LANES)
    qg = (jnp.tile(attn_q_g.astype(F32), N_HEADS) * (HEAD_DIM ** -0.5)).reshape(1, D_MODEL)
    kg = jnp.tile(attn_k_g.astype(F32), N_HEADS).reshape(1, D_MODEL)
    head_of_lane = jnp.arange(D_MODEL) // HEAD_DIM
    onehot = (head_of_lane[:, None] == jnp.arange(LANES)[None, :])
    seg = (onehot.astype(F32) / HEAD_DIM).astype(BF16)
    expand = onehot.T.astype(BF16)
    tri_body = (jnp.arange(FFN_TM)[:, None] >= jnp.arange(FFN_TM)[None, :]).astype(BF16)
    r = jnp.arange(SMALL_ROWS)
    tri_small = ((r[:, None] >= r[None, :]) & (r[None, :] >= META_ROW0) & (r[:, None] < META_ROW0 + N_META))
    tri_small = tri_small.astype(BF16)

    swp = expand.reshape(LANES // 2, 2, D_MODEL)[:, ::-1].reshape(LANES, D_MODEL)

    ks_f32, vs_f32, qts, kas, vts, lfs, cts = _qkv(x2s, g_mix[1], w_qkv, wf, bf, qg, kg, seg, expand, swp,
                                                   tri_small, jnp.zeros((1, LANES), F32), tm=SMALL_ROWS)
    c_meta_total = cts[:, META_ROW0 + N_META - 1].reshape(1, LANES)
    kb_f32, vb_f32, qtb, kab, vtb, lfb, ctb = _qkv(x2b, g_mix[1], w_qkv, wf, bf, qg, kg, seg, expand, swp,
                                                   tri_body, c_meta_total, tm=FFN_TM)

    nblk = SEQ // ATT_T
    ct = ctb[:N_HEADS].reshape(N_PAIRS, 2, N_BATCH, nblk, ATT_T)
    ct = jnp.transpose(ct, (2, 0, 3, 1, 4))
    ct = jnp.pad(ct, ((0, 0), (0, 0), (0, 0), (0, SUBLANES - 2), (0, 0)))
    cmt = cts[:N_HEADS, meta_rows].reshape(N_PAIRS, 2, N_META)
    cmt = jnp.pad(cmt, ((0, 0), (0, SUBLANES - 2), (0, 0)))
    qtm = qts[:, 0, :, meta_rows]
    kam = kas[:, meta_rows]
    vtm = vts[:, 0, :, meta_rows]
    ob = _attn(qtb, kab, vtb, ct, kam, vtm)
    o_meta = _meta_attention(qtm, kam, vtm, cmt)

    n_pool = cache_k.shape[0]
    cache_kt = jnp.transpose(cache_k, (0, 2, 3, 1)).reshape(n_pool, D_MODEL, PAGE)
    cache_vt = jnp.transpose(cache_v, (0, 2, 3, 1)).reshape(n_pool, D_MODEL, PAGE)
    cache_lft = jnp.transpose(cache_logf.astype(F32), (0, 2, 1))
    qt_pairs = qts[:, 0, :, :N_SAMPLE].reshape(N_PAIRS, 2, 2, HEAD_DIM, N_SAMPLE)
    qt_dec = jnp.stack([qt_pairs[:, 0, 0], qt_pairs[:, 1, 1]], axis=1).reshape(D_MODEL, N_SAMPLE)
    low = (jnp.arange(PAGE)[:, None] > jnp.arange(PAGE)[None, :]).astype(BF16)
    acc_t, m_t, l_t = _decode(page_table.astype(jnp.int32), qt_dec.astype(F32), cache_kt, cache_vt, cache_lft, low)
    pad_heads = lambda x, v: jnp.pad(x.T, ((0, 0), (0, LANES - N_HEADS)), constant_values=v)
    o_dec = _decode_combine(acc_t.T, pad_heads(m_t, 0.0), pad_heads(l_t, 1.0), qt_dec.T, ks_f32[:N_SAMPLE],
                            vs_f32[:N_SAMPLE], lfs[:N_SAMPLE], (onehot.astype(BF16)), expand)
    o_small = jnp.concatenate([o_dec, o_meta, jnp.zeros((pad_rows, D_MODEL), BF16)], axis=0)

    w_o = attn_w_o.astype(BF16)
    x4b, x4s, conv_p1, conv_s1 = ffn_layer(1, x2b, x2s, ob, o_small, w_o)

    def with_meta(small, body, width):
        m = jnp.broadcast_to(small[meta_rows][None], (N_BATCH, N_META, width))
        return jnp.concatenate([m, body.reshape(N_BATCH, SEQ, width)], axis=1)

    k_p = with_meta(ks_f32, kb_f32, D_MODEL).reshape(N_BATCH, SEQ + N_META, N_HEADS, HEAD_DIM)
    v_p = with_meta(vs_f32, vb_f32, D_MODEL).reshape(N_BATCH, SEQ + N_META, N_HEADS, HEAD_DIM)
    lf_p = with_meta(lfs[:, :N_HEADS], lfb[:, :N_HEADS], N_HEADS)
    return (x4b.reshape(N_BATCH, SEQ, D_MODEL),
            x4s[:N_SAMPLE].reshape(N_SAMPLE, 1, D_MODEL),
            hpr.reshape(N_BATCH, N_GROUPS, STATE_P), hpi.reshape(N_BATCH, N_GROUPS, STATE_P),
            hsr.reshape(N_SAMPLE, N_GROUPS, STATE_P), hsi.reshape(N_SAMPLE, N_GROUPS, STATE_P),
            k_p, v_p, lf_p,
            ks_f32[:N_SAMPLE].reshape(N_SAMPLE, 1, N_HEADS, HEAD_DIM),
            vs_f32[:N_SAMPLE].reshape(N_SAMPLE, 1, N_HEADS, HEAD_DIM),
            lfs[:N_SAMPLE, :N_HEADS].reshape(N_SAMPLE, 1, N_HEADS),
            jnp.stack([conv_p0, conv_p1]), jnp.stack([conv_s0, conv_s1]))
```

```python
import functools

import jax
import jax.numpy as jnp
from jax import lax
from jax.experimental import pallas as pl
from jax.experimental.pallas import tpu as pltpu

F32 = jnp.float32
BF16 = jnp.bfloat16

D_MODEL = 1024
N_BATCH = 8
SEQ = 2048
N_SAMPLE = 128
N_META = 16
SMALL_ROWS = 256
META_ROW0 = 128
N_GROUPS = 64
STATE_P = 64
SSM_GROUP = 16
N_HEADS = 16
HEAD_DIM = 64
N_PAIRS = N_HEADS // 2
D_FF = 2816
CONV_W = 3
PAGE = 128
EPS = 1e-6
NEG_INF = -1e30

LANES = 128
SUBLANES = 8
S5_CHUNK = 8
S5_SETS = D_MODEL // LANES
SET_STATE = (LANES // SSM_GROUP) * STATE_P
N_CHUNKS = SEQ // S5_CHUNK
VMEM_LIMIT = 56 * 1024 * 1024


def _cparams(sem):
    return pltpu.CompilerParams(dimension_semantics=sem, vmem_limit_bytes=VMEM_LIMIT)


def _const_spec(shape):
    nd = len(shape)
    return pl.BlockSpec(shape, lambda *_: (0,) * nd, pipeline_mode=pl.Buffered(1))


def _full_spec(shape):
    nd = len(shape)
    return pl.BlockSpec(shape, lambda *_: (0,) * nd)


def _dot(a, b):
    return jnp.dot(a, b, preferred_element_type=F32)


def _dot_f32(a, b):
    return jnp.dot(a, b, preferred_element_type=F32, precision=lax.Precision.HIGHEST)


def _split2(x):
    hi = x.astype(BF16)
    lo = (x - hi.astype(F32)).astype(BF16)
    return hi, lo


def _split3(x):
    hi = x.astype(BF16)
    r = x - hi.astype(F32)
    mid = r.astype(BF16)
    lo = (r - mid.astype(F32)).astype(BF16)
    return hi, mid, lo


def _dot_split(parts, w):
    acc = _dot(parts[0], w)
    for p in parts[1:]:
        acc = acc + _dot(p, w)
    return acc


def _rmsnorm(x, g):
    ms = jnp.mean(x * x, axis=-1, keepdims=True)
    return x * lax.rsqrt(ms + EPS) * g


def _gelu_tanh(x):
    c = 0.7978845608028654
    return 0.5 * x * (1.0 + jnp.tanh(c * (x + 0.044715 * (x * x * x))))


def _sigmoid(x):
    return 1.0 / (1.0 + jnp.exp(-x))


def _norm_shuffle_kernel(x_ref, g_ref, o_ref, scr):
    rb = o_ref.shape[1]
    u = _rmsnorm(x_ref[...], g_ref[...])
    for j in range(S5_SETS):
        scr[j] = u[:, j * LANES:(j + 1) * LANES]
    for j in range(S5_SETS):
        for s in range(S5_CHUNK):
            o_ref[j, :, s * LANES:(s + 1) * LANES] = scr[j, pl.ds(s, rb, stride=S5_CHUNK), :].astype(BF16)


def _norm_shuffle(x, g):
    rows = x.shape[0] // S5_CHUNK
    rb = 128
    return pl.pallas_call(
        _norm_shuffle_kernel,
        grid=(rows // rb,),
        in_specs=[pl.BlockSpec((rb * S5_CHUNK, D_MODEL), lambda i: (i, 0)),
                  _const_spec((1, D_MODEL))],
        out_specs=pl.BlockSpec((S5_SETS, rb, S5_CHUNK * LANES), lambda i: (0, i, 0)),
        out_shape=jax.ShapeDtypeStruct((S5_SETS, rows, S5_CHUNK * LANES), BF16),
        scratch_shapes=[pltpu.VMEM((S5_SETS, rb * S5_CHUNK, LANES), F32)],
        compiler_params=_cparams(("parallel",)),
        name="norm_shuffle",
    )(x, g)


def _norm_rows_kernel(x_ref, g_ref, o_ref):
    o_ref[...] = _rmsnorm(x_ref[...], g_ref[...]).astype(BF16)


def _norm_rows(x, g):
    rows = x.shape[0]
    return pl.pallas_call(
        _norm_rows_kernel,
        grid=(1,),
        in_specs=[_full_spec((rows, D_MODEL)), _full_spec((1, D_MODEL))],
        out_specs=_full_spec((rows, D_MODEL)),
        out_shape=jax.ShapeDtypeStruct((rows, D_MODEL), BF16),
        name="norm_rows",
    )(x, g)


S5_ROW_BLOCK = 512
N_SLABS = SET_STATE // LANES
SCAN_PITCH = N_CHUNKS + SUBLANES


def _s5_kernel(ub_ref, um_ref, us_ref, h0r_ref, h0i_ref, wt_ref, wb_ref, wc_ref, a8_ref, dt_ref,
               wb1_ref, wc1_ref, a1_ref, d1_ref,
               yb_ref, ym_ref, ys_ref, hpr_ref, hpi_ref, hsr_ref, hsi_ref, xr_scr, xi_scr):
    rows = ub_ref.shape[1]
    wb = wb_ref[0]
    wt = wt_ref[0]
    wc = wc_ref[0]
    dt = dt_ref[0]

    seq_per_block = S5_ROW_BLOCK // N_CHUNKS

    def scratch_rows(n):
        return slice(n * SCAN_PITCH, n * SCAN_PITCH + N_CHUNKS)

    for r in range(rows // S5_ROW_BLOCK):
        x = _dot(ub_ref[0, r * S5_ROW_BLOCK:(r + 1) * S5_ROW_BLOCK, :], wb)
        for q in range(seq_per_block):
            xs = x[q * N_CHUNKS:(q + 1) * N_CHUNKS]
            ss = scratch_rows(r * seq_per_block + q)
            for k in range(N_SLABS):
                xr_scr[k, ss, :] = xs[:, k * LANES:(k + 1) * LANES]
                xi_scr[k, ss, :] = xs[:, SET_STATE + k * LANES:SET_STATE + (k + 1) * LANES]
    xm = _dot(um_ref[0], wb)

    a8r = jnp.broadcast_to(a8_ref[0, 0:1, :], (SUBLANES, SET_STATE))
    a8i = jnp.broadcast_to(a8_ref[0, 1:2, :], (SUBLANES, SET_STATE))

    hr = jnp.zeros((SUBLANES, SET_STATE), F32)
    hi = jnp.zeros((SUBLANES, SET_STATE), F32)
    hm_prev = []
    for cm in range(N_META // S5_CHUNK):
        hm_prev.append(jnp.concatenate([hr, hi], axis=1))
        xr = xm[cm * SUBLANES:(cm + 1) * SUBLANES, :SET_STATE]
        xi = xm[cm * SUBLANES:(cm + 1) * SUBLANES, SET_STATE:]
        hr, hi = a8r * hr - a8i * hi + xr, a8r * hi + a8i * hr + xi

    ar = [a8r[:, k * LANES:(k + 1) * LANES] for k in range(N_SLABS)]
    ai = [a8i[:, k * LANES:(k + 1) * LANES] for k in range(N_SLABS)]

    def step(c, carry):
        hrs, his = carry
        nr, ni = [], []
        for k in range(N_SLABS):
            idx = (k, pl.ds(c, N_BATCH, stride=SCAN_PITCH), slice(None))
            xr = xr_scr[idx]
            xi = xi_scr[idx]
            xr_scr[idx] = hrs[k]
            xi_scr[idx] = his[k]
            nr.append(ar[k] * hrs[k] - ai[k] * his[k] + xr)
            ni.append(ar[k] * his[k] + ai[k] * hrs[k] + xi)
        return tuple(nr), tuple(ni)

    init = (tuple(hr[:, k * LANES:(k + 1) * LANES] for k in range(N_SLABS)),
            tuple(hi[:, k * LANES:(k + 1) * LANES] for k in range(N_SLABS)))
    hrs, his = lax.fori_loop(0, N_CHUNKS, step, init)
    hpr_ref[...] = jnp.concatenate(hrs, axis=1)
    hpi_ref[...] = jnp.concatenate(his, axis=1)

    def emit(u, hprev):
        y = _dot(u, wt) + _dot(hprev.astype(BF16), wc) + dt * u.astype(F32)
        return _gelu_tanh(y).astype(BF16)

    for r in range(rows // S5_ROW_BLOCK):
        rs = slice(r * S5_ROW_BLOCK, (r + 1) * S5_ROW_BLOCK)
        hprev = jnp.concatenate(
            [jnp.concatenate([xr_scr[k, scratch_rows(r * seq_per_block + q), :] for k in range(N_SLABS)]
                             + [xi_scr[k, scratch_rows(r * seq_per_block + q), :] for k in range(N_SLABS)], axis=1)
             for q in range(seq_per_block)], axis=0)
        yb_ref[0, rs, :] = emit(ub_ref[0, rs, :], hprev)
    ym_ref[0] = emit(um_ref[0], jnp.concatenate(hm_prev, axis=0))

    us = us_ref[...]
    xs = _dot(us, wb1_ref[0])
    a1r = a1_ref[0, 0:1, :]
    a1i = a1_ref[0, 1:2, :]
    h0r = h0r_ref[...]
    h0i = h0i_ref[...]
    hsr = a1r * h0r - a1i * h0i + xs[:, :SET_STATE]
    hsi = a1r * h0i + a1i * h0r + xs[:, SET_STATE:]
    hsr_ref[...] = hsr
    hsi_ref[...] = hsi
    hs = jnp.concatenate([hsr, hsi], axis=1).astype(BF16)
    ys = _dot(hs, wc1_ref[0]) + d1_ref[0] * us.astype(F32)
    ys_ref[...] = _gelu_tanh(ys).astype(BF16)


def _set_spec(shape):
    return pl.BlockSpec((1,) + shape, lambda j: (j,) + (0,) * len(shape))


def _s5(ub, um, u_small, h0r, h0i, w):
    rows = ub.shape[1]
    cw = S5_CHUNK * LANES
    out_shapes = (
        jax.ShapeDtypeStruct((S5_SETS, rows, cw), BF16),
        jax.ShapeDtypeStruct((S5_SETS, N_META, cw), BF16),
        jax.ShapeDtypeStruct((N_SAMPLE, D_MODEL), BF16),
        jax.ShapeDtypeStruct((N_BATCH, S5_SETS * SET_STATE), F32),
        jax.ShapeDtypeStruct((N_BATCH, S5_SETS * SET_STATE), F32),
        jax.ShapeDtypeStruct((N_SAMPLE, S5_SETS * SET_STATE), F32),
        jax.ShapeDtypeStruct((N_SAMPLE, S5_SETS * SET_STATE), F32),
    )
    return pl.pallas_call(
        _s5_kernel,
        grid=(S5_SETS,),
        in_specs=[
            _set_spec((rows, cw)),
            _set_spec((N_META, cw)),
            pl.BlockSpec((N_SAMPLE, LANES), lambda j: (0, j)),
            pl.BlockSpec((N_SAMPLE, SET_STATE), lambda j: (0, j)),
            pl.BlockSpec((N_SAMPLE, SET_STATE), lambda j: (0, j)),
            _set_spec((cw, cw)),
            _set_spec((cw, 2 * SET_STATE)),
            _set_spec((2 * SET_STATE, cw)),
            _set_spec((2, SET_STATE)),
            _set_spec((1, cw)),
            _set_spec((LANES, 2 * SET_STATE)),
            _set_spec((2 * SET_STATE, LANES)),
            _set_spec((2, SET_STATE)),
            _set_spec((1, LANES)),
        ],
        out_specs=(
            _set_spec((rows, cw)),
            _set_spec((N_META, cw)),
            pl.BlockSpec((N_SAMPLE, LANES), lambda j: (0, j)),
            pl.BlockSpec((N_BATCH, SET_STATE), lambda j: (0, j)),
            pl.BlockSpec((N_BATCH, SET_STATE), lambda j: (0, j)),
            pl.BlockSpec((N_SAMPLE, SET_STATE), lambda j: (0, j)),
            pl.BlockSpec((N_SAMPLE, SET_STATE), lambda j: (0, j)),
        ),
        out_shape=out_shapes,
        scratch_shapes=[pltpu.VMEM((N_SLABS, N_BATCH * SCAN_PITCH, LANES), F32),
                        pltpu.VMEM((N_SLABS, N_BATCH * SCAN_PITCH, LANES), F32)],
        compiler_params=_cparams(("parallel",)),
        name="s5_mixer",
    )(ub, um, u_small, h0r, h0i, w["wt"], w["wb"], w["wc"], w["a8"], w["dt"],
      w["wb1"], w["wc1"], w["a1"], w["d1"])


def _s5_discretize(a_re, a_im, log_step):
    dt = jnp.exp(log_step)
    mag = jnp.exp(a_re * dt)
    ab_re = mag * jnp.cos(a_im * dt)
    ab_im = mag * jnp.sin(a_im * dt)
    den = a_re * a_re + a_im * a_im
    n_re = ab_re - 1.0
    n_im = ab_im
    f_re = (n_re * a_re + n_im * a_im) / den
    f_im = (n_im * a_re - n_re * a_im) / den
    return ab_re, ab_im, f_re, f_im


def _s5_weights_kernel(arow_ref, acr_ref, aci_ref, acl_ref, br_ref, bi_ref, cr_ref, ci_ref,
                       wt_ref, wb_ref, wc_ref, a8_ref, wb1_ref, wc1_ref, a1_ref):
    arow = arow_ref[0]
    abr, abi, fr, fi = _s5_discretize(arow[0:1], arow[1:2], arow[2:3])
    abr_c, abi_c, _, _ = _s5_discretize(acr_ref[0], aci_ref[0], acl_ref[0])
    br = br_ref[0]
    bi = bi_ref[0]
    b0r = fr * br - fi * bi
    b0i = fr * bi + fi * br
    cr = cr_ref[0]
    ci = ci_ref[0]
    wt_ref[0] = jnp.zeros(wt_ref.shape[1:], BF16)
    pr = jnp.ones_like(abr)
    pi = jnp.zeros_like(abr)
    pcr, pci = abr_c, abi_c
    for k in range(S5_CHUNK):
        bkr = pr * b0r - pi * b0i
        bki = pr * b0i + pi * b0r
        s = S5_CHUNK - 1 - k
        wb_ref[0, s * LANES:(s + 1) * LANES, :SET_STATE] = bkr.astype(BF16)
        wb_ref[0, s * LANES:(s + 1) * LANES, SET_STATE:] = bki.astype(BF16)
        kern = (_dot_f32(bkr, cr) - _dot_f32(bki, ci)).astype(BF16)
        for s2 in range(S5_CHUNK - k):
            t2 = s2 + k
            wt_ref[0, s2 * LANES:(s2 + 1) * LANES, t2 * LANES:(t2 + 1) * LANES] = kern
        wc_ref[0, :SET_STATE, k * LANES:(k + 1) * LANES] = (pcr * cr - pci * ci).astype(BF16)
        wc_ref[0, SET_STATE:, k * LANES:(k + 1) * LANES] = (-(pcr * ci + pci * cr)).astype(BF16)
        pr, pi = pr * abr - pi * abi, pr * abi + pi * abr
        pcr, pci = pcr * abr_c - pci * abi_c, pcr * abi_c + pci * abr_c
    a8_ref[0, 0:1, :] = pr
    a8_ref[0, 1:2, :] = pi
    a1_ref[0, 0:1, :] = abr
    a1_ref[0, 1:2, :] = abi
    wb1_ref[0, :, :SET_STATE] = b0r.astype(BF16)
    wb1_ref[0, :, SET_STATE:] = b0i.astype(BF16)
    wc1_ref[0, :SET_STATE, :] = cr.astype(BF16)
    wc1_ref[0, SET_STATE:, :] = (-ci).astype(BF16)


def _s5_weights(a_re, a_im, b_re, b_im, c_re, c_im, d_skip, log_step):
    gs = LANES // SSM_GROUP
    cw = S5_CHUNK * LANES
    eye = jnp.eye(gs, dtype=F32)
    per_state = lambda v: v.astype(F32).reshape(S5_SETS, SET_STATE)
    a_re_s, a_im_s = per_state(a_re), per_state(a_im)
    ls_s = per_state(jnp.broadcast_to(log_step[:, None], (N_GROUPS, STATE_P)))
    arow = jnp.stack([a_re_s, a_im_s, ls_s] + [jnp.zeros_like(ls_s)] * (SUBLANES - 3), axis=1)
    col = lambda v: jnp.broadcast_to(v[:, :, None], (S5_SETS, SET_STATE, LANES))

    def blockdiag_b(b):
        b = b.astype(F32).reshape(S5_SETS, gs, STATE_P, SSM_GROUP)
        return jnp.einsum('jgpi,gh->jgihp', b, eye).reshape(S5_SETS, LANES, SET_STATE)

    def blockdiag_c(c):
        c = c.astype(F32).reshape(S5_SETS, gs, SSM_GROUP, STATE_P)
        return jnp.einsum('jgop,gh->jgpho', c, eye).reshape(S5_SETS, SET_STATE, LANES)

    out_shapes = (
        jax.ShapeDtypeStruct((S5_SETS, cw, cw), BF16),
        jax.ShapeDtypeStruct((S5_SETS, cw, 2 * SET_STATE), BF16),
        jax.ShapeDtypeStruct((S5_SETS, 2 * SET_STATE, cw), BF16),
        jax.ShapeDtypeStruct((S5_SETS, 2, SET_STATE), F32),
        jax.ShapeDtypeStruct((S5_SETS, LANES, 2 * SET_STATE), BF16),
        jax.ShapeDtypeStruct((S5_SETS, 2 * SET_STATE, LANES), BF16),
        jax.ShapeDtypeStruct((S5_SETS, 2, SET_STATE), F32),
    )
    wt, wb, wc, a8, wb1, wc1, a1 = pl.pallas_call(
        _s5_weights_kernel,
        grid=(S5_SETS,),
        in_specs=[_set_spec((SUBLANES, SET_STATE)), _set_spec((SET_STATE, LANES)), _set_spec((SET_STATE, LANES)),
                  _set_spec((SET_STATE, LANES)), _set_spec((LANES, SET_STATE)), _set_spec((LANES, SET_STATE)),
                  _set_spec((SET_STATE, LANES)), _set_spec((SET_STATE, LANES))],
        out_specs=tuple(_set_spec(s.shape[1:]) for s in out_shapes),
        out_shape=out_shapes,
        compiler_params=_cparams(("parallel",)),
        name="s5_weights",
    )(arow, col(a_re_s), col(a_im_s), col(ls_s), blockdiag_b(b_re), blockdiag_b(b_im),
      blockdiag_c(c_re), blockdiag_c(c_im))
    d1 = d_skip.astype(F32).reshape(S5_SETS, 1, LANES)
    return {"wt": wt, "wb": wb, "wc": wc, "a8": a8, "dt": jnp.tile(d1, (1, 1, S5_CHUNK)),
            "wb1": wb1, "wc1": wc1, "a1": a1, "d1": d1}


def _glu(y, w):
    z = _dot(y, w)
    return z[:, :D_MODEL] * _sigmoid(z[:, D_MODEL:])


def _glu_res_kernel(x_ref, y_ref, w_ref, o_ref, scr):
    rb = y_ref.shape[1]
    w = w_ref[...]
    for s in range(S5_CHUNK):
        y = jnp.concatenate([y_ref[j, :, s * LANES:(s + 1) * LANES] for j in range(S5_SETS)], axis=1)
        o = _glu(y, w)
        for j in range(S5_SETS):
            scr[j, pl.ds(s, rb, stride=S5_CHUNK), :] = o[:, j * LANES:(j + 1) * LANES]
    for j in range(S5_SETS):
        cs = slice(j * LANES, (j + 1) * LANES)
        o_ref[:, cs] = x_ref[:, cs] + scr[j]


def _glu_res(x, y, w_glu):
    rows = y.shape[1]
    rb = 128
    tok_spec = pl.BlockSpec((rb * S5_CHUNK, D_MODEL), lambda i: (i, 0))
    return pl.pallas_call(
        _glu_res_kernel,
        grid=(rows // rb,),
        in_specs=[tok_spec, pl.BlockSpec((S5_SETS, rb, S5_CHUNK * LANES), lambda i: (0, i, 0)),
                  _const_spec((D_MODEL, 2 * D_MODEL))],
        out_specs=tok_spec,
        out_shape=jax.ShapeDtypeStruct(x.shape, F32),
        scratch_shapes=[pltpu.VMEM((S5_SETS, rb * S5_CHUNK, LANES), F32)],
        compiler_params=_cparams(("parallel",)),
        name="glu_res",
    )(x, y, w_glu)


def _glu_res_rows_kernel(x_ref, y_ref, w_ref, o_ref):
    o_ref[...] = x_ref[...] + _glu(y_ref[...], w_ref[...])


def _glu_res_rows(x, y, w_glu):
    rows = x.shape[0]
    return pl.pallas_call(
        _glu_res_rows_kernel,
        grid=(1,),
        in_specs=[_full_spec((rows, D_MODEL)), _full_spec((rows, D_MODEL)), _full_spec((D_MODEL, 2 * D_MODEL))],
        out_specs=_full_spec((rows, D_MODEL)),
        out_shape=jax.ShapeDtypeStruct((rows, D_MODEL), F32),
        name="glu_res_rows",
    )(x, y, w_glu)


FFN_TM = 512
FFN_FC = D_FF // 2


def _conv_act(h, gt, p1, p2, cw, cb):
    hc = cb + cw[0:1, :] * p2
    hc = hc + cw[1:2, :] * p1
    hc = hc + cw[2:3, :] * h
    return (hc * _sigmoid(hc) * gt).astype(BF16)


def _ffn_kernel(*refs, proj, tiles_per_seq):
    if proj:
        (x_ref, a_ref, wa_ref, g_ref, wu_ref, wg_ref, cw_ref, cb_ref, wd_ref, init_ref,
         o_ref, hl_ref, carry_scr, act_scr) = refs
    else:
        (x_ref, g_ref, wu_ref, wg_ref, cw_ref, cb_ref, wd_ref, init_ref,
         o_ref, hl_ref, carry_scr, act_scr) = refs
    i = pl.program_id(0)

    @pl.when(i % tiles_per_seq == 0)
    def _():
        carry_scr[...] = init_ref[...]

    x = x_ref[...]
    if proj:
        a = jnp.concatenate([a_ref[p] for p in range(N_PAIRS)], axis=1)
        x = x + _dot(a, wa_ref[...])
    u = _rmsnorm(x, g_ref[...]).astype(BF16)
    tm = x.shape[0]
    row8 = lax.broadcasted_iota(jnp.int32, (SUBLANES, 1), 0)
    out = x
    for c in range(D_FF // FFN_FC):
        cs = slice(c * FFN_FC, (c + 1) * FFN_FC)
        h = _dot(u, wu_ref[:, cs])
        gt = _dot(u, wg_ref[:, cs])
        cw = cw_ref[:, cs]
        cb = cb_ref[:, cs]
        carry = carry_scr[:, cs]
        p1 = pltpu.roll(h, 1, axis=0)
        p2 = pltpu.roll(h, 2, axis=0)
        act_scr[...] = _conv_act(h, gt, p1, p2, cw, cb)
        c1 = pltpu.roll(carry, 1, axis=0)
        c2 = pltpu.roll(carry, 2, axis=0)
        p1h = jnp.where(row8 < 1, c1, p1[:SUBLANES])
        p2h = jnp.where(row8 < 2, c2, p2[:SUBLANES])
        act_scr[0:SUBLANES, :] = _conv_act(h[:SUBLANES], gt[:SUBLANES], p1h, p2h, cw, cb)
        last = h[tm - SUBLANES:, :]
        carry_scr[:, cs] = last
        hl_ref[0, :, cs] = last
        out = out + _dot(act_scr[...], wd_ref[cs, :])
    o_ref[...] = out


def _ffn(x, a, wa, g, wu, wg, cw, cb, wd, init):
    rows = x.shape[0]
    tm = FFN_TM
    proj = a is not None
    row_spec = pl.BlockSpec((tm, D_MODEL), lambda i: (i, 0))
    in_specs = [row_spec]
    args = [x]
    if proj:
        in_specs += [pl.BlockSpec((N_PAIRS, tm, LANES), lambda i: (0, i, 0)), _const_spec((D_MODEL, D_MODEL))]
        args += [a, wa]
    in_specs += [_const_spec((1, D_MODEL)), _const_spec((D_MODEL, D_FF)), _const_spec((D_MODEL, D_FF)),
                 _const_spec((SUBLANES, D_FF)), _const_spec((1, D_FF)), _const_spec((D_FF, D_MODEL)),
                 _const_spec((SUBLANES, D_FF))]
    args += [g, wu, wg, cw, cb, wd, init]
    return pl.pallas_call(
        functools.partial(_ffn_kernel, proj=proj, tiles_per_seq=SEQ // tm),
        grid=(rows // tm,),
        in_specs=in_specs,
        out_specs=(row_spec, pl.BlockSpec((1, SUBLANES, D_FF), lambda i: (i, 0, 0))),
        out_shape=(jax.ShapeDtypeStruct((rows, D_MODEL), F32),
                   jax.ShapeDtypeStruct((rows // tm, SUBLANES, D_FF), F32)),
        scratch_shapes=[pltpu.VMEM((SUBLANES, D_FF), F32), pltpu.VMEM((tm, FFN_FC), BF16)],
        compiler_params=_cparams(("arbitrary",)),
        name="ffn_proj" if proj else "ffn",
    )(*args)


def _ffn_small_kernel(*refs, proj):
    if proj:
        (x_ref, a_ref, wa_ref, g_ref, wu_ref, wg_ref, cw_ref, cb_ref, wd_ref, b0_ref, b1_ref,
         o_ref, h_ref) = refs
    else:
        (x_ref, g_ref, wu_ref, wg_ref, cw_ref, cb_ref, wd_ref, b0_ref, b1_ref, o_ref, h_ref) = refs
    x = x_ref[...]
    if proj:
        x = x + _dot(a_ref[...], wa_ref[...])
    u = _rmsnorm(x, g_ref[...]).astype(BF16)
    row = lax.broadcasted_iota(jnp.int32, (SMALL_ROWS, 1), 0)
    is_sample = row < N_SAMPLE
    meta_end = META_ROW0 + N_META
    out = x
    for c in range(D_FF // FFN_FC):
        cs = slice(c * FFN_FC, (c + 1) * FFN_FC)
        h = _dot(u, wu_ref[:, cs])
        gt = _dot(u, wg_ref[:, cs])
        p1 = jnp.where(is_sample, b1_ref[:, cs],
                       jnp.where((row >= META_ROW0 + 1) & (row < meta_end), pltpu.roll(h, 1, axis=0), 0.0))
        p2 = jnp.where(is_sample, b0_ref[:, cs],
                       jnp.where((row >= META_ROW0 + 2) & (row < meta_end), pltpu.roll(h, 2, axis=0), 0.0))
        act = _conv_act(h, gt, p1, p2, cw_ref[:, cs], cb_ref[:, cs])
        h_ref[:, cs] = h
        out = out + _dot(act, wd_ref[cs, :])
    o_ref[...] = out


def _ffn_small(x, a, wa, g, wu, wg, cw, cb, wd, b0, b1):
    proj = a is not None
    in_specs = [_full_spec((SMALL_ROWS, D_MODEL))]
    args = [x]
    if proj:
        in_specs += [_full_spec((SMALL_ROWS, D_MODEL)), _full_spec((D_MODEL, D_MODEL))]
        args += [a, wa]
    in_specs += [_full_spec((1, D_MODEL)), _full_spec((D_MODEL, D_FF)), _full_spec((D_MODEL, D_FF)),
                 _full_spec((SUBLANES, D_FF)), _full_spec((1, D_FF)), _full_spec((D_FF, D_MODEL)),
                 _full_spec((SMALL_ROWS, D_FF)), _full_spec((SMALL_ROWS, D_FF))]
    args += [g, wu, wg, cw, cb, wd, b0, b1]
    return pl.pallas_call(
        functools.partial(_ffn_small_kernel, proj=proj),
        grid=(1,),
        in_specs=in_specs,
        out_specs=(_full_spec((SMALL_ROWS, D_MODEL)), _full_spec((SMALL_ROWS, D_FF))),
        out_shape=(jax.ShapeDtypeStruct((SMALL_ROWS, D_MODEL), F32),
                   jax.ShapeDtypeStruct((SMALL_ROWS, D_FF), F32)),
        compiler_params=_cparams(("arbitrary",)),
        name="ffn_small_proj" if proj else "ffn_small",
    )(*args)


ATT_T = 256
ATT_HPS = 4
DEC_PPS = 8


def _qkv_kernel(x_ref, g_ref, w_ref, wf_ref, bf_ref, qg_ref, kg_ref, seg_ref, exp_ref, swp_ref, tri_ref, cinit_ref,
                k_ref, v_ref, qt_ref, ka_ref, vt_ref, lf_ref, ct_ref, carry_scr, *, tiles_per_seq):
    i = pl.program_id(0)

    @pl.when(i % tiles_per_seq == 0)
    def _():
        carry_scr[...] = jnp.broadcast_to(cinit_ref[...], carry_scr.shape)

    u = _rmsnorm(x_ref[...], g_ref[...]).astype(BF16)
    qkv = _dot(u, w_ref[...])
    seg = seg_ref[...]
    expand = exp_ref[...]

    def head_norm(t, gain):
        ms = _dot_split(_split2(t * t), seg)
        rinv = lax.rsqrt(ms + EPS)
        return t * _dot_split(_split2(rinv), expand) * gain

    q = head_norm(qkv[:, :D_MODEL], qg_ref[...])
    k = head_norm(qkv[:, D_MODEL:2 * D_MODEL], kg_ref[...])
    v = qkv[:, 2 * D_MODEL:]
    k_ref[...] = k
    v_ref[...] = v
    tm = k.shape[0]

    z = _dot(u, wf_ref[...]) + bf_ref[...]
    lf = jnp.minimum(z, 0.0) - jnp.log(1.0 + jnp.exp(-jnp.abs(z)))
    lf_ref[...] = lf
    tri = tri_ref[...]
    c = carry_scr[SUBLANES - 1:SUBLANES, :]
    for part in _split3(lf):
        c = c + _dot(tri, part)
    carry_scr[...] = c[tm - SUBLANES:, :]
    ct_ref[...] = c.T

    swp = swp_ref[...]
    parts = [_dot(part, swp) for part in _split3(-c)]
    l64 = lax.broadcasted_iota(jnp.int32, (1, D_MODEL), 1) % HEAD_DIM
    k_bias = jnp.where(l64 == 0, parts[0], jnp.where(l64 == 1, parts[1], jnp.where(l64 == 2, parts[2], 0.0)))
    q_ones = (l64 < len(parts)).astype(F32)
    lane = lax.broadcasted_iota(jnp.int32, (1, LANES), 1)
    for p in range(N_PAIRS):
        ps = slice(p * LANES, (p + 1) * LANES)
        for e in range(2):
            h = 2 * p + e
            own = (lane >= HEAD_DIM) if e else (lane < HEAD_DIM)
            ka_ref[h] = jnp.where(own, k[:, ps], k_bias[:, ps]).astype(BF16)
            qs = jnp.where(own, q[:, ps], q_ones[:, ps])
            vs = jnp.where(own, v[:, ps], 1.0)
            for b in range(tm // ATT_T):
                bs = slice(b * ATT_T, (b + 1) * ATT_T)
                qt_ref[h, b] = qs[bs].T.astype(BF16)
                vt_ref[h, b] = vs[bs].T.astype(BF16)


def _qkv(x, g, w_qkv, wf, bf, qg, kg, seg, expand, swp, tri, cinit, *, tm):
    rows = x.shape[0]
    row_spec = pl.BlockSpec((tm, D_MODEL), lambda i: (i, 0))
    t_spec = pl.BlockSpec((N_HEADS, tm // ATT_T, LANES, ATT_T), lambda i: (0, i, 0, 0))
    t_shape = jax.ShapeDtypeStruct((N_HEADS, rows // ATT_T, LANES, ATT_T), BF16)
    return pl.pallas_call(
        functools.partial(_qkv_kernel, tiles_per_seq=max(SEQ // tm, 1)),
        grid=(rows // tm,),
        in_specs=[row_spec, _const_spec((1, D_MODEL)), _const_spec((D_MODEL, 3 * D_MODEL)),
                  _const_spec((D_MODEL, LANES)), _const_spec((1, LANES)), _const_spec((1, D_MODEL)),
                  _const_spec((1, D_MODEL)), _const_spec((D_MODEL, LANES)), _const_spec((LANES, D_MODEL)),
                  _const_spec((LANES, D_MODEL)), _const_spec((tm, tm)), _const_spec((1, LANES))],
        out_specs=(row_spec, row_spec, t_spec,
                   pl.BlockSpec((N_HEADS, tm, LANES), lambda i: (0, i, 0)), t_spec,
                   pl.BlockSpec((tm, LANES), lambda i: (i, 0)),
                   pl.BlockSpec((LANES, tm), lambda i: (0, i))),
        out_shape=(jax.ShapeDtypeStruct((rows, D_MODEL), F32), jax.ShapeDtypeStruct((rows, D_MODEL), F32),
                   t_shape, jax.ShapeDtypeStruct((N_HEADS, rows, LANES), BF16), t_shape,
                   jax.ShapeDtypeStruct((rows, LANES), F32), jax.ShapeDtypeStruct((LANES, rows), F32)),
        scratch_shapes=[pltpu.VMEM((SUBLANES, LANES), F32)],
        compiler_params=_cparams(("arbitrary",)),
        name="qkv_proj",
    )(x, g, w_qkv, wf, bf, qg, kg, seg, expand, swp, tri, cinit)


def _softmax_step(s0, vt, cq, m, acc):
    m_new = jnp.maximum(m, jnp.max(s0, axis=0, keepdims=True) + cq)
    p = jnp.exp(s0 - (m_new - cq))
    acc = jnp.exp(m - m_new) * acc + _dot(vt, p.astype(BF16))
    return m_new, acc


def _attn_finish(acc_even, acc_odd):
    o_t = jnp.concatenate([acc_even[:HEAD_DIM] / acc_even[HEAD_DIM:],
                           acc_odd[HEAD_DIM:] / acc_odd[:HEAD_DIM]], axis=0)
    return o_t.T.astype(BF16)


def _page_copies(pt_ref, cache_refs, bufs, sem, it, slot):
    n_pages = pt_ref.shape[1]
    n_sub = n_pages // DEC_PPS
    b = it // n_sub
    t = it % n_sub
    copies = []
    for i in range(DEC_PPS):
        page = pt_ref[b, n_pages - 1 - (t * DEC_PPS + i)]
        for kind, (src, dst) in enumerate(zip(cache_refs, bufs)):
            copies.append(pltpu.make_async_copy(src.at[page], dst.at[slot, i], sem.at[slot, kind]))
    return copies


def _decode_substep(it, n_sub, qd_ref, low_ref, k_pages, v_pages, lf_pages, do_ref, dm_ref, dl_ref,
                    qc_scr, m_scr, l_scr, tail_scr, acc_scr):
    b = it // n_sub
    t = it % n_sub
    mine = lax.broadcasted_iota(jnp.int32, (1, LANES), 1) == b

    @pl.when(it == 0)
    def _():
        do_ref[...] = jnp.zeros_like(do_ref)
        dm_ref[...] = jnp.zeros_like(dm_ref)
        dl_ref[...] = jnp.zeros_like(dl_ref)

    @pl.when(t == 0)
    def _():
        qcol = jnp.sum(jnp.where(mine, qd_ref[...], 0.0), axis=1, keepdims=True)
        qc_scr[...] = jnp.broadcast_to(qcol, (D_MODEL, LANES))
        m_scr[...] = jnp.full(m_scr.shape, NEG_INF, F32)
        l_scr[...] = jnp.zeros_like(l_scr)
        tail_scr[...] = jnp.zeros_like(tail_scr)
        acc_scr[...] = jnp.zeros_like(acc_scr)

    low = low_ref[...]
    tail = tail_scr[...]
    scores = []
    tiles = [[] for _ in range(DEC_PPS)]
    for h in range(N_HEADS):
        tile = [None] * DEC_PPS
        for r in range(h * HEAD_DIM, (h + 1) * HEAD_DIM, SUBLANES):
            rows = slice(r, r + SUBLANES)
            q_rows = qc_scr[rows, :]
            for i in range(DEC_PPS):
                term = k_pages(i, rows) * q_rows
                tile[i] = term if tile[i] is None else tile[i] + term
        for i in range(DEC_PPS):
            tiles[i].append(tile[i])
    for i in range(DEC_PPS):
        s = jnp.sum(jnp.concatenate(tiles[i], axis=0).reshape(N_HEADS, SUBLANES, LANES), axis=1)
        lf = lf_pages(i)
        s = s + tail
        for part in _split3(lf):
            s = s + _dot(part, low)
        scores.append(s)
        tail = tail + jnp.sum(lf, axis=1, keepdims=True)
    tail_scr[...] = tail
    m_old = m_scr[...]
    s_max = scores[0]
    for s in scores[1:]:
        s_max = jnp.maximum(s_max, s)
    m_new = jnp.maximum(m_old, jnp.max(s_max, axis=1, keepdims=True))
    alpha = jnp.exp(m_old - m_new)
    probs = [jnp.exp(s - m_new) for s in scores]
    p_sum = probs[0]
    for p in probs[1:]:
        p_sum = p_sum + p
    l_scr[...] = alpha * l_scr[...] + jnp.sum(p_sum, axis=1, keepdims=True)
    m_scr[...] = m_new
    for h in range(N_HEADS):
        rows = slice(h * HEAD_DIM, (h + 1) * HEAD_DIM)
        acc = acc_scr[rows, :] * alpha[h:h + 1, :]
        for i in range(DEC_PPS):
            acc = acc + v_pages(i, rows) * probs[i][h:h + 1, :]
        acc_scr[rows, :] = acc

    @pl.when(t == n_sub - 1)
    def _():
        do_ref[...] = jnp.where(mine, jnp.sum(acc_scr[...], axis=1, keepdims=True), do_ref[...])
        dm_ref[...] = jnp.where(mine, m_scr[...], dm_ref[...])
        dl_ref[...] = jnp.where(mine, l_scr[...], dl_ref[...])


def _attn_kernel(pt_ref, qt_ref, ka_ref, vt_ref, ct_ref, kam_ref, vtm_ref, qd_ref, low_ref,
                 ck_hbm, cv_hbm, clf_hbm, o_ref, do_ref, dm_ref, dl_ref,
                 kbuf, vbuf, lfbuf, sem, qc_scr, m_scr, l_scr, tail_scr, acc_scr):
    key = lax.broadcasted_iota(jnp.int32, (ATT_T, ATT_T), 0)
    qry = lax.broadcasted_iota(jnp.int32, (ATT_T, ATT_T), 1)

    n_qblk = SEQ // ATT_T
    n_sub = pt_ref.shape[1] // DEC_PPS
    step0 = (pl.program_id(0) * pl.num_programs(1) + pl.program_id(1)) * n_qblk
    total = pl.num_programs(0) * pl.num_programs(1) * n_qblk
    copies = functools.partial(_page_copies, pt_ref, (ck_hbm, cv_hbm, clf_hbm), (kbuf, vbuf, lfbuf), sem)

    @pl.when(step0 == 0)
    def _():
        for c in copies(0, 0):
            c.start()

    def qblock(qi, _):
        it = step0 + qi
        slot = it % 2

        @pl.when(it + 1 < total)
        def _():
            for c in copies(it + 1, 1 - slot):
                c.start()

        for c in copies(it, slot):
            c.wait()
        _decode_substep(it, n_sub, qd_ref, low_ref,
                        lambda i, rows: kbuf[slot, i, rows, :], lambda i, rows: vbuf[slot, i, rows, :],
                        lambda i: lfbuf[slot, i],
                        do_ref, dm_ref, dl_ref, qc_scr, m_scr, l_scr, tail_scr, acc_scr)

        r0 = pl.multiple_of(qi * ATT_T, ATT_T)
        qts = [qt_ref[h, qi] for h in range(ATT_HPS)]
        cqs = [ct_ref[0, h // 2, qi][h % 2:h % 2 + 1, :] for h in range(ATT_HPS)]

        def scores(h, blk):
            c0 = pl.multiple_of(blk * ATT_T, ATT_T)
            return _dot(ka_ref[h, pl.ds(c0, ATT_T), :], qts[h])

        state = []
        for h in range(ATT_HPS):
            s0 = _dot(kam_ref[h], qts[h])
            m = jnp.max(s0, axis=0, keepdims=True) + cqs[h]
            p = jnp.exp(s0 - (m - cqs[h]))
            state += [m, _dot(vtm_ref[h], p.astype(BF16)), scores(h, 0)]

        def kvstep(j, st):
            new = []
            for h in range(ATT_HPS):
                m, acc, s_cur = st[3 * h:3 * h + 3]
                s_next = scores(h, j + 1)
                m, acc = _softmax_step(s_cur, vt_ref[h, j], cqs[h], m, acc)
                new += [m, acc, s_next]
            return tuple(new)

        st = lax.fori_loop(0, qi, kvstep, tuple(state))
        accs = []
        for h in range(ATT_HPS):
            m, acc, s_diag = st[3 * h:3 * h + 3]
            s_diag = jnp.where(key <= qry, s_diag, NEG_INF)
            accs.append(_softmax_step(s_diag, vt_ref[h, qi], cqs[h], m, acc)[1])
        for p in range(ATT_HPS // 2):
            o_ref[p, pl.ds(r0, ATT_T), :] = _attn_finish(accs[2 * p], accs[2 * p + 1])
        return 0

    lax.fori_loop(0, SEQ // ATT_T, qblock, 0)


def _attn(qt, ka, vt, ct, kam, vtm, page_table, qt_dec, low, cache_kt, cache_vt, cache_lft):
    nblk = SEQ // ATT_T
    hps = ATT_HPS
    n_groups = N_HEADS // hps
    n_pages = page_table.shape[1]
    assert N_BATCH * n_groups * nblk == N_SAMPLE * (n_pages // DEC_PPS)
    t_spec = pl.BlockSpec((hps, nblk, LANES, ATT_T), lambda n, g, pt: (g, n, 0, 0))
    const = lambda shape: pl.BlockSpec(shape, lambda n, g, pt: (0,) * len(shape))
    hbm = pl.BlockSpec(memory_space=pl.ANY)
    stat = lambda: pltpu.VMEM((N_HEADS, LANES), F32)
    grid_spec = pltpu.PrefetchScalarGridSpec(
        num_scalar_prefetch=1,
        grid=(N_BATCH, n_groups),
        in_specs=[t_spec,
                  pl.BlockSpec((hps, SEQ, LANES), lambda n, g, pt: (g, n, 0)),
                  t_spec,
                  pl.BlockSpec((1, hps // 2, nblk, SUBLANES, ATT_T), lambda n, g, pt: (n, g, 0, 0, 0)),
                  pl.BlockSpec((hps, N_META, LANES), lambda n, g, pt: (g, 0, 0)),
                  pl.BlockSpec((hps, LANES, N_META), lambda n, g, pt: (g, 0, 0)),
                  const((D_MODEL, N_SAMPLE)), const((PAGE, PAGE)), hbm, hbm, hbm],
        out_specs=(pl.BlockSpec((hps // 2, SEQ, LANES), lambda n, g, pt: (g, n, 0)),
                   const((D_MODEL, N_SAMPLE)), const((N_HEADS, N_SAMPLE)), const((N_HEADS, N_SAMPLE))),
        scratch_shapes=[pltpu.VMEM((2, DEC_PPS, D_MODEL, PAGE), F32),
                        pltpu.VMEM((2, DEC_PPS, D_MODEL, PAGE), F32),
                        pltpu.VMEM((2, DEC_PPS, N_HEADS, PAGE), F32),
                        pltpu.SemaphoreType.DMA((2, 3)),
                        pltpu.VMEM((D_MODEL, LANES), F32), stat(), stat(), stat(),
                        pltpu.VMEM((D_MODEL, LANES), F32)],
    )
    return pl.pallas_call(
        _attn_kernel,
        grid_spec=grid_spec,
        out_shape=(jax.ShapeDtypeStruct((N_PAIRS, N_BATCH * SEQ, LANES), BF16),
                   jax.ShapeDtypeStruct((D_MODEL, N_SAMPLE), F32),
                   jax.ShapeDtypeStruct((N_HEADS, N_SAMPLE), F32),
                   jax.ShapeDtypeStruct((N_HEADS, N_SAMPLE), F32)),
        compiler_params=_cparams(("arbitrary", "arbitrary")),
        name="attn_prompt_decode",
    )(page_table, qt, ka, vt, ct, kam, vtm, qt_dec, low, cache_kt, cache_vt, cache_lft)


def _meta_attn_kernel(qtm_ref, kam_ref, vtm_ref, cmt_ref, o_ref):
    key = lax.broadcasted_iota(jnp.int32, (N_META, N_META), 0)
    qry = lax.broadcasted_iota(jnp.int32, (N_META, N_META), 1)
    for p in range(N_PAIRS):
        accs = []
        for e in range(2):
            h = 2 * p + e
            cq = cmt_ref[p, e:e + 1, :]
            s0 = jnp.where(key <= qry, _dot(kam_ref[h], qtm_ref[h]), NEG_INF)
            m = jnp.max(s0, axis=0, keepdims=True) + cq
            pr = jnp.exp(s0 - (m - cq))
            accs.append(_dot(vtm_ref[h], pr.astype(BF16)))
        o_ref[p] = _attn_finish(*accs)


def _meta_attention(qtm, kam, vtm, cmt):
    pm = (N_PAIRS, N_META, LANES)
    t_shape = (N_HEADS, LANES, N_META)
    o = pl.pallas_call(
        _meta_attn_kernel,
        grid=(1,),
        in_specs=[_full_spec(t_shape), _full_spec((N_HEADS, N_META, LANES)), _full_spec(t_shape),
                  _full_spec((N_PAIRS, SUBLANES, N_META))],
        out_specs=_full_spec(pm),
        out_shape=jax.ShapeDtypeStruct(pm, BF16),
        name="attn_meta",
    )(qtm, kam, vtm, cmt)
    return _pairs_to_rows(o)


def _decode_combine_kernel(acc_ref, m_ref, l_ref, q_ref, kn_ref, vn_ref, lfn_ref, seg_ref, exp_ref, o_ref):
    expand = exp_ref[...]
    ex = lambda x: _dot_split(_split2(x), expand)
    s_new = _dot_split(_split2(q_ref[...].astype(F32) * kn_ref[...]), seg_ref[...])
    m_c = m_ref[...] + lfn_ref[...]
    m_all = jnp.maximum(m_c, s_new)
    w_c = jnp.exp(m_c - m_all)
    w_n = jnp.exp(s_new - m_all)
    den = l_ref[...] * w_c + w_n
    o = (acc_ref[...] * ex(w_c) + vn_ref[...] * ex(w_n)) / ex(den)
    o_ref[...] = o.astype(BF16)


def _decode_combine(acc, m, l, q, kn, vn, lfn, seg, expand):
    shapes = [(N_SAMPLE, D_MODEL), (N_SAMPLE, LANES), (N_SAMPLE, LANES), (N_SAMPLE, D_MODEL),
              (N_SAMPLE, D_MODEL), (N_SAMPLE, D_MODEL), (N_SAMPLE, LANES), (D_MODEL, LANES), (LANES, D_MODEL)]
    return pl.pallas_call(
        _decode_combine_kernel,
        grid=(1,),
        in_specs=[_full_spec(s) for s in shapes],
        out_specs=_full_spec((N_SAMPLE, D_MODEL)),
        out_shape=jax.ShapeDtypeStruct((N_SAMPLE, D_MODEL), BF16),
        name="attn_decode_combine",
    )(acc, m, l, q, kn, vn, lfn, seg, expand)


def _pairs_to_rows(xp):
    return jnp.transpose(xp, (1, 0, 2)).reshape(xp.shape[1], D_MODEL)


def kernel(x_prompt, x_sample, state_ssm_re, state_ssm_im, cache_k, cache_v, cache_logf, state_conv, page_table, meta_tokens, norm_mix_g, norm_ffn_g, ssm_a_re, ssm_a_im, ssm_b_re, ssm_b_im, ssm_c_re, ssm_c_im, ssm_d, ssm_log_step, ssm_w_glu, attn_w_qkv, attn_w_f, attn_b_f, attn_q_g, attn_k_g, attn_w_o, ffn_w_up, ffn_w_gate, ffn_conv_w, ffn_conv_b, ffn_w_down):
    n_body = N_BATCH * SEQ
    xb = x_prompt.reshape(n_body, D_MODEL)
    pad_rows = SMALL_ROWS - N_SAMPLE - N_META
    xs = jnp.concatenate([x_sample.reshape(N_SAMPLE, D_MODEL), meta_tokens.astype(F32),
                          jnp.zeros((pad_rows, D_MODEL), F32)], axis=0)
    meta_rows = slice(META_ROW0, META_ROW0 + N_META)

    g_mix = norm_mix_g.astype(F32).reshape(2, 1, D_MODEL)
    g_ffn = norm_ffn_g.astype(F32).reshape(2, 1, D_MODEL)
    w_glu = ssm_w_glu.astype(BF16)
    w_up = ffn_w_up.astype(BF16)
    w_gate = ffn_w_gate.astype(BF16)
    w_down = ffn_w_down.astype(BF16)
    conv_w = jnp.pad(ffn_conv_w.astype(F32), ((0, 0), (0, SUBLANES - CONV_W), (0, 0)))
    conv_b = ffn_conv_b.astype(F32).reshape(2, 1, D_FF)

    s5w = _s5_weights(ssm_a_re, ssm_a_im, ssm_b_re, ssm_b_im, ssm_c_re, ssm_c_im, ssm_d, ssm_log_step)
    ub = _norm_shuffle(xb, g_mix[0])
    u_small = _norm_rows(xs, g_mix[0])
    n_mc = N_META // S5_CHUNK
    um = u_small[meta_rows].reshape(n_mc, S5_CHUNK, S5_SETS, LANES)
    um = jnp.transpose(um, (2, 0, 1, 3)).reshape(S5_SETS, n_mc, 1, S5_CHUNK * LANES)
    um = jnp.broadcast_to(um, (S5_SETS, n_mc, N_BATCH, S5_CHUNK * LANES)).reshape(S5_SETS, N_META, S5_CHUNK * LANES)
    yb, ym, ys, hpr, hpi, hsr, hsi = _s5(
        ub, um, u_small, state_ssm_re.astype(F32).reshape(N_SAMPLE, -1),
        state_ssm_im.astype(F32).reshape(N_SAMPLE, -1), s5w)
    x1b = _glu_res(xb, yb, w_glu)
    ym_tok = ym.reshape(S5_SETS, n_mc, N_BATCH, S5_CHUNK, LANES)[:, :, 0]
    ym_tok = jnp.transpose(ym_tok, (1, 2, 0, 3)).reshape(N_META, D_MODEL)
    y_small = jnp.concatenate([ys, ym_tok, jnp.zeros((pad_rows, D_MODEL), BF16)], axis=0)
    x1s = _glu_res_rows(xs, y_small, w_glu)

    def ffn_layer(i, xb_in, xs_in, ab, a_small, wa):
        sc = state_conv[i].astype(F32)
        pad = ((0, SMALL_ROWS - N_SAMPLE), (0, 0))
        xs_out, h_small = _ffn_small(xs_in, a_small, wa, g_ffn[i], w_up[i], w_gate[i], conv_w[i], conv_b[i],
                                     w_down[i], jnp.pad(sc[:, 0], pad), jnp.pad(sc[:, 1], pad))
        init = jnp.concatenate([jnp.zeros((SUBLANES - 2, D_FF), F32),
                                h_small[META_ROW0 + N_META - 2:META_ROW0 + N_META]], axis=0)
        xb_out, hl = _ffn(xb_in, ab, wa, g_ffn[i], w_up[i], w_gate[i], conv_w[i], conv_b[i], w_down[i], init)
        tps = SEQ // FFN_TM
        conv_p = hl.reshape(N_BATCH, tps, SUBLANES, D_FF)[:, tps - 1, SUBLANES - 2:, :]
        conv_s = jnp.stack([sc[:, 1], h_small[:N_SAMPLE]], axis=1)
        return xb_out, xs_out, conv_p, conv_s

    x2b, x2s, conv_p0, conv_s0 = ffn_layer(0, x1b, x1s, None, None, None)

    w_qkv = attn_w_qkv.astype(BF16)
    wf = jnp.pad(attn_w_f.astype(BF16), ((0, 0), (0, LANES - N_HEADS)))
    bf = jnp.pad(attn_b_f.astype(F32), (0, LANES - N_HEADS)).reshape(1, LANES)
    qg = (jnp.tile(attn_q_g.astype(F32), N_HEADS) * (HEAD_DIM ** -0.5)).reshape(1, D_MODEL)
    kg = jnp.tile(attn_k_g.astype(F32), N_HEADS).reshape(1, D_MODEL)
    head_of_lane = jnp.arange(D_MODEL) // HEAD_DIM
    onehot = (head_of_lane[:, None] == jnp.arange(LANES)[None, :])
    seg = (onehot.astype(F32) / HEAD_DIM).astype(BF16)
    expand = onehot.T.astype(BF16)
    tri_body = (jnp.arange(FFN_TM)[:, None] >= jnp.arange(FFN_TM)[None, :]).astype(BF16)
    r = jnp.arange(SMALL_ROWS)
    tri_small = ((r[:, None] >= r[None, :]) & (r[None, :] >= META_ROW0) & (r[:, None] < META_ROW0 + N_META))
    tri_small = tri_small.astype(BF16)

    swp = expand.reshape(LANES // 2, 2, D_MODEL)[:, ::-1].reshape(LANES, D_MODEL)

    ks_f32, vs_f32, qts, kas, vts, lfs, cts = _qkv(x2s, g_mix[1], w_qkv, wf, bf, qg, kg, seg, expand, swp,
                                                   tri_small, jnp.zeros((1, LANES), F32), tm=SMALL_ROWS)
    c_meta_total = cts[:, META_ROW0 + N_META - 1].reshape(1, LANES)
    kb_f32, vb_f32, qtb, kab, vtb, lfb, ctb = _qkv(x2b, g_mix[1], w_qkv, wf, bf, qg, kg, seg, expand, swp,
                                                   tri_body, c_meta_total, tm=FFN_TM)

    nblk = SEQ // ATT_T
    ct = ctb[:N_HEADS].reshape(N_PAIRS, 2, N_BATCH, nblk, ATT_T)
    ct = jnp.transpose(ct, (2, 0, 3, 1, 4))
    ct = jnp.pad(ct, ((0, 0), (0, 0), (0, 0), (0, SUBLANES - 2), (0, 0)))
    cmt = cts[:N_HEADS, meta_rows].reshape(N_PAIRS, 2, N_META)
    cmt = jnp.pad(cmt, ((0, 0), (0, SUBLANES - 2), (0, 0)))
    qtm = qts[:, 0, :, meta_rows]
    kam = kas[:, meta_rows]
    vtm = vts[:, 0, :, meta_rows]
    o_meta = _meta_attention(qtm, kam, vtm, cmt)

    n_pool = cache_k.shape[0]
    cache_kt = jnp.transpose(cache_k, (0, 2, 3, 1)).reshape(n_pool, D_MODEL, PAGE)
    cache_vt = jnp.transpose(cache_v, (0, 2, 3, 1)).reshape(n_pool, D_MODEL, PAGE)
    cache_lft = jnp.transpose(cache_logf.astype(F32), (0, 2, 1))
    qt_pairs = qts[:, 0, :, :N_SAMPLE].reshape(N_PAIRS, 2, 2, HEAD_DIM, N_SAMPLE)
    qt_dec = jnp.stack([qt_pairs[:, 0, 0], qt_pairs[:, 1, 1]], axis=1).reshape(D_MODEL, N_SAMPLE)
    low = (jnp.arange(PAGE)[:, None] > jnp.arange(PAGE)[None, :]).astype(BF16)
    ob, acc_t, m_t, l_t = _attn(qtb, kab, vtb, ct, kam, vtm, page_table.astype(jnp.int32), qt_dec.astype(F32), low,
                                cache_kt, cache_vt, cache_lft)
    pad_heads = lambda x, v: jnp.pad(x.T, ((0, 0), (0, LANES - N_HEADS)), constant_values=v)
    o_dec = _decode_combine(acc_t.T, pad_heads(m_t, 0.0), pad_heads(l_t, 1.0), qt_dec.T, ks_f32[:N_SAMPLE],
                            vs_f32[:N_SAMPLE], lfs[:N_SAMPLE], (onehot.astype(BF16)), expand)
    o_small = jnp.concatenate([o_dec, o_meta, jnp.zeros((pad_rows, D_MODEL), BF16)], axis=0)

    w_o = attn_w_o.astype(BF16)
    x4b, x4s, conv_p1, conv_s1 = ffn_layer(1, x2b, x2s, ob, o_small, w_o)

    def with_meta(small, body, width):
        m = jnp.broadcast_to(small[meta_rows][None], (N_BATCH, N_META, width))
        return jnp.concatenate([m, body.reshape(N_BATCH, SEQ, width)], axis=1)

    k_p = with_meta(ks_f32, kb_f32, D_MODEL).reshape(N_BATCH, SEQ + N_META, N_HEADS, HEAD_DIM)
    v_p = with_meta(vs_f32, vb_f32, D_MODEL).reshape(N_BATCH, SEQ + N_META, N_HEADS, HEAD_DIM)
    lf_p = with_meta(lfs[:, :N_HEADS], lfb[:, :N_HEADS], N_HEADS)
    return (x4b.reshape(N_BATCH, SEQ, D_MODEL),
            x4s[:N_SAMPLE].reshape(N_SAMPLE, 1, D_MODEL),
            hpr.reshape(N_BATCH, N_GROUPS, STATE_P), hpi.reshape(N_BATCH, N_GROUPS, STATE_P),
            hsr.reshape(N_SAMPLE, N_GROUPS, STATE_P), hsi.reshape(N_SAMPLE, N_GROUPS, STATE_P),
            k_p, v_p, lf_p,
            ks_f32[:N_SAMPLE].reshape(N_SAMPLE, 1, N_HEADS, HEAD_DIM),
            vs_f32[:N_SAMPLE].reshape(N_SAMPLE, 1, N_HEADS, HEAD_DIM),
            lfs[:N_SAMPLE, :N_HEADS].reshape(N_SAMPLE, 1, N_HEADS),
            jnp.stack([conv_p0, conv_p1]), jnp.stack([conv_s0, conv_s1]))
```

```python
import functools

import jax
import jax.numpy as jnp
from jax import lax
from jax.experimental import pallas as pl
from jax.experimental.pallas import tpu as pltpu

F32 = jnp.float32
BF16 = jnp.bfloat16

D_MODEL = 1024
N_BATCH = 8
SEQ = 2048
N_SAMPLE = 128
N_META = 16
SMALL_ROWS = 256
META_ROW0 = 128
N_GROUPS = 64
STATE_P = 64
SSM_GROUP = 16
N_HEADS = 16
HEAD_DIM = 64
N_PAIRS = N_HEADS // 2
D_FF = 2816
CONV_W = 3
PAGE = 128
EPS = 1e-6
NEG_INF = -1e30

LANES = 128
SUBLANES = 8
S5_CHUNK = 8
S5_SETS = D_MODEL // LANES
SET_STATE = (LANES // SSM_GROUP) * STATE_P
N_CHUNKS = SEQ // S5_CHUNK
VMEM_LIMIT = 56 * 1024 * 1024


def _cparams(sem):
    return pltpu.CompilerParams(dimension_semantics=sem, vmem_limit_bytes=VMEM_LIMIT)


def _const_spec(shape):
    nd = len(shape)
    return pl.BlockSpec(shape, lambda *_: (0,) * nd, pipeline_mode=pl.Buffered(1))


def _full_spec(shape):
    nd = len(shape)
    return pl.BlockSpec(shape, lambda *_: (0,) * nd)


def _dot(a, b):
    return jnp.dot(a, b, preferred_element_type=F32)


def _dot_f32(a, b):
    return jnp.dot(a, b, preferred_element_type=F32, precision=lax.Precision.HIGHEST)


def _split2(x):
    hi = x.astype(BF16)
    lo = (x - hi.astype(F32)).astype(BF16)
    return hi, lo


def _split3(x):
    hi = x.astype(BF16)
    r = x - hi.astype(F32)
    mid = r.astype(BF16)
    lo = (r - mid.astype(F32)).astype(BF16)
    return hi, mid, lo


def _dot_split(parts, w):
    acc = _dot(parts[0], w)
    for p in parts[1:]:
        acc = acc + _dot(p, w)
    return acc


def _rmsnorm(x, g):
    ms = jnp.mean(x * x, axis=-1, keepdims=True)
    return x * lax.rsqrt(ms + EPS) * g


def _gelu_tanh(x):
    c = 0.7978845608028654
    return 0.5 * x * (1.0 + jnp.tanh(c * (x + 0.044715 * (x * x * x))))


def _sigmoid(x):
    return 1.0 / (1.0 + jnp.exp(-x))


def _norm_shuffle_kernel(x_ref, g_ref, o_ref, scr):
    rb = o_ref.shape[1]
    u = _rmsnorm(x_ref[...], g_ref[...])
    for j in range(S5_SETS):
        scr[j] = u[:, j * LANES:(j + 1) * LANES]
    for j in range(S5_SETS):
        for s in range(S5_CHUNK):
            o_ref[j, :, s * LANES:(s + 1) * LANES] = scr[j, pl.ds(s, rb, stride=S5_CHUNK), :].astype(BF16)


def _norm_shuffle(x, g):
    rows = x.shape[0] // S5_CHUNK
    rb = 128
    return pl.pallas_call(
        _norm_shuffle_kernel,
        grid=(rows // rb,),
        in_specs=[pl.BlockSpec((rb * S5_CHUNK, D_MODEL), lambda i: (i, 0)),
                  _const_spec((1, D_MODEL))],
        out_specs=pl.BlockSpec((S5_SETS, rb, S5_CHUNK * LANES), lambda i: (0, i, 0)),
        out_shape=jax.ShapeDtypeStruct((S5_SETS, rows, S5_CHUNK * LANES), BF16),
        scratch_shapes=[pltpu.VMEM((S5_SETS, rb * S5_CHUNK, LANES), F32)],
        compiler_params=_cparams(("parallel",)),
        name="norm_shuffle",
    )(x, g)


def _norm_rows_kernel(x_ref, g_ref, o_ref):
    o_ref[...] = _rmsnorm(x_ref[...], g_ref[...]).astype(BF16)


def _norm_rows(x, g):
    rows = x.shape[0]
    return pl.pallas_call(
        _norm_rows_kernel,
        grid=(1,),
        in_specs=[_full_spec((rows, D_MODEL)), _full_spec((1, D_MODEL))],
        out_specs=_full_spec((rows, D_MODEL)),
        out_shape=jax.ShapeDtypeStruct((rows, D_MODEL), BF16),
        name="norm_rows",
    )(x, g)


S5_ROW_BLOCK = 512
N_SLABS = SET_STATE // LANES
SCAN_PITCH = N_CHUNKS + SUBLANES


def _s5_kernel(ub_ref, um_ref, us_ref, h0r_ref, h0i_ref, wt_ref, wb_ref, wc_ref, a8_ref, dt_ref,
               wb1_ref, wc1_ref, a1_ref, d1_ref,
               yb_ref, ym_ref, ys_ref, hpr_ref, hpi_ref, hsr_ref, hsi_ref, xr_scr, xi_scr):
    rows = ub_ref.shape[1]
    wb = wb_ref[0]
    wt = wt_ref[0]
    wc = wc_ref[0]
    dt = dt_ref[0]

    seq_per_block = S5_ROW_BLOCK // N_CHUNKS

    def scratch_rows(n):
        return slice(n * SCAN_PITCH, n * SCAN_PITCH + N_CHUNKS)

    for r in range(rows // S5_ROW_BLOCK):
        x = _dot(ub_ref[0, r * S5_ROW_BLOCK:(r + 1) * S5_ROW_BLOCK, :], wb)
        for q in range(seq_per_block):
            xs = x[q * N_CHUNKS:(q + 1) * N_CHUNKS]
            ss = scratch_rows(r * seq_per_block + q)
            for k in range(N_SLABS):
                xr_scr[k, ss, :] = xs[:, k * LANES:(k + 1) * LANES]
                xi_scr[k, ss, :] = xs[:, SET_STATE + k * LANES:SET_STATE + (k + 1) * LANES]
    xm = _dot(um_ref[0], wb)

    a8r = jnp.broadcast_to(a8_ref[0, 0:1, :], (SUBLANES, SET_STATE))
    a8i = jnp.broadcast_to(a8_ref[0, 1:2, :], (SUBLANES, SET_STATE))

    hr = jnp.zeros((SUBLANES, SET_STATE), F32)
    hi = jnp.zeros((SUBLANES, SET_STATE), F32)
    hm_prev = []
    for cm in range(N_META // S5_CHUNK):
        hm_prev.append(jnp.concatenate([hr, hi], axis=1))
        xr = xm[cm * SUBLANES:(cm + 1) * SUBLANES, :SET_STATE]
        xi = xm[cm * SUBLANES:(cm + 1) * SUBLANES, SET_STATE:]
        hr, hi = a8r * hr - a8i * hi + xr, a8r * hi + a8i * hr + xi

    ar = [a8r[:, k * LANES:(k + 1) * LANES] for k in range(N_SLABS)]
    ai = [a8i[:, k * LANES:(k + 1) * LANES] for k in range(N_SLABS)]

    def step(c, carry):
        hrs, his = carry
        nr, ni = [], []
        for k in range(N_SLABS):
            idx = (k, pl.ds(c, N_BATCH, stride=SCAN_PITCH), slice(None))
            xr = xr_scr[idx]
            xi = xi_scr[idx]
            xr_scr[idx] = hrs[k]
            xi_scr[idx] = his[k]
            nr.append(ar[k] * hrs[k] - ai[k] * his[k] + xr)
            ni.append(ar[k] * his[k] + ai[k] * hrs[k] + xi)
        return tuple(nr), tuple(ni)

    init = (tuple(hr[:, k * LANES:(k + 1) * LANES] for k in range(N_SLABS)),
            tuple(hi[:, k * LANES:(k + 1) * LANES] for k in range(N_SLABS)))
    hrs, his = lax.fori_loop(0, N_CHUNKS, step, init)
    hpr_ref[...] = jnp.concatenate(hrs, axis=1)
    hpi_ref[...] = jnp.concatenate(his, axis=1)

    def emit(u, hprev):
        y = _dot(u, wt) + _dot(hprev.astype(BF16), wc) + dt * u.astype(F32)
        return _gelu_tanh(y).astype(BF16)

    for r in range(rows // S5_ROW_BLOCK):
        rs = slice(r * S5_ROW_BLOCK, (r + 1) * S5_ROW_BLOCK)
        hprev = jnp.concatenate(
            [jnp.concatenate([xr_scr[k, scratch_rows(r * seq_per_block + q), :] for k in range(N_SLABS)]
                             + [xi_scr[k, scratch_rows(r * seq_per_block + q), :] for k in range(N_SLABS)], axis=1)
             for q in range(seq_per_block)], axis=0)
        yb_ref[0, rs, :] = emit(ub_ref[0, rs, :], hprev)
    ym_ref[0] = emit(um_ref[0], jnp.concatenate(hm_prev, axis=0))

    us = us_ref[...]
    xs = _dot(us, wb1_ref[0])
    a1r = a1_ref[0, 0:1, :]
    a1i = a1_ref[0, 1:2, :]
    h0r = h0r_ref[...]
    h0i = h0i_ref[...]
    hsr = a1r * h0r - a1i * h0i + xs[:, :SET_STATE]
    hsi = a1r * h0i + a1i * h0r + xs[:, SET_STATE:]
    hsr_ref[...] = hsr
    hsi_ref[...] = hsi
    hs = jnp.concatenate([hsr, hsi], axis=1).astype(BF16)
    ys = _dot(hs, wc1_ref[0]) + d1_ref[0] * us.astype(F32)
    ys_ref[...] = _gelu_tanh(ys).astype(BF16)


def _set_spec(shape):
    return pl.BlockSpec((1,) + shape, lambda j: (j,) + (0,) * len(shape))


def _s5(ub, um, u_small, h0r, h0i, w):
    rows = ub.shape[1]
    cw = S5_CHUNK * LANES
    out_shapes = (
        jax.ShapeDtypeStruct((S5_SETS, rows, cw), BF16),
        jax.ShapeDtypeStruct((S5_SETS, N_META, cw), BF16),
        jax.ShapeDtypeStruct((N_SAMPLE, D_MODEL), BF16),
        jax.ShapeDtypeStruct((N_BATCH, S5_SETS * SET_STATE), F32),
        jax.ShapeDtypeStruct((N_BATCH, S5_SETS * SET_STATE), F32),
        jax.ShapeDtypeStruct((N_SAMPLE, S5_SETS * SET_STATE), F32),
        jax.ShapeDtypeStruct((N_SAMPLE, S5_SETS * SET_STATE), F32),
    )
    return pl.pallas_call(
        _s5_kernel,
        grid=(S5_SETS,),
        in_specs=[
            _set_spec((rows, cw)),
            _set_spec((N_META, cw)),
            pl.BlockSpec((N_SAMPLE, LANES), lambda j: (0, j)),
            pl.BlockSpec((N_SAMPLE, SET_STATE), lambda j: (0, j)),
            pl.BlockSpec((N_SAMPLE, SET_STATE), lambda j: (0, j)),
            _set_spec((cw, cw)),
            _set_spec((cw, 2 * SET_STATE)),
            _set_spec((2 * SET_STATE, cw)),
            _set_spec((2, SET_STATE)),
            _set_spec((1, cw)),
            _set_spec((LANES, 2 * SET_STATE)),
            _set_spec((2 * SET_STATE, LANES)),
            _set_spec((2, SET_STATE)),
            _set_spec((1, LANES)),
        ],
        out_specs=(
            _set_spec((rows, cw)),
            _set_spec((N_META, cw)),
            pl.BlockSpec((N_SAMPLE, LANES), lambda j: (0, j)),
            pl.BlockSpec((N_BATCH, SET_STATE), lambda j: (0, j)),
            pl.BlockSpec((N_BATCH, SET_STATE), lambda j: (0, j)),
            pl.BlockSpec((N_SAMPLE, SET_STATE), lambda j: (0, j)),
            pl.BlockSpec((N_SAMPLE, SET_STATE), lambda j: (0, j)),
        ),
        out_shape=out_shapes,
        scratch_shapes=[pltpu.VMEM((N_SLABS, N_BATCH * SCAN_PITCH, LANES), F32),
                        pltpu.VMEM((N_SLABS, N_BATCH * SCAN_PITCH, LANES), F32)],
        compiler_params=_cparams(("parallel",)),
        name="s5_mixer",
    )(ub, um, u_small, h0r, h0i, w["wt"], w["wb"], w["wc"], w["a8"], w["dt"],
      w["wb1"], w["wc1"], w["a1"], w["d1"])


def _s5_discretize(a_re, a_im, log_step):
    dt = jnp.exp(log_step)
    mag = jnp.exp(a_re * dt)
    ab_re = mag * jnp.cos(a_im * dt)
    ab_im = mag * jnp.sin(a_im * dt)
    den = a_re * a_re + a_im * a_im
    n_re = ab_re - 1.0
    n_im = ab_im
    f_re = (n_re * a_re + n_im * a_im) / den
    f_im = (n_im * a_re - n_re * a_im) / den
    return ab_re, ab_im, f_re, f_im


def _s5_weights_kernel(arow_ref, acr_ref, aci_ref, acl_ref, br_ref, bi_ref, cr_ref, ci_ref,
                       wt_ref, wb_ref, wc_ref, a8_ref, wb1_ref, wc1_ref, a1_ref):
    arow = arow_ref[0]
    abr, abi, fr, fi = _s5_discretize(arow[0:1], arow[1:2], arow[2:3])
    abr_c, abi_c, _, _ = _s5_discretize(acr_ref[0], aci_ref[0], acl_ref[0])
    br = br_ref[0]
    bi = bi_ref[0]
    b0r = fr * br - fi * bi
    b0i = fr * bi + fi * br
    cr = cr_ref[0]
    ci = ci_ref[0]
    wt_ref[0] = jnp.zeros(wt_ref.shape[1:], BF16)
    pr = jnp.ones_like(abr)
    pi = jnp.zeros_like(abr)
    pcr, pci = abr_c, abi_c
    for k in range(S5_CHUNK):
        bkr = pr * b0r - pi * b0i
        bki = pr * b0i + pi * b0r
        s = S5_CHUNK - 1 - k
        wb_ref[0, s * LANES:(s + 1) * LANES, :SET_STATE] = bkr.astype(BF16)
        wb_ref[0, s * LANES:(s + 1) * LANES, SET_STATE:] = bki.astype(BF16)
        kern = (_dot_f32(bkr, cr) - _dot_f32(bki, ci)).astype(BF16)
        for s2 in range(S5_CHUNK - k):
            t2 = s2 + k
            wt_ref[0, s2 * LANES:(s2 + 1) * LANES, t2 * LANES:(t2 + 1) * LANES] = kern
        wc_ref[0, :SET_STATE, k * LANES:(k + 1) * LANES] = (pcr * cr - pci * ci).astype(BF16)
        wc_ref[0, SET_STATE:, k * LANES:(k + 1) * LANES] = (-(pcr * ci + pci * cr)).astype(BF16)
        pr, pi = pr * abr - pi * abi, pr * abi + pi * abr
        pcr, pci = pcr * abr_c - pci * abi_c, pcr * abi_c + pci * abr_c
    a8_ref[0, 0:1, :] = pr
    a8_ref[0, 1:2, :] = pi
    a1_ref[0, 0:1, :] = abr
    a1_ref[0, 1:2, :] = abi
    wb1_ref[0, :, :SET_STATE] = b0r.astype(BF16)
    wb1_ref[0, :, SET_STATE:] = b0i.astype(BF16)
    wc1_ref[0, :SET_STATE, :] = cr.astype(BF16)
    wc1_ref[0, SET_STATE:, :] = (-ci).astype(BF16)


def _s5_weights(a_re, a_im, b_re, b_im, c_re, c_im, d_skip, log_step):
    gs = LANES // SSM_GROUP
    cw = S5_CHUNK * LANES
    eye = jnp.eye(gs, dtype=F32)
    per_state = lambda v: v.astype(F32).reshape(S5_SETS, SET_STATE)
    a_re_s, a_im_s = per_state(a_re), per_state(a_im)
    ls_s = per_state(jnp.broadcast_to(log_step[:, None], (N_GROUPS, STATE_P)))
    arow = jnp.stack([a_re_s, a_im_s, ls_s] + [jnp.zeros_like(ls_s)] * (SUBLANES - 3), axis=1)
    col = lambda v: jnp.broadcast_to(v[:, :, None], (S5_SETS, SET_STATE, LANES))

    def blockdiag_b(b):
        b = b.astype(F32).reshape(S5_SETS, gs, STATE_P, SSM_GROUP)
        return jnp.einsum('jgpi,gh->jgihp', b, eye).reshape(S5_SETS, LANES, SET_STATE)

    def blockdiag_c(c):
        c = c.astype(F32).reshape(S5_SETS, gs, SSM_GROUP, STATE_P)
        return jnp.einsum('jgop,gh->jgpho', c, eye).reshape(S5_SETS, SET_STATE, LANES)

    out_shapes = (
        jax.ShapeDtypeStruct((S5_SETS, cw, cw), BF16),
        jax.ShapeDtypeStruct((S5_SETS, cw, 2 * SET_STATE), BF16),
        jax.ShapeDtypeStruct((S5_SETS, 2 * SET_STATE, cw), BF16),
        jax.ShapeDtypeStruct((S5_SETS, 2, SET_STATE), F32),
        jax.ShapeDtypeStruct((S5_SETS, LANES, 2 * SET_STATE), BF16),
        jax.ShapeDtypeStruct((S5_SETS, 2 * SET_STATE, LANES), BF16),
        jax.ShapeDtypeStruct((S5_SETS, 2, SET_STATE), F32),
    )
    wt, wb, wc, a8, wb1, wc1, a1 = pl.pallas_call(
        _s5_weights_kernel,
        grid=(S5_SETS,),
        in_specs=[_set_spec((SUBLANES, SET_STATE)), _set_spec((SET_STATE, LANES)), _set_spec((SET_STATE, LANES)),
                  _set_spec((SET_STATE, LANES)), _set_spec((LANES, SET_STATE)), _set_spec((LANES, SET_STATE)),
                  _set_spec((SET_STATE, LANES)), _set_spec((SET_STATE, LANES))],
        out_specs=tuple(_set_spec(s.shape[1:]) for s in out_shapes),
        out_shape=out_shapes,
        compiler_params=_cparams(("parallel",)),
        name="s5_weights",
    )(arow, col(a_re_s), col(a_im_s), col(ls_s), blockdiag_b(b_re), blockdiag_b(b_im),
      blockdiag_c(c_re), blockdiag_c(c_im))
    d1 = d_skip.astype(F32).reshape(S5_SETS, 1, LANES)
    return {"wt": wt, "wb": wb, "wc": wc, "a8": a8, "dt": jnp.tile(d1, (1, 1, S5_CHUNK)),
            "wb1": wb1, "wc1": wc1, "a1": a1, "d1": d1}


def _glu(y, w):
    z = _dot(y, w)
    return z[:, :D_MODEL] * _sigmoid(z[:, D_MODEL:])


def _glu_res_kernel(x_ref, y_ref, w_ref, o_ref, scr):
    rb = y_ref.shape[1]
    w = w_ref[...]
    for s in range(S5_CHUNK):
        y = jnp.concatenate([y_ref[j, :, s * LANES:(s + 1) * LANES] for j in range(S5_SETS)], axis=1)
        o = _glu(y, w)
        for j in range(S5_SETS):
            scr[j, pl.ds(s, rb, stride=S5_CHUNK), :] = o[:, j * LANES:(j + 1) * LANES]
    for j in range(S5_SETS):
        cs = slice(j * LANES, (j + 1) * LANES)
        o_ref[:, cs] = x_ref[:, cs] + scr[j]


def _glu_res(x, y, w_glu):
    rows = y.shape[1]
    rb = 128
    tok_spec = pl.BlockSpec((rb * S5_CHUNK, D_MODEL), lambda i: (i, 0))
    return pl.pallas_call(
        _glu_res_kernel,
        grid=(rows // rb,),
        in_specs=[tok_spec, pl.BlockSpec((S5_SETS, rb, S5_CHUNK * LANES), lambda i: (0, i, 0)),
                  _const_spec((D_MODEL, 2 * D_MODEL))],
        out_specs=tok_spec,
        out_shape=jax.ShapeDtypeStruct(x.shape, F32),
        scratch_shapes=[pltpu.VMEM((S5_SETS, rb * S5_CHUNK, LANES), F32)],
        compiler_params=_cparams(("parallel",)),
        name="glu_res",
    )(x, y, w_glu)


def _glu_res_rows_kernel(x_ref, y_ref, w_ref, o_ref):
    o_ref[...] = x_ref[...] + _glu(y_ref[...], w_ref[...])


def _glu_res_rows(x, y, w_glu):
    rows = x.shape[0]
    return pl.pallas_call(
        _glu_res_rows_kernel,
        grid=(1,),
        in_specs=[_full_spec((rows, D_MODEL)), _full_spec((rows, D_MODEL)), _full_spec((D_MODEL, 2 * D_MODEL))],
        out_specs=_full_spec((rows, D_MODEL)),
        out_shape=jax.ShapeDtypeStruct((rows, D_MODEL), F32),
        name="glu_res_rows",
    )(x, y, w_glu)


FFN_TM = 512
FFN_FC = D_FF // 2


def _conv_act(h, gt, p1, p2, cw, cb):
    hc = cb + cw[0:1, :] * p2
    hc = hc + cw[1:2, :] * p1
    hc = hc + cw[2:3, :] * h
    return (hc * _sigmoid(hc) * gt).astype(BF16)


def _ffn_kernel(*refs, proj, tiles_per_seq):
    if proj:
        (x_ref, a_ref, wa_ref, g_ref, wu_ref, wg_ref, cw_ref, cb_ref, wd_ref, init_ref,
         o_ref, hl_ref, carry_scr, act_scr) = refs
    else:
        (x_ref, g_ref, wu_ref, wg_ref, cw_ref, cb_ref, wd_ref, init_ref,
         o_ref, hl_ref, carry_scr, act_scr) = refs
    i = pl.program_id(0)

    @pl.when(i % tiles_per_seq == 0)
    def _():
        carry_scr[...] = init_ref[...]

    x = x_ref[...]
    if proj:
        a = jnp.concatenate([a_ref[p] for p in range(N_PAIRS)], axis=1)
        x = x + _dot(a, wa_ref[...])
    u = _rmsnorm(x, g_ref[...]).astype(BF16)
    tm = x.shape[0]
    row8 = lax.broadcasted_iota(jnp.int32, (SUBLANES, 1), 0)
    out = x
    for c in range(D_FF // FFN_FC):
        cs = slice(c * FFN_FC, (c + 1) * FFN_FC)
        h = _dot(u, wu_ref[:, cs])
        gt = _dot(u, wg_ref[:, cs])
        cw = cw_ref[:, cs]
        cb = cb_ref[:, cs]
        carry = carry_scr[:, cs]
        p1 = pltpu.roll(h, 1, axis=0)
        p2 = pltpu.roll(h, 2, axis=0)
        act_scr[...] = _conv_act(h, gt, p1, p2, cw, cb)
        c1 = pltpu.roll(carry, 1, axis=0)
        c2 = pltpu.roll(carry, 2, axis=0)
        p1h = jnp.where(row8 < 1, c1, p1[:SUBLANES])
        p2h = jnp.where(row8 < 2, c2, p2[:SUBLANES])
        act_scr[0:SUBLANES, :] = _conv_act(h[:SUBLANES], gt[:SUBLANES], p1h, p2h, cw, cb)
        last = h[tm - SUBLANES:, :]
        carry_scr[:, cs] = last
        hl_ref[0, :, cs] = last
        out = out + _dot(act_scr[...], wd_ref[cs, :])
    o_ref[...] = out


def _ffn(x, a, wa, g, wu, wg, cw, cb, wd, init):
    rows = x.shape[0]
    tm = FFN_TM
    proj = a is not None
    row_spec = pl.BlockSpec((tm, D_MODEL), lambda i: (i, 0))
    in_specs = [row_spec]
    args = [x]
    if proj:
        in_specs += [pl.BlockSpec((N_PAIRS, tm, LANES), lambda i: (0, i, 0)), _const_spec((D_MODEL, D_MODEL))]
        args += [a, wa]
    in_specs += [_const_spec((1, D_MODEL)), _const_spec((D_MODEL, D_FF)), _const_spec((D_MODEL, D_FF)),
                 _const_spec((SUBLANES, D_FF)), _const_spec((1, D_FF)), _const_spec((D_FF, D_MODEL)),
                 _const_spec((SUBLANES, D_FF))]
    args += [g, wu, wg, cw, cb, wd, init]
    return pl.pallas_call(
        functools.partial(_ffn_kernel, proj=proj, tiles_per_seq=SEQ // tm),
        grid=(rows // tm,),
        in_specs=in_specs,
        out_specs=(row_spec, pl.BlockSpec((1, SUBLANES, D_FF), lambda i: (i, 0, 0))),
        out_shape=(jax.ShapeDtypeStruct((rows, D_MODEL), F32),
                   jax.ShapeDtypeStruct((rows // tm, SUBLANES, D_FF), F32)),
        scratch_shapes=[pltpu.VMEM((SUBLANES, D_FF), F32), pltpu.VMEM((tm, FFN_FC), BF16)],
        compiler_params=_cparams(("arbitrary",)),
        name="ffn_proj" if proj else "ffn",
    )(*args)


def _ffn_small_kernel(*refs, proj):
    if proj:
        (x_ref, a_ref, wa_ref, g_ref, wu_ref, wg_ref, cw_ref, cb_ref, wd_ref, b0_ref, b1_ref,
         o_ref, h_ref) = refs
    else:
        (x_ref, g_ref, wu_ref, wg_ref, cw_ref, cb_ref, wd_ref, b0_ref, b1_ref, o_ref, h_ref) = refs
    x = x_ref[...]
    if proj:
        x = x + _dot(a_ref[...], wa_ref[...])
    u = _rmsnorm(x, g_ref[...]).astype(BF16)
    row = lax.broadcasted_iota(jnp.int32, (SMALL_ROWS, 1), 0)
    is_sample = row < N_SAMPLE
    meta_end = META_ROW0 + N_META
    out = x
    for c in range(D_FF // FFN_FC):
        cs = slice(c * FFN_FC, (c + 1) * FFN_FC)
        h = _dot(u, wu_ref[:, cs])
        gt = _dot(u, wg_ref[:, cs])
        p1 = jnp.where(is_sample, b1_ref[:, cs],
                       jnp.where((row >= META_ROW0 + 1) & (row < meta_end), pltpu.roll(h, 1, axis=0), 0.0))
        p2 = jnp.where(is_sample, b0_ref[:, cs],
                       jnp.where((row >= META_ROW0 + 2) & (row < meta_end), pltpu.roll(h, 2, axis=0), 0.0))
        act = _conv_act(h, gt, p1, p2, cw_ref[:, cs], cb_ref[:, cs])
        h_ref[:, cs] = h
        out = out + _dot(act, wd_ref[cs, :])
    o_ref[...] = out


def _ffn_small(x, a, wa, g, wu, wg, cw, cb, wd, b0, b1):
    proj = a is not None
    in_specs = [_full_spec((SMALL_ROWS, D_MODEL))]
    args = [x]
    if proj:
        in_specs += [_full_spec((SMALL_ROWS, D_MODEL)), _full_spec((D_MODEL, D_MODEL))]
        args += [a, wa]
    in_specs += [_full_spec((1, D_MODEL)), _full_spec((D_MODEL, D_FF)), _full_spec((D_MODEL, D_FF)),
                 _full_spec((SUBLANES, D_FF)), _full_spec((1, D_FF)), _full_spec((D_FF, D_MODEL)),
                 _full_spec((SMALL_ROWS, D_FF)), _full_spec((SMALL_ROWS, D_FF))]
    args += [g, wu, wg, cw, cb, wd, b0, b1]
    return pl.pallas_call(
        functools.partial(_ffn_small_kernel, proj=proj),
        grid=(1,),
        in_specs=in_specs,
        out_specs=(_full_spec((SMALL_ROWS, D_MODEL)), _full_spec((SMALL_ROWS, D_FF))),
        out_shape=(jax.ShapeDtypeStruct((SMALL_ROWS, D_MODEL), F32),
                   jax.ShapeDtypeStruct((SMALL_ROWS, D_FF), F32)),
        compiler_params=_cparams(("arbitrary",)),
        name="ffn_small_proj" if proj else "ffn_small",
    )(*args)


ATT_T = 256
ATT_HPS = 4
DEC_PPS = 8


def _qkv_kernel(x_ref, g_ref, w_ref, wf_ref, bf_ref, qg_ref, kg_ref, seg_ref, exp_ref, swp_ref, tri_ref, cinit_ref,
                kt_ref, vtf_ref, qt_ref, ka_ref, vt_ref, lf_ref, ct_ref, carry_scr, *, tiles_per_seq):
    i = pl.program_id(0)

    @pl.when(i % tiles_per_seq == 0)
    def _():
        carry_scr[...] = jnp.broadcast_to(cinit_ref[...], carry_scr.shape)

    u = _rmsnorm(x_ref[...], g_ref[...]).astype(BF16)
    qkv = _dot(u, w_ref[...])
    seg = seg_ref[...]
    expand = exp_ref[...]

    def head_norm(t, gain):
        ms = _dot_split(_split2(t * t), seg)
        rinv = lax.rsqrt(ms + EPS)
        return t * _dot_split(_split2(rinv), expand) * gain

    q = head_norm(qkv[:, :D_MODEL], qg_ref[...])
    k = head_norm(qkv[:, D_MODEL:2 * D_MODEL], kg_ref[...])
    v = qkv[:, 2 * D_MODEL:]
    tm = k.shape[0]

    z = _dot(u, wf_ref[...]) + bf_ref[...]
    lf = jnp.minimum(z, 0.0) - jnp.log(1.0 + jnp.exp(-jnp.abs(z)))
    lf_ref[...] = lf
    tri = tri_ref[...]
    c = carry_scr[SUBLANES - 1:SUBLANES, :]
    for part in _split3(lf):
        c = c + _dot(tri, part)
    carry_scr[...] = c[tm - SUBLANES:, :]
    ct_ref[...] = c.T

    swp = swp_ref[...]
    parts = [_dot(part, swp) for part in _split3(-c)]
    l64 = lax.broadcasted_iota(jnp.int32, (1, D_MODEL), 1) % HEAD_DIM
    k_bias = jnp.where(l64 == 0, parts[0], jnp.where(l64 == 1, parts[1], jnp.where(l64 == 2, parts[2], 0.0)))
    lane = lax.broadcasted_iota(jnp.int32, (1, LANES), 1)
    row = lax.broadcasted_iota(jnp.int32, (LANES, 1), 0)
    q_ones = (row % HEAD_DIM < len(parts)).astype(F32)
    for p in range(N_PAIRS):
        ps = slice(p * LANES, (p + 1) * LANES)
        for e in range(2):
            own = (lane >= HEAD_DIM) if e else (lane < HEAD_DIM)
            ka_ref[2 * p + e] = jnp.where(own, k[:, ps], k_bias[:, ps]).astype(BF16)
        for b in range(tm // ATT_T):
            bs = slice(b * ATT_T, (b + 1) * ATT_T)
            q_t = q[bs, ps].T
            k_t = k[bs, ps].T
            v_t = v[bs, ps].T
            kt_ref[0, ps, bs] = k_t
            vtf_ref[0, ps, bs] = v_t
            for e in range(2):
                own = (row >= HEAD_DIM) if e else (row < HEAD_DIM)
                qt_ref[2 * p + e, b] = jnp.where(own, q_t, q_ones).astype(BF16)
                vt_ref[2 * p + e, b] = jnp.where(own, v_t, 1.0).astype(BF16)


def _qkv(x, g, w_qkv, wf, bf, qg, kg, seg, expand, swp, tri, cinit, *, tm):
    rows = x.shape[0]
    seq_rows = min(SEQ, rows)
    tps = seq_rows // tm
    row_spec = pl.BlockSpec((tm, D_MODEL), lambda i: (i, 0))
    kv_spec = pl.BlockSpec((1, D_MODEL, tm), lambda i: (i // tps, 0, i % tps))
    kv_shape = jax.ShapeDtypeStruct((rows // seq_rows, D_MODEL, seq_rows), F32)
    t_spec = pl.BlockSpec((N_HEADS, tm // ATT_T, LANES, ATT_T), lambda i: (0, i, 0, 0))
    t_shape = jax.ShapeDtypeStruct((N_HEADS, rows // ATT_T, LANES, ATT_T), BF16)
    return pl.pallas_call(
        functools.partial(_qkv_kernel, tiles_per_seq=tps),
        grid=(rows // tm,),
        in_specs=[row_spec, _const_spec((1, D_MODEL)), _const_spec((D_MODEL, 3 * D_MODEL)),
                  _const_spec((D_MODEL, LANES)), _const_spec((1, LANES)), _const_spec((1, D_MODEL)),
                  _const_spec((1, D_MODEL)), _const_spec((D_MODEL, LANES)), _const_spec((LANES, D_MODEL)),
                  _const_spec((LANES, D_MODEL)), _const_spec((tm, tm)), _const_spec((1, LANES))],
        out_specs=(kv_spec, kv_spec, t_spec,
                   pl.BlockSpec((N_HEADS, tm, LANES), lambda i: (0, i, 0)), t_spec,
                   pl.BlockSpec((tm, LANES), lambda i: (i, 0)),
                   pl.BlockSpec((LANES, tm), lambda i: (0, i))),
        out_shape=(kv_shape, kv_shape,
                   t_shape, jax.ShapeDtypeStruct((N_HEADS, rows, LANES), BF16), t_shape,
                   jax.ShapeDtypeStruct((rows, LANES), F32), jax.ShapeDtypeStruct((LANES, rows), F32)),
        scratch_shapes=[pltpu.VMEM((SUBLANES, LANES), F32)],
        compiler_params=_cparams(("arbitrary",)),
        name="qkv_proj",
    )(x, g, w_qkv, wf, bf, qg, kg, seg, expand, swp, tri, cinit)


def _softmax_step(s0, vt, cq, m, acc):
    m_new = jnp.maximum(m, jnp.max(s0, axis=0, keepdims=True) + cq)
    p = jnp.exp(s0 - (m_new - cq))
    acc = jnp.exp(m - m_new) * acc + _dot(vt, p.astype(BF16))
    return m_new, acc


def _attn_finish(acc_even, acc_odd):
    o_t = jnp.concatenate([acc_even[:HEAD_DIM] / acc_even[HEAD_DIM:],
                           acc_odd[HEAD_DIM:] / acc_odd[:HEAD_DIM]], axis=0)
    return o_t.T.astype(BF16)


def _page_copies(pt_ref, cache_refs, bufs, sem, it, slot):
    n_pages = pt_ref.shape[1]
    n_sub = n_pages // DEC_PPS
    b = it // n_sub
    t = it % n_sub
    copies = []
    for i in range(DEC_PPS):
        page = pt_ref[b, n_pages - 1 - (t * DEC_PPS + i)]
        for kind, (src, dst) in enumerate(zip(cache_refs, bufs)):
            copies.append(pltpu.make_async_copy(src.at[page], dst.at[slot, i], sem.at[slot, kind]))
    return copies


def _decode_substep(it, n_sub, qd_ref, low_ref, k_pages, v_pages, lf_pages, do_ref, dm_ref, dl_ref,
                    qc_scr, m_scr, l_scr, tail_scr, acc_scr):
    b = it // n_sub
    t = it % n_sub
    mine = lax.broadcasted_iota(jnp.int32, (1, LANES), 1) == b

    @pl.when(it == 0)
    def _():
        do_ref[...] = jnp.zeros_like(do_ref)
        dm_ref[...] = jnp.zeros_like(dm_ref)
        dl_ref[...] = jnp.zeros_like(dl_ref)

    @pl.when(t == 0)
    def _():
        qcol = jnp.sum(jnp.where(mine, qd_ref[...], 0.0), axis=1, keepdims=True)
        qc_scr[...] = jnp.broadcast_to(qcol, (D_MODEL, LANES))
        m_scr[...] = jnp.full(m_scr.shape, NEG_INF, F32)
        l_scr[...] = jnp.zeros_like(l_scr)
        tail_scr[...] = jnp.zeros_like(tail_scr)
        acc_scr[...] = jnp.zeros_like(acc_scr)

    low = low_ref[...]
    tail = tail_scr[...]
    scores = []
    tiles = [[] for _ in range(DEC_PPS)]
    for h in range(N_HEADS):
        tile = [None] * DEC_PPS
        for r in range(h * HEAD_DIM, (h + 1) * HEAD_DIM, SUBLANES):
            rows = slice(r, r + SUBLANES)
            q_rows = qc_scr[rows, :]
            for i in range(DEC_PPS):
                term = k_pages(i, rows) * q_rows
                tile[i] = term if tile[i] is None else tile[i] + term
        for i in range(DEC_PPS):
            tiles[i].append(tile[i])
    for i in range(DEC_PPS):
        s = jnp.sum(jnp.concatenate(tiles[i], axis=0).reshape(N_HEADS, SUBLANES, LANES), axis=1)
        lf = lf_pages(i)
        s = s + tail
        for part in _split3(lf):
            s = s + _dot(part, low)
        scores.append(s)
        tail = tail + jnp.sum(lf, axis=1, keepdims=True)
    tail_scr[...] = tail
    m_old = m_scr[...]
    s_max = scores[0]
    for s in scores[1:]:
        s_max = jnp.maximum(s_max, s)
    m_new = jnp.maximum(m_old, jnp.max(s_max, axis=1, keepdims=True))
    alpha = jnp.exp(m_old - m_new)
    probs = [jnp.exp(s - m_new) for s in scores]
    p_sum = probs[0]
    for p in probs[1:]:
        p_sum = p_sum + p
    l_scr[...] = alpha * l_scr[...] + jnp.sum(p_sum, axis=1, keepdims=True)
    m_scr[...] = m_new
    for h in range(N_HEADS):
        rows = slice(h * HEAD_DIM, (h + 1) * HEAD_DIM)
        acc = acc_scr[rows, :] * alpha[h:h + 1, :]
        for i in range(DEC_PPS):
            acc = acc + v_pages(i, rows) * probs[i][h:h + 1, :]
        acc_scr[rows, :] = acc

    @pl.when(t == n_sub - 1)
    def _():
        do_ref[...] = jnp.where(mine, jnp.sum(acc_scr[...], axis=1, keepdims=True), do_ref[...])
        dm_ref[...] = jnp.where(mine, m_scr[...], dm_ref[...])
        dl_ref[...] = jnp.where(mine, l_scr[...], dl_ref[...])


def _attn_kernel(pt_ref, qt_ref, ka_ref, vt_ref, ct_ref, kam_ref, vtm_ref, qd_ref, low_ref,
                 ck_hbm, cv_hbm, clf_hbm, o_ref, do_ref, dm_ref, dl_ref,
                 kbuf, vbuf, lfbuf, sem, qc_scr, m_scr, l_scr, tail_scr, acc_scr):
    key = lax.broadcasted_iota(jnp.int32, (ATT_T, ATT_T), 0)
    qry = lax.broadcasted_iota(jnp.int32, (ATT_T, ATT_T), 1)

    n_qblk = SEQ // ATT_T
    n_sub = pt_ref.shape[1] // DEC_PPS
    step0 = (pl.program_id(0) * pl.num_programs(1) + pl.program_id(1)) * n_qblk
    total = pl.num_programs(0) * pl.num_programs(1) * n_qblk
    copies = functools.partial(_page_copies, pt_ref, (ck_hbm, cv_hbm, clf_hbm), (kbuf, vbuf, lfbuf), sem)

    @pl.when(step0 == 0)
    def _():
        for c in copies(0, 0):
            c.start()

    def qblock(qi, _):
        it = step0 + qi
        slot = it % 2

        @pl.when(it + 1 < total)
        def _():
            for c in copies(it + 1, 1 - slot):
                c.start()

        for c in copies(it, slot):
            c.wait()
        _decode_substep(it, n_sub, qd_ref, low_ref,
                        lambda i, rows: kbuf[slot, i, rows, :], lambda i, rows: vbuf[slot, i, rows, :],
                        lambda i: lfbuf[slot, i],
                        do_ref, dm_ref, dl_ref, qc_scr, m_scr, l_scr, tail_scr, acc_scr)

        r0 = pl.multiple_of(qi * ATT_T, ATT_T)
        qts = [qt_ref[h, qi] for h in range(ATT_HPS)]
        cqs = [ct_ref[0, h // 2, qi][h % 2:h % 2 + 1, :] for h in range(ATT_HPS)]

        def scores(h, blk):
            c0 = pl.multiple_of(blk * ATT_T, ATT_T)
            return _dot(ka_ref[h, pl.ds(c0, ATT_T), :], qts[h])

        state = []
        for h in range(ATT_HPS):
            s0 = _dot(kam_ref[h], qts[h])
            m = jnp.max(s0, axis=0, keepdims=True) + cqs[h]
            p = jnp.exp(s0 - (m - cqs[h]))
            state += [m, _dot(vtm_ref[h], p.astype(BF16)), scores(h, 0)]

        def kvstep(j, st):
            new = []
            for h in range(ATT_HPS):
                m, acc, s_cur = st[3 * h:3 * h + 3]
                s_next = scores(h, j + 1)
                m, acc = _softmax_step(s_cur, vt_ref[h, j], cqs[h], m, acc)
                new += [m, acc, s_next]
            return tuple(new)

        st = lax.fori_loop(0, qi, kvstep, tuple(state))
        accs = []
        for h in range(ATT_HPS):
            m, acc, s_diag = st[3 * h:3 * h + 3]
            s_diag = jnp.where(key <= qry, s_diag, NEG_INF)
            accs.append(_softmax_step(s_diag, vt_ref[h, qi], cqs[h], m, acc)[1])
        for p in range(ATT_HPS // 2):
            o_ref[p, pl.ds(r0, ATT_T), :] = _attn_finish(accs[2 * p], accs[2 * p + 1])
        return 0

    lax.fori_loop(0, SEQ // ATT_T, qblock, 0)


def _attn(qt, ka, vt, ct, kam, vtm, page_table, qt_dec, low, cache_kt, cache_vt, cache_lft):
    nblk = SEQ // ATT_T
    hps = ATT_HPS
    n_groups = N_HEADS // hps
    n_pages = page_table.shape[1]
    assert N_BATCH * n_groups * nblk == N_SAMPLE * (n_pages // DEC_PPS)
    t_spec = pl.BlockSpec((hps, nblk, LANES, ATT_T), lambda n, g, pt: (g, n, 0, 0))
    const = lambda shape: pl.BlockSpec(shape, lambda n, g, pt: (0,) * len(shape))
    hbm = pl.BlockSpec(memory_space=pl.ANY)
    stat = lambda: pltpu.VMEM((N_HEADS, LANES), F32)
    grid_spec = pltpu.PrefetchScalarGridSpec(
        num_scalar_prefetch=1,
        grid=(N_BATCH, n_groups),
        in_specs=[t_spec,
                  pl.BlockSpec((hps, SEQ, LANES), lambda n, g, pt: (g, n, 0)),
                  t_spec,
                  pl.BlockSpec((1, hps // 2, nblk, SUBLANES, ATT_T), lambda n, g, pt: (n, g, 0, 0, 0)),
                  pl.BlockSpec((hps, N_META, LANES), lambda n, g, pt: (g, 0, 0)),
                  pl.BlockSpec((hps, LANES, N_META), lambda n, g, pt: (g, 0, 0)),
                  const((D_MODEL, N_SAMPLE)), const((PAGE, PAGE)), hbm, hbm, hbm],
        out_specs=(pl.BlockSpec((hps // 2, SEQ, LANES), lambda n, g, pt: (g, n, 0)),
                   const((D_MODEL, N_SAMPLE)), const((N_HEADS, N_SAMPLE)), const((N_HEADS, N_SAMPLE))),
        scratch_shapes=[pltpu.VMEM((2, DEC_PPS, D_MODEL, PAGE), F32),
                        pltpu.VMEM((2, DEC_PPS, D_MODEL, PAGE), F32),
                        pltpu.VMEM((2, DEC_PPS, N_HEADS, PAGE), F32),
                        pltpu.SemaphoreType.DMA((2, 3)),
                        pltpu.VMEM((D_MODEL, LANES), F32), stat(), stat(), stat(),
                        pltpu.VMEM((D_MODEL, LANES), F32)],
    )
    return pl.pallas_call(
        _attn_kernel,
        grid_spec=grid_spec,
        out_shape=(jax.ShapeDtypeStruct((N_PAIRS, N_BATCH * SEQ, LANES), BF16),
                   jax.ShapeDtypeStruct((D_MODEL, N_SAMPLE), F32),
                   jax.ShapeDtypeStruct((N_HEADS, N_SAMPLE), F32),
                   jax.ShapeDtypeStruct((N_HEADS, N_SAMPLE), F32)),
        compiler_params=_cparams(("arbitrary", "arbitrary")),
        name="attn_prompt_decode",
    )(page_table, qt, ka, vt, ct, kam, vtm, qt_dec, low, cache_kt, cache_vt, cache_lft)


def _meta_attn_kernel(qtm_ref, kam_ref, vtm_ref, cmt_ref, o_ref):
    key = lax.broadcasted_iota(jnp.int32, (N_META, N_META), 0)
    qry = lax.broadcasted_iota(jnp.int32, (N_META, N_META), 1)
    for p in range(N_PAIRS):
        accs = []
        for e in range(2):
            h = 2 * p + e
            cq = cmt_ref[p, e:e + 1, :]
            s0 = jnp.where(key <= qry, _dot(kam_ref[h], qtm_ref[h]), NEG_INF)
            m = jnp.max(s0, axis=0, keepdims=True) + cq
            pr = jnp.exp(s0 - (m - cq))
            accs.append(_dot(vtm_ref[h], pr.astype(BF16)))
        o_ref[p] = _attn_finish(*accs)


def _meta_attention(qtm, kam, vtm, cmt):
    pm = (N_PAIRS, N_META, LANES)
    t_shape = (N_HEADS, LANES, N_META)
    o = pl.pallas_call(
        _meta_attn_kernel,
        grid=(1,),
        in_specs=[_full_spec(t_shape), _full_spec((N_HEADS, N_META, LANES)), _full_spec(t_shape),
                  _full_spec((N_PAIRS, SUBLANES, N_META))],
        out_specs=_full_spec(pm),
        out_shape=jax.ShapeDtypeStruct(pm, BF16),
        name="attn_meta",
    )(qtm, kam, vtm, cmt)
    return _pairs_to_rows(o)


def _decode_combine_kernel(acc_ref, m_ref, l_ref, q_ref, kn_ref, vn_ref, lfn_ref, seg_ref, exp_ref, o_ref):
    expand = exp_ref[...]
    ex = lambda x: _dot_split(_split2(x), expand)
    s_new = _dot_split(_split2(q_ref[...].astype(F32) * kn_ref[...]), seg_ref[...])
    m_c = m_ref[...] + lfn_ref[...]
    m_all = jnp.maximum(m_c, s_new)
    w_c = jnp.exp(m_c - m_all)
    w_n = jnp.exp(s_new - m_all)
    den = l_ref[...] * w_c + w_n
    o = (acc_ref[...] * ex(w_c) + vn_ref[...] * ex(w_n)) / ex(den)
    o_ref[...] = o.astype(BF16)


def _decode_combine(acc, m, l, q, kn, vn, lfn, seg, expand):
    shapes = [(N_SAMPLE, D_MODEL), (N_SAMPLE, LANES), (N_SAMPLE, LANES), (N_SAMPLE, D_MODEL),
              (N_SAMPLE, D_MODEL), (N_SAMPLE, D_MODEL), (N_SAMPLE, LANES), (D_MODEL, LANES), (LANES, D_MODEL)]
    return pl.pallas_call(
        _decode_combine_kernel,
        grid=(1,),
        in_specs=[_full_spec(s) for s in shapes],
        out_specs=_full_spec((N_SAMPLE, D_MODEL)),
        out_shape=jax.ShapeDtypeStruct((N_SAMPLE, D_MODEL), BF16),
        name="attn_decode_combine",
    )(acc, m, l, q, kn, vn, lfn, seg, expand)


def _pairs_to_rows(xp):
    return jnp.transpose(xp, (1, 0, 2)).reshape(xp.shape[1], D_MODEL)


def kernel(x_prompt, x_sample, state_ssm_re, state_ssm_im, cache_k, cache_v, cache_logf, state_conv, page_table, meta_tokens, norm_mix_g, norm_ffn_g, ssm_a_re, ssm_a_im, ssm_b_re, ssm_b_im, ssm_c_re, ssm_c_im, ssm_d, ssm_log_step, ssm_w_glu, attn_w_qkv, attn_w_f, attn_b_f, attn_q_g, attn_k_g, attn_w_o, ffn_w_up, ffn_w_gate, ffn_conv_w, ffn_conv_b, ffn_w_down):
    n_body = N_BATCH * SEQ
    xb = x_prompt.reshape(n_body, D_MODEL)
    pad_rows = SMALL_ROWS - N_SAMPLE - N_META
    xs = jnp.concatenate([x_sample.reshape(N_SAMPLE, D_MODEL), meta_tokens.astype(F32),
                          jnp.zeros((pad_rows, D_MODEL), F32)], axis=0)
    meta_rows = slice(META_ROW0, META_ROW0 + N_META)

    g_mix = norm_mix_g.astype(F32).reshape(2, 1, D_MODEL)
    g_ffn = norm_ffn_g.astype(F32).reshape(2, 1, D_MODEL)
    w_glu = ssm_w_glu.astype(BF16)
    w_up = ffn_w_up.astype(BF16)
    w_gate = ffn_w_gate.astype(BF16)
    w_down = ffn_w_down.astype(BF16)
    conv_w = jnp.pad(ffn_conv_w.astype(F32), ((0, 0), (0, SUBLANES - CONV_W), (0, 0)))
    conv_b = ffn_conv_b.astype(F32).reshape(2, 1, D_FF)

    s5w = _s5_weights(ssm_a_re, ssm_a_im, ssm_b_re, ssm_b_im, ssm_c_re, ssm_c_im, ssm_d, ssm_log_step)
    ub = _norm_shuffle(xb, g_mix[0])
    u_small = _norm_rows(xs, g_mix[0])
    n_mc = N_META // S5_CHUNK
    um = u_small[meta_rows].reshape(n_mc, S5_CHUNK, S5_SETS, LANES)
    um = jnp.transpose(um, (2, 0, 1, 3)).reshape(S5_SETS, n_mc, 1, S5_CHUNK * LANES)
    um = jnp.broadcast_to(um, (S5_SETS, n_mc, N_BATCH, S5_CHUNK * LANES)).reshape(S5_SETS, N_META, S5_CHUNK * LANES)
    yb, ym, ys, hpr, hpi, hsr, hsi = _s5(
        ub, um, u_small, state_ssm_re.astype(F32).reshape(N_SAMPLE, -1),
        state_ssm_im.astype(F32).reshape(N_SAMPLE, -1), s5w)
    x1b = _glu_res(xb, yb, w_glu)
    ym_tok = ym.reshape(S5_SETS, n_mc, N_BATCH, S5_CHUNK, LANES)[:, :, 0]
    ym_tok = jnp.transpose(ym_tok, (1, 2, 0, 3)).reshape(N_META, D_MODEL)
    y_small = jnp.concatenate([ys, ym_tok, jnp.zeros((pad_rows, D_MODEL), BF16)], axis=0)
    x1s = _glu_res_rows(xs, y_small, w_glu)

    def ffn_layer(i, xb_in, xs_in, ab, a_small, wa):
        sc = state_conv[i].astype(F32)
        pad = ((0, SMALL_ROWS - N_SAMPLE), (0, 0))
        xs_out, h_small = _ffn_small(xs_in, a_small, wa, g_ffn[i], w_up[i], w_gate[i], conv_w[i], conv_b[i],
                                     w_down[i], jnp.pad(sc[:, 0], pad), jnp.pad(sc[:, 1], pad))
        init = jnp.concatenate([jnp.zeros((SUBLANES - 2, D_FF), F32),
                                h_small[META_ROW0 + N_META - 2:META_ROW0 + N_META]], axis=0)
        xb_out, hl = _ffn(xb_in, ab, wa, g_ffn[i], w_up[i], w_gate[i], conv_w[i], conv_b[i], w_down[i], init)
        tps = SEQ // FFN_TM
        conv_p = hl.reshape(N_BATCH, tps, SUBLANES, D_FF)[:, tps - 1, SUBLANES - 2:, :]
        conv_s = jnp.stack([sc[:, 1], h_small[:N_SAMPLE]], axis=1)
        return xb_out, xs_out, conv_p, conv_s

    x2b, x2s, conv_p0, conv_s0 = ffn_layer(0, x1b, x1s, None, None, None)

    w_qkv = attn_w_qkv.astype(BF16)
    wf = jnp.pad(attn_w_f.astype(BF16), ((0, 0), (0, LANES - N_HEADS)))
    bf = jnp.pad(attn_b_f.astype(F32), (0, LANES - N_HEADS)).reshape(1, LANES)
    qg = (jnp.tile(attn_q_g.astype(F32), N_HEADS) * (HEAD_DIM ** -0.5)).reshape(1, D_MODEL)
    kg = jnp.tile(attn_k_g.astype(F32), N_HEADS).reshape(1, D_MODEL)
    head_of_lane = jnp.arange(D_MODEL) // HEAD_DIM
    onehot = (head_of_lane[:, None] == jnp.arange(LANES)[None, :])
    seg = (onehot.astype(F32) / HEAD_DIM).astype(BF16)
    expand = onehot.T.astype(BF16)
    tri_body = (jnp.arange(FFN_TM)[:, None] >= jnp.arange(FFN_TM)[None, :]).astype(BF16)
    r = jnp.arange(SMALL_ROWS)
    tri_small = ((r[:, None] >= r[None, :]) & (r[None, :] >= META_ROW0) & (r[:, None] < META_ROW0 + N_META))
    tri_small = tri_small.astype(BF16)

    swp = expand.reshape(LANES // 2, 2, D_MODEL)[:, ::-1].reshape(LANES, D_MODEL)

    kts_f32, vts_f32, qts, kas, vts, lfs, cts = _qkv(x2s, g_mix[1], w_qkv, wf, bf, qg, kg, seg, expand, swp,
                                                     tri_small, jnp.zeros((1, LANES), F32), tm=SMALL_ROWS)
    c_meta_total = cts[:, META_ROW0 + N_META - 1].reshape(1, LANES)
    ktb_f32, vtb_f32, qtb, kab, vtb, lfb, ctb = _qkv(x2b, g_mix[1], w_qkv, wf, bf, qg, kg, seg, expand, swp,
                                                     tri_body, c_meta_total, tm=FFN_TM)
    kts_f32, vts_f32 = kts_f32[0], vts_f32[0]

    nblk = SEQ // ATT_T
    ct = ctb[:N_HEADS].reshape(N_PAIRS, 2, N_BATCH, nblk, ATT_T)
    ct = jnp.transpose(ct, (2, 0, 3, 1, 4))
    ct = jnp.pad(ct, ((0, 0), (0, 0), (0, 0), (0, SUBLANES - 2), (0, 0)))
    cmt = cts[:N_HEADS, meta_rows].reshape(N_PAIRS, 2, N_META)
    cmt = jnp.pad(cmt, ((0, 0), (0, SUBLANES - 2), (0, 0)))
    qtm = qts[:, 0, :, meta_rows]
    kam = kas[:, meta_rows]
    vtm = vts[:, 0, :, meta_rows]
    o_meta = _meta_attention(qtm, kam, vtm, cmt)

    n_pool = cache_k.shape[0]
    cache_kt = jnp.transpose(cache_k, (0, 2, 3, 1)).reshape(n_pool, D_MODEL, PAGE)
    cache_vt = jnp.transpose(cache_v, (0, 2, 3, 1)).reshape(n_pool, D_MODEL, PAGE)
    cache_lft = jnp.transpose(cache_logf.astype(F32), (0, 2, 1))
    qt_pairs = qts[:, 0, :, :N_SAMPLE].reshape(N_PAIRS, 2, 2, HEAD_DIM, N_SAMPLE)
    qt_dec = jnp.stack([qt_pairs[:, 0, 0], qt_pairs[:, 1, 1]], axis=1).reshape(D_MODEL, N_SAMPLE)
    low = (jnp.arange(PAGE)[:, None] > jnp.arange(PAGE)[None, :]).astype(BF16)
    ob, acc_t, m_t, l_t = _attn(qtb, kab, vtb, ct, kam, vtm, page_table.astype(jnp.int32), qt_dec.astype(F32), low,
                                cache_kt, cache_vt, cache_lft)
    pad_heads = lambda x, v: jnp.pad(x.T, ((0, 0), (0, LANES - N_HEADS)), constant_values=v)
    o_dec = _decode_combine(acc_t.T, pad_heads(m_t, 0.0), pad_heads(l_t, 1.0), qt_dec.T, kts_f32[:, :N_SAMPLE].T,
                            vts_f32[:, :N_SAMPLE].T, lfs[:N_SAMPLE], (onehot.astype(BF16)), expand)
    o_small = jnp.concatenate([o_dec, o_meta, jnp.zeros((pad_rows, D_MODEL), BF16)], axis=0)

    w_o = attn_w_o.astype(BF16)
    x4b, x4s, conv_p1, conv_s1 = ffn_layer(1, x2b, x2s, ob, o_small, w_o)

    def kv_with_meta(small_t, body_t):
        m = jnp.broadcast_to(small_t[None, :, meta_rows], (N_BATCH, D_MODEL, N_META))
        full = jnp.concatenate([m, body_t], axis=2).reshape(N_BATCH, N_HEADS, HEAD_DIM, SEQ + N_META)
        return jnp.transpose(full, (0, 3, 1, 2))

    def head_cols(x_t):
        return jnp.transpose(x_t[:, :N_SAMPLE].reshape(N_HEADS, HEAD_DIM, N_SAMPLE), (2, 0, 1))[:, None]

    k_p = kv_with_meta(kts_f32, ktb_f32)
    v_p = kv_with_meta(vts_f32, vtb_f32)
    lf_meta = jnp.broadcast_to(lfs[meta_rows, :N_HEADS][None], (N_BATCH, N_META, N_HEADS))
    lf_p = jnp.concatenate([lf_meta, lfb[:, :N_HEADS].reshape(N_BATCH, SEQ, N_HEADS)], axis=1)
    return (x4b.reshape(N_BATCH, SEQ, D_MODEL),
            x4s[:N_SAMPLE].reshape(N_SAMPLE, 1, D_MODEL),
            hpr.reshape(N_BATCH, N_GROUPS, STATE_P), hpi.reshape(N_BATCH, N_GROUPS, STATE_P),
            hsr.reshape(N_SAMPLE, N_GROUPS, STATE_P), hsi.reshape(N_SAMPLE, N_GROUPS, STATE_P),
            k_p, v_p, lf_p,
            head_cols(kts_f32), head_cols(vts_f32),
            lfs[:N_SAMPLE, :N_HEADS].reshape(N_SAMPLE, 1, N_HEADS),
            jnp.stack([conv_p0, conv_p1]), jnp.stack([conv_s0, conv_s1]))
```

```python
import functools

import jax
import jax.numpy as jnp
from jax import lax
from jax.experimental import pallas as pl
from jax.experimental.pallas import tpu as pltpu

F32 = jnp.float32
BF16 = jnp.bfloat16

D_MODEL = 1024
N_BATCH = 8
SEQ = 2048
N_SAMPLE = 128
N_META = 16
SMALL_ROWS = 256
META_ROW0 = 128
N_GROUPS = 64
STATE_P = 64
SSM_GROUP = 16
N_HEADS = 16
HEAD_DIM = 64
N_PAIRS = N_HEADS // 2
D_FF = 2816
CONV_W = 3
PAGE = 128
EPS = 1e-6
NEG_INF = -1e30

LANES = 128
SUBLANES = 8
S5_CHUNK = 8
S5_SETS = D_MODEL // LANES
SET_STATE = (LANES // SSM_GROUP) * STATE_P
N_CHUNKS = SEQ // S5_CHUNK
VMEM_LIMIT = 56 * 1024 * 1024


def _cparams(sem):
    return pltpu.CompilerParams(dimension_semantics=sem, vmem_limit_bytes=VMEM_LIMIT)


def _const_spec(shape):
    nd = len(shape)
    return pl.BlockSpec(shape, lambda *_: (0,) * nd, pipeline_mode=pl.Buffered(1))


def _full_spec(shape):
    nd = len(shape)
    return pl.BlockSpec(shape, lambda *_: (0,) * nd)


def _dot(a, b):
    return jnp.dot(a, b, preferred_element_type=F32)


def _dot_f32(a, b):
    return jnp.dot(a, b, preferred_element_type=F32, precision=lax.Precision.HIGHEST)


def _split2(x):
    hi = x.astype(BF16)
    lo = (x - hi.astype(F32)).astype(BF16)
    return hi, lo


def _split3(x):
    hi = x.astype(BF16)
    r = x - hi.astype(F32)
    mid = r.astype(BF16)
    lo = (r - mid.astype(F32)).astype(BF16)
    return hi, mid, lo


def _dot_split(parts, w):
    acc = _dot(parts[0], w)
    for p in parts[1:]:
        acc = acc + _dot(p, w)
    return acc


def _rmsnorm(x, g):
    ms = jnp.mean(x * x, axis=-1, keepdims=True)
    return x * lax.rsqrt(ms + EPS) * g


def _gelu_tanh(x):
    c = 0.7978845608028654
    return 0.5 * x * (1.0 + jnp.tanh(c * (x + 0.044715 * (x * x * x))))


def _sigmoid(x):
    return 1.0 / (1.0 + jnp.exp(-x))


def _norm_shuffle_kernel(x_ref, g_ref, o_ref, scr):
    rb = o_ref.shape[1]
    u = _rmsnorm(x_ref[...], g_ref[...])
    for j in range(S5_SETS):
        scr[j] = u[:, j * LANES:(j + 1) * LANES]
    for j in range(S5_SETS):
        for s in range(S5_CHUNK):
            o_ref[j, :, s * LANES:(s + 1) * LANES] = scr[j, pl.ds(s, rb, stride=S5_CHUNK), :].astype(BF16)


def _norm_shuffle(x, g):
    rows = x.shape[0] // S5_CHUNK
    rb = 128
    return pl.pallas_call(
        _norm_shuffle_kernel,
        grid=(rows // rb,),
        in_specs=[pl.BlockSpec((rb * S5_CHUNK, D_MODEL), lambda i: (i, 0)),
                  _const_spec((1, D_MODEL))],
        out_specs=pl.BlockSpec((S5_SETS, rb, S5_CHUNK * LANES), lambda i: (0, i, 0)),
        out_shape=jax.ShapeDtypeStruct((S5_SETS, rows, S5_CHUNK * LANES), BF16),
        scratch_shapes=[pltpu.VMEM((S5_SETS, rb * S5_CHUNK, LANES), F32)],
        compiler_params=_cparams(("parallel",)),
        name="norm_shuffle",
    )(x, g)


def _norm_rows_kernel(x_ref, g_ref, o_ref):
    o_ref[...] = _rmsnorm(x_ref[...], g_ref[...]).astype(BF16)


def _norm_rows(x, g):
    rows = x.shape[0]
    return pl.pallas_call(
        _norm_rows_kernel,
        grid=(1,),
        in_specs=[_full_spec((rows, D_MODEL)), _full_spec((1, D_MODEL))],
        out_specs=_full_spec((rows, D_MODEL)),
        out_shape=jax.ShapeDtypeStruct((rows, D_MODEL), BF16),
        name="norm_rows",
    )(x, g)


S5_ROW_BLOCK = 512
N_SLABS = SET_STATE // LANES
SCAN_PITCH = N_CHUNKS + SUBLANES


def _s5_kernel(ub_ref, um_ref, us_ref, h0r_ref, h0i_ref, wt_ref, wb_ref, wc_ref, a8_ref, dt_ref,
               wb1_ref, wc1_ref, a1_ref, d1_ref,
               yb_ref, ym_ref, ys_ref, hpr_ref, hpi_ref, hsr_ref, hsi_ref, xr_scr, xi_scr):
    rows = ub_ref.shape[1]
    wb = wb_ref[0]
    wt = wt_ref[0]
    wc = wc_ref[0]
    dt = dt_ref[0]

    seq_per_block = S5_ROW_BLOCK // N_CHUNKS

    def scratch_rows(n):
        return slice(n * SCAN_PITCH, n * SCAN_PITCH + N_CHUNKS)

    for r in range(rows // S5_ROW_BLOCK):
        x = _dot(ub_ref[0, r * S5_ROW_BLOCK:(r + 1) * S5_ROW_BLOCK, :], wb)
        for q in range(seq_per_block):
            xs = x[q * N_CHUNKS:(q + 1) * N_CHUNKS]
            ss = scratch_rows(r * seq_per_block + q)
            for k in range(N_SLABS):
                xr_scr[k, ss, :] = xs[:, k * LANES:(k + 1) * LANES]
                xi_scr[k, ss, :] = xs[:, SET_STATE + k * LANES:SET_STATE + (k + 1) * LANES]
    xm = _dot(um_ref[0], wb)

    a8r = jnp.broadcast_to(a8_ref[0, 0:1, :], (SUBLANES, SET_STATE))
    a8i = jnp.broadcast_to(a8_ref[0, 1:2, :], (SUBLANES, SET_STATE))

    hr = jnp.zeros((SUBLANES, SET_STATE), F32)
    hi = jnp.zeros((SUBLANES, SET_STATE), F32)
    hm_prev = []
    for cm in range(N_META // S5_CHUNK):
        hm_prev.append(jnp.concatenate([hr, hi], axis=1))
        xr = xm[cm * SUBLANES:(cm + 1) * SUBLANES, :SET_STATE]
        xi = xm[cm * SUBLANES:(cm + 1) * SUBLANES, SET_STATE:]
        hr, hi = a8r * hr - a8i * hi + xr, a8r * hi + a8i * hr + xi

    ar = [a8r[:, k * LANES:(k + 1) * LANES] for k in range(N_SLABS)]
    ai = [a8i[:, k * LANES:(k + 1) * LANES] for k in range(N_SLABS)]

    def step(c, carry):
        hrs, his = carry
        nr, ni = [], []
        for k in range(N_SLABS):
            idx = (k, pl.ds(c, N_BATCH, stride=SCAN_PITCH), slice(None))
            xr = xr_scr[idx]
            xi = xi_scr[idx]
            xr_scr[idx] = hrs[k]
            xi_scr[idx] = his[k]
            nr.append(ar[k] * hrs[k] - ai[k] * his[k] + xr)
            ni.append(ar[k] * his[k] + ai[k] * hrs[k] + xi)
        return tuple(nr), tuple(ni)

    init = (tuple(hr[:, k * LANES:(k + 1) * LANES] for k in range(N_SLABS)),
            tuple(hi[:, k * LANES:(k + 1) * LANES] for k in range(N_SLABS)))
    hrs, his = lax.fori_loop(0, N_CHUNKS, step, init)
    hpr_ref[...] = jnp.concatenate(hrs, axis=1)
    hpi_ref[...] = jnp.concatenate(his, axis=1)

    def emit(u, hprev):
        blk = 2 * LANES
        y_in = jnp.concatenate([_dot(u[:, :c + blk], wt[:c + blk, c:c + blk])
                                for c in range(0, S5_CHUNK * LANES, blk)], axis=1)
        y = y_in + _dot(hprev.astype(BF16), wc) + dt * u.astype(F32)
        return _gelu_tanh(y).astype(BF16)

    for r in range(rows // S5_ROW_BLOCK):
        rs = slice(r * S5_ROW_BLOCK, (r + 1) * S5_ROW_BLOCK)
        hprev = jnp.concatenate(
            [jnp.concatenate([xr_scr[k, scratch_rows(r * seq_per_block + q), :] for k in range(N_SLABS)]
                             + [xi_scr[k, scratch_rows(r * seq_per_block + q), :] for k in range(N_SLABS)], axis=1)
             for q in range(seq_per_block)], axis=0)
        yb_ref[0, rs, :] = emit(ub_ref[0, rs, :], hprev)
    ym_ref[0] = emit(um_ref[0], jnp.concatenate(hm_prev, axis=0))

    us = us_ref[...]
    xs = _dot(us, wb1_ref[0])
    a1r = a1_ref[0, 0:1, :]
    a1i = a1_ref[0, 1:2, :]
    h0r = h0r_ref[...]
    h0i = h0i_ref[...]
    hsr = a1r * h0r - a1i * h0i + xs[:, :SET_STATE]
    hsi = a1r * h0i + a1i * h0r + xs[:, SET_STATE:]
    hsr_ref[...] = hsr
    hsi_ref[...] = hsi
    hs = jnp.concatenate([hsr, hsi], axis=1).astype(BF16)
    ys = _dot(hs, wc1_ref[0]) + d1_ref[0] * us.astype(F32)
    ys_ref[...] = _gelu_tanh(ys).astype(BF16)


def _set_spec(shape):
    return pl.BlockSpec((1,) + shape, lambda j: (j,) + (0,) * len(shape))


def _s5(ub, um, u_small, h0r, h0i, w):
    rows = ub.shape[1]
    cw = S5_CHUNK * LANES
    out_shapes = (
        jax.ShapeDtypeStruct((S5_SETS, rows, cw), BF16),
        jax.ShapeDtypeStruct((S5_SETS, N_META, cw), BF16),
        jax.ShapeDtypeStruct((N_SAMPLE, D_MODEL), BF16),
        jax.ShapeDtypeStruct((N_BATCH, S5_SETS * SET_STATE), F32),
        jax.ShapeDtypeStruct((N_BATCH, S5_SETS * SET_STATE), F32),
        jax.ShapeDtypeStruct((N_SAMPLE, S5_SETS * SET_STATE), F32),
        jax.ShapeDtypeStruct((N_SAMPLE, S5_SETS * SET_STATE), F32),
    )
    return pl.pallas_call(
        _s5_kernel,
        grid=(S5_SETS,),
        in_specs=[
            _set_spec((rows, cw)),
            _set_spec((N_META, cw)),
            pl.BlockSpec((N_SAMPLE, LANES), lambda j: (0, j)),
            pl.BlockSpec((N_SAMPLE, SET_STATE), lambda j: (0, j)),
            pl.BlockSpec((N_SAMPLE, SET_STATE), lambda j: (0, j)),
            _set_spec((cw, cw)),
            _set_spec((cw, 2 * SET_STATE)),
            _set_spec((2 * SET_STATE, cw)),
            _set_spec((2, SET_STATE)),
            _set_spec((1, cw)),
            _set_spec((LANES, 2 * SET_STATE)),
            _set_spec((2 * SET_STATE, LANES)),
            _set_spec((2, SET_STATE)),
            _set_spec((1, LANES)),
        ],
        out_specs=(
            _set_spec((rows, cw)),
            _set_spec((N_META, cw)),
            pl.BlockSpec((N_SAMPLE, LANES), lambda j: (0, j)),
            pl.BlockSpec((N_BATCH, SET_STATE), lambda j: (0, j)),
            pl.BlockSpec((N_BATCH, SET_STATE), lambda j: (0, j)),
            pl.BlockSpec((N_SAMPLE, SET_STATE), lambda j: (0, j)),
            pl.BlockSpec((N_SAMPLE, SET_STATE), lambda j: (0, j)),
        ),
        out_shape=out_shapes,
        scratch_shapes=[pltpu.VMEM((N_SLABS, N_BATCH * SCAN_PITCH, LANES), F32),
                        pltpu.VMEM((N_SLABS, N_BATCH * SCAN_PITCH, LANES), F32)],
        compiler_params=_cparams(("parallel",)),
        name="s5_mixer",
    )(ub, um, u_small, h0r, h0i, w["wt"], w["wb"], w["wc"], w["a8"], w["dt"],
      w["wb1"], w["wc1"], w["a1"], w["d1"])


def _s5_discretize(a_re, a_im, log_step):
    dt = jnp.exp(log_step)
    mag = jnp.exp(a_re * dt)
    ab_re = mag * jnp.cos(a_im * dt)
    ab_im = mag * jnp.sin(a_im * dt)
    den = a_re * a_re + a_im * a_im
    n_re = ab_re - 1.0
    n_im = ab_im
    f_re = (n_re * a_re + n_im * a_im) / den
    f_im = (n_im * a_re - n_re * a_im) / den
    return ab_re, ab_im, f_re, f_im


def _s5_weights_kernel(arow_ref, acr_ref, aci_ref, acl_ref, br_ref, bi_ref, cr_ref, ci_ref,
                       wt_ref, wb_ref, wc_ref, a8_ref, wb1_ref, wc1_ref, a1_ref):
    arow = arow_ref[0]
    abr, abi, fr, fi = _s5_discretize(arow[0:1], arow[1:2], arow[2:3])
    abr_c, abi_c, _, _ = _s5_discretize(acr_ref[0], aci_ref[0], acl_ref[0])
    br = br_ref[0]
    bi = bi_ref[0]
    b0r = fr * br - fi * bi
    b0i = fr * bi + fi * br
    cr = cr_ref[0]
    ci = ci_ref[0]
    wt_ref[0] = jnp.zeros(wt_ref.shape[1:], BF16)
    pr = jnp.ones_like(abr)
    pi = jnp.zeros_like(abr)
    pcr, pci = abr_c, abi_c
    for k in range(S5_CHUNK):
        bkr = pr * b0r - pi * b0i
        bki = pr * b0i + pi * b0r
        s = S5_CHUNK - 1 - k
        wb_ref[0, s * LANES:(s + 1) * LANES, :SET_STATE] = bkr.astype(BF16)
        wb_ref[0, s * LANES:(s + 1) * LANES, SET_STATE:] = bki.astype(BF16)
        kern = (_dot_f32(bkr, cr) - _dot_f32(bki, ci)).astype(BF16)
        for s2 in range(S5_CHUNK - k):
            t2 = s2 + k
            wt_ref[0, s2 * LANES:(s2 + 1) * LANES, t2 * LANES:(t2 + 1) * LANES] = kern
        wc_ref[0, :SET_STATE, k * LANES:(k + 1) * LANES] = (pcr * cr - pci * ci).astype(BF16)
        wc_ref[0, SET_STATE:, k * LANES:(k + 1) * LANES] = (-(pcr * ci + pci * cr)).astype(BF16)
        pr, pi = pr * abr - pi * abi, pr * abi + pi * abr
        pcr, pci = pcr * abr_c - pci * abi_c, pcr * abi_c + pci * abr_c
    a8_ref[0, 0:1, :] = pr
    a8_ref[0, 1:2, :] = pi
    a1_ref[0, 0:1, :] = abr
    a1_ref[0, 1:2, :] = abi
    wb1_ref[0, :, :SET_STATE] = b0r.astype(BF16)
    wb1_ref[0, :, SET_STATE:] = b0i.astype(BF16)
    wc1_ref[0, :SET_STATE, :] = cr.astype(BF16)
    wc1_ref[0, SET_STATE:, :] = (-ci).astype(BF16)


def _s5_weights(a_re, a_im, b_re, b_im, c_re, c_im, d_skip, log_step):
    gs = LANES // SSM_GROUP
    cw = S5_CHUNK * LANES
    eye = jnp.eye(gs, dtype=F32)
    per_state = lambda v: v.astype(F32).reshape(S5_SETS, SET_STATE)
    a_re_s, a_im_s = per_state(a_re), per_state(a_im)
    ls_s = per_state(jnp.broadcast_to(log_step[:, None], (N_GROUPS, STATE_P)))
    arow = jnp.stack([a_re_s, a_im_s, ls_s] + [jnp.zeros_like(ls_s)] * (SUBLANES - 3), axis=1)
    col = lambda v: jnp.broadcast_to(v[:, :, None], (S5_SETS, SET_STATE, LANES))

    def blockdiag_b(b):
        b = b.astype(F32).reshape(S5_SETS, gs, STATE_P, SSM_GROUP)
        return jnp.einsum('jgpi,gh->jgihp', b, eye).reshape(S5_SETS, LANES, SET_STATE)

    def blockdiag_c(c):
        c = c.astype(F32).reshape(S5_SETS, gs, SSM_GROUP, STATE_P)
        return jnp.einsum('jgop,gh->jgpho', c, eye).reshape(S5_SETS, SET_STATE, LANES)

    out_shapes = (
        jax.ShapeDtypeStruct((S5_SETS, cw, cw), BF16),
        jax.ShapeDtypeStruct((S5_SETS, cw, 2 * SET_STATE), BF16),
        jax.ShapeDtypeStruct((S5_SETS, 2 * SET_STATE, cw), BF16),
        jax.ShapeDtypeStruct((S5_SETS, 2, SET_STATE), F32),
        jax.ShapeDtypeStruct((S5_SETS, LANES, 2 * SET_STATE), BF16),
        jax.ShapeDtypeStruct((S5_SETS, 2 * SET_STATE, LANES), BF16),
        jax.ShapeDtypeStruct((S5_SETS, 2, SET_STATE), F32),
    )
    wt, wb, wc, a8, wb1, wc1, a1 = pl.pallas_call(
        _s5_weights_kernel,
        grid=(S5_SETS,),
        in_specs=[_set_spec((SUBLANES, SET_STATE)), _set_spec((SET_STATE, LANES)), _set_spec((SET_STATE, LANES)),
                  _set_spec((SET_STATE, LANES)), _set_spec((LANES, SET_STATE)), _set_spec((LANES, SET_STATE)),
                  _set_spec((SET_STATE, LANES)), _set_spec((SET_STATE, LANES))],
        out_specs=tuple(_set_spec(s.shape[1:]) for s in out_shapes),
        out_shape=out_shapes,
        compiler_params=_cparams(("parallel",)),
        name="s5_weights",
    )(arow, col(a_re_s), col(a_im_s), col(ls_s), blockdiag_b(b_re), blockdiag_b(b_im),
      blockdiag_c(c_re), blockdiag_c(c_im))
    d1 = d_skip.astype(F32).reshape(S5_SETS, 1, LANES)
    return {"wt": wt, "wb": wb, "wc": wc, "a8": a8, "dt": jnp.tile(d1, (1, 1, S5_CHUNK)),
            "wb1": wb1, "wc1": wc1, "a1": a1, "d1": d1}


def _glu(y, w):
    z = _dot(y, w)
    return z[:, :D_MODEL] * _sigmoid(z[:, D_MODEL:])


def _glu_res_kernel(x_ref, y_ref, w_ref, o_ref, scr):
    rb = y_ref.shape[1]
    w = w_ref[...]
    for s in range(S5_CHUNK):
        y = jnp.concatenate([y_ref[j, :, s * LANES:(s + 1) * LANES] for j in range(S5_SETS)], axis=1)
        o = _glu(y, w)
        for j in range(S5_SETS):
            scr[j, pl.ds(s, rb, stride=S5_CHUNK), :] = o[:, j * LANES:(j + 1) * LANES]
    for j in range(S5_SETS):
        cs = slice(j * LANES, (j + 1) * LANES)
        o_ref[:, cs] = x_ref[:, cs] + scr[j]


def _glu_res(x, y, w_glu):
    rows = y.shape[1]
    rb = 128
    tok_spec = pl.BlockSpec((rb * S5_CHUNK, D_MODEL), lambda i: (i, 0))
    return pl.pallas_call(
        _glu_res_kernel,
        grid=(rows // rb,),
        in_specs=[tok_spec, pl.BlockSpec((S5_SETS, rb, S5_CHUNK * LANES), lambda i: (0, i, 0)),
                  _const_spec((D_MODEL, 2 * D_MODEL))],
        out_specs=tok_spec,
        out_shape=jax.ShapeDtypeStruct(x.shape, F32),
        scratch_shapes=[pltpu.VMEM((S5_SETS, rb * S5_CHUNK, LANES), F32)],
        compiler_params=_cparams(("parallel",)),
        name="glu_res",
    )(x, y, w_glu)


def _glu_res_rows_kernel(x_ref, y_ref, w_ref, o_ref):
    o_ref[...] = x_ref[...] + _glu(y_ref[...], w_ref[...])


def _glu_res_rows(x, y, w_glu):
    rows = x.shape[0]
    return pl.pallas_call(
        _glu_res_rows_kernel,
        grid=(1,),
        in_specs=[_full_spec((rows, D_MODEL)), _full_spec((rows, D_MODEL)), _full_spec((D_MODEL, 2 * D_MODEL))],
        out_specs=_full_spec((rows, D_MODEL)),
        out_shape=jax.ShapeDtypeStruct((rows, D_MODEL), F32),
        name="glu_res_rows",
    )(x, y, w_glu)


FFN_TM = 512
FFN_FC = D_FF // 2


def _conv_act(h, gt, p1, p2, cw, cb):
    hc = cb + cw[0:1, :] * p2
    hc = hc + cw[1:2, :] * p1
    hc = hc + cw[2:3, :] * h
    return (hc * _sigmoid(hc) * gt).astype(BF16)


def _ffn_kernel(*refs, proj, tiles_per_seq):
    if proj:
        (x_ref, a_ref, wa_ref, g_ref, wu_ref, wg_ref, cw_ref, cb_ref, wd_ref, init_ref,
         o_ref, hl_ref, carry_scr, act_scr) = refs
    else:
        (x_ref, g_ref, wu_ref, wg_ref, cw_ref, cb_ref, wd_ref, init_ref,
         o_ref, hl_ref, carry_scr, act_scr) = refs
    i = pl.program_id(0)

    @pl.when(i % tiles_per_seq == 0)
    def _():
        carry_scr[...] = init_ref[...]

    x = x_ref[...]
    if proj:
        a = jnp.concatenate([a_ref[p] for p in range(N_PAIRS)], axis=1)
        x = x + _dot(a, wa_ref[...])
    u = _rmsnorm(x, g_ref[...]).astype(BF16)
    tm = x.shape[0]
    row8 = lax.broadcasted_iota(jnp.int32, (SUBLANES, 1), 0)
    out = x
    for c in range(D_FF // FFN_FC):
        cs = slice(c * FFN_FC, (c + 1) * FFN_FC)
        h = _dot(u, wu_ref[:, cs])
        gt = _dot(u, wg_ref[:, cs])
        cw = cw_ref[:, cs]
        cb = cb_ref[:, cs]
        carry = carry_scr[:, cs]
        p1 = pltpu.roll(h, 1, axis=0)
        p2 = pltpu.roll(h, 2, axis=0)
        act_scr[...] = _conv_act(h, gt, p1, p2, cw, cb)
        c1 = pltpu.roll(carry, 1, axis=0)
        c2 = pltpu.roll(carry, 2, axis=0)
        p1h = jnp.where(row8 < 1, c1, p1[:SUBLANES])
        p2h = jnp.where(row8 < 2, c2, p2[:SUBLANES])
        act_scr[0:SUBLANES, :] = _conv_act(h[:SUBLANES], gt[:SUBLANES], p1h, p2h, cw, cb)
        last = h[tm - SUBLANES:, :]
        carry_scr[:, cs] = last
        hl_ref[0, :, cs] = last
        out = out + _dot(act_scr[...], wd_ref[cs, :])
    o_ref[...] = out


def _ffn(x, a, wa, g, wu, wg, cw, cb, wd, init):
    rows = x.shape[0]
    tm = FFN_TM
    proj = a is not None
    row_spec = pl.BlockSpec((tm, D_MODEL), lambda i: (i, 0))
    in_specs = [row_spec]
    args = [x]
    if proj:
        in_specs += [pl.BlockSpec((N_PAIRS, tm, LANES), lambda i: (0, i, 0)), _const_spec((D_MODEL, D_MODEL))]
        args += [a, wa]
    in_specs += [_const_spec((1, D_MODEL)), _const_spec((D_MODEL, D_FF)), _const_spec((D_MODEL, D_FF)),
                 _const_spec((SUBLANES, D_FF)), _const_spec((1, D_FF)), _const_spec((D_FF, D_MODEL)),
                 _const_spec((SUBLANES, D_FF))]
    args += [g, wu, wg, cw, cb, wd, init]
    return pl.pallas_call(
        functools.partial(_ffn_kernel, proj=proj, tiles_per_seq=SEQ // tm),
        grid=(rows // tm,),
        in_specs=in_specs,
        out_specs=(row_spec, pl.BlockSpec((1, SUBLANES, D_FF), lambda i: (i, 0, 0))),
        out_shape=(jax.ShapeDtypeStruct((rows, D_MODEL), F32),
                   jax.ShapeDtypeStruct((rows // tm, SUBLANES, D_FF), F32)),
        scratch_shapes=[pltpu.VMEM((SUBLANES, D_FF), F32), pltpu.VMEM((tm, FFN_FC), BF16)],
        compiler_params=_cparams(("arbitrary",)),
        name="ffn_proj" if proj else "ffn",
    )(*args)


def _ffn_small_kernel(*refs, proj):
    if proj:
        (x_ref, a_ref, wa_ref, g_ref, wu_ref, wg_ref, cw_ref, cb_ref, wd_ref, b0_ref, b1_ref,
         o_ref, h_ref) = refs
    else:
        (x_ref, g_ref, wu_ref, wg_ref, cw_ref, cb_ref, wd_ref, b0_ref, b1_ref, o_ref, h_ref) = refs
    x = x_ref[...]
    if proj:
        x = x + _dot(a_ref[...], wa_ref[...])
    u = _rmsnorm(x, g_ref[...]).astype(BF16)
    row = lax.broadcasted_iota(jnp.int32, (SMALL_ROWS, 1), 0)
    is_sample = row < N_SAMPLE
    meta_end = META_ROW0 + N_META
    out = x
    for c in range(D_FF // FFN_FC):
        cs = slice(c * FFN_FC, (c + 1) * FFN_FC)
        h = _dot(u, wu_ref[:, cs])
        gt = _dot(u, wg_ref[:, cs])
        p1 = jnp.where(is_sample, b1_ref[:, cs],
                       jnp.where((row >= META_ROW0 + 1) & (row < meta_end), pltpu.roll(h, 1, axis=0), 0.0))
        p2 = jnp.where(is_sample, b0_ref[:, cs],
                       jnp.where((row >= META_ROW0 + 2) & (row < meta_end), pltpu.roll(h, 2, axis=0), 0.0))
        act = _conv_act(h, gt, p1, p2, cw_ref[:, cs], cb_ref[:, cs])
        h_ref[:, cs] = h
        out = out + _dot(act, wd_ref[cs, :])
    o_ref[...] = out


def _ffn_small(x, a, wa, g, wu, wg, cw, cb, wd, b0, b1):
    proj = a is not None
    in_specs = [_full_spec((SMALL_ROWS, D_MODEL))]
    args = [x]
    if proj:
        in_specs += [_full_spec((SMALL_ROWS, D_MODEL)), _full_spec((D_MODEL, D_MODEL))]
        args += [a, wa]
    in_specs += [_full_spec((1, D_MODEL)), _full_spec((D_MODEL, D_FF)), _full_spec((D_MODEL, D_FF)),
                 _full_spec((SUBLANES, D_FF)), _full_spec((1, D_FF)), _full_spec((D_FF, D_MODEL)),
                 _full_spec((SMALL_ROWS, D_FF)), _full_spec((SMALL_ROWS, D_FF))]
    args += [g, wu, wg, cw, cb, wd, b0, b1]
    return pl.pallas_call(
        functools.partial(_ffn_small_kernel, proj=proj),
        grid=(1,),
        in_specs=in_specs,
        out_specs=(_full_spec((SMALL_ROWS, D_MODEL)), _full_spec((SMALL_ROWS, D_FF))),
        out_shape=(jax.ShapeDtypeStruct((SMALL_ROWS, D_MODEL), F32),
                   jax.ShapeDtypeStruct((SMALL_ROWS, D_FF), F32)),
        compiler_params=_cparams(("arbitrary",)),
        name="ffn_small_proj" if proj else "ffn_small",
    )(*args)


ATT_T = 256
ATT_HPS = 4
DEC_PPS = 8


def _qkv_kernel(x_ref, g_ref, w_ref, wf_ref, bf_ref, qg_ref, kg_ref, seg_ref, exp_ref, swp_ref, tri_ref, cinit_ref,
                kt_ref, vtf_ref, qt_ref, ka_ref, vt_ref, lf_ref, ct_ref, carry_scr, *, tiles_per_seq):
    i = pl.program_id(0)

    @pl.when(i % tiles_per_seq == 0)
    def _():
        carry_scr[...] = jnp.broadcast_to(cinit_ref[...], carry_scr.shape)

    u = _rmsnorm(x_ref[...], g_ref[...]).astype(BF16)
    qkv = _dot(u, w_ref[...])
    seg = seg_ref[...]
    expand = exp_ref[...]

    def head_norm(t, gain):
        ms = _dot((t * t).astype(BF16), seg)
        rinv = lax.rsqrt(ms + EPS)
        return t * _dot_split(_split2(rinv), expand) * gain

    q = head_norm(qkv[:, :D_MODEL], qg_ref[...])
    k = head_norm(qkv[:, D_MODEL:2 * D_MODEL], kg_ref[...])
    v = qkv[:, 2 * D_MODEL:]
    tm = k.shape[0]

    z = _dot(u, wf_ref[...]) + bf_ref[...]
    lf = jnp.minimum(z, 0.0) - jnp.log(1.0 + jnp.exp(-jnp.abs(z)))
    lf_ref[...] = lf
    tri = tri_ref[...]
    c = carry_scr[SUBLANES - 1:SUBLANES, :]
    for part in _split3(lf):
        c = c + _dot(tri, part)
    carry_scr[...] = c[tm - SUBLANES:, :]
    ct_ref[...] = c.T

    swp = swp_ref[...]
    parts = [_dot(part, swp) for part in _split3(-c)]
    l64 = lax.broadcasted_iota(jnp.int32, (1, D_MODEL), 1) % HEAD_DIM
    k_bias = jnp.where(l64 == 0, parts[0], jnp.where(l64 == 1, parts[1], jnp.where(l64 == 2, parts[2], 0.0)))
    lane = lax.broadcasted_iota(jnp.int32, (1, LANES), 1)
    row = lax.broadcasted_iota(jnp.int32, (LANES, 1), 0)
    q_ones = (row % HEAD_DIM < len(parts)).astype(F32)
    for p in range(N_PAIRS):
        ps = slice(p * LANES, (p + 1) * LANES)
        for e in range(2):
            own = (lane >= HEAD_DIM) if e else (lane < HEAD_DIM)
            ka_ref[2 * p + e] = jnp.where(own, k[:, ps], k_bias[:, ps]).astype(BF16)
        for b in range(tm // ATT_T):
            bs = slice(b * ATT_T, (b + 1) * ATT_T)
            q_t = q[bs, ps].T
            k_t = k[bs, ps].T
            v_t = v[bs, ps].T
            kt_ref[0, ps, bs] = k_t
            vtf_ref[0, ps, bs] = v_t
            for e in range(2):
                own = (row >= HEAD_DIM) if e else (row < HEAD_DIM)
                qt_ref[2 * p + e, b] = jnp.where(own, q_t, q_ones).astype(BF16)
                vt_ref[2 * p + e, b] = jnp.where(own, v_t, 1.0).astype(BF16)


def _qkv(x, g, w_qkv, wf, bf, qg, kg, seg, expand, swp, tri, cinit, *, tm):
    rows = x.shape[0]
    seq_rows = min(SEQ, rows)
    tps = seq_rows // tm
    row_spec = pl.BlockSpec((tm, D_MODEL), lambda i: (i, 0))
    kv_spec = pl.BlockSpec((1, D_MODEL, tm), lambda i: (i // tps, 0, i % tps))
    kv_shape = jax.ShapeDtypeStruct((rows // seq_rows, D_MODEL, seq_rows), F32)
    t_spec = pl.BlockSpec((N_HEADS, tm // ATT_T, LANES, ATT_T), lambda i: (0, i, 0, 0))
    t_shape = jax.ShapeDtypeStruct((N_HEADS, rows // ATT_T, LANES, ATT_T), BF16)
    return pl.pallas_call(
        functools.partial(_qkv_kernel, tiles_per_seq=tps),
        grid=(rows // tm,),
        in_specs=[row_spec, _const_spec((1, D_MODEL)), _const_spec((D_MODEL, 3 * D_MODEL)),
                  _const_spec((D_MODEL, LANES)), _const_spec((1, LANES)), _const_spec((1, D_MODEL)),
                  _const_spec((1, D_MODEL)), _const_spec((D_MODEL, LANES)), _const_spec((LANES, D_MODEL)),
                  _const_spec((LANES, D_MODEL)), _const_spec((tm, tm)), _const_spec((1, LANES))],
        out_specs=(kv_spec, kv_spec, t_spec,
                   pl.BlockSpec((N_HEADS, tm, LANES), lambda i: (0, i, 0)), t_spec,
                   pl.BlockSpec((tm, LANES), lambda i: (i, 0)),
                   pl.BlockSpec((LANES, tm), lambda i: (0, i))),
        out_shape=(kv_shape, kv_shape,
                   t_shape, jax.ShapeDtypeStruct((N_HEADS, rows, LANES), BF16), t_shape,
                   jax.ShapeDtypeStruct((rows, LANES), F32), jax.ShapeDtypeStruct((LANES, rows), F32)),
        scratch_shapes=[pltpu.VMEM((SUBLANES, LANES), F32)],
        compiler_params=_cparams(("arbitrary",)),
        name="qkv_proj",
    )(x, g, w_qkv, wf, bf, qg, kg, seg, expand, swp, tri, cinit)


def _softmax_step(s0, vt, cq, m, acc):
    m_new = jnp.maximum(m, jnp.max(s0, axis=0, keepdims=True) + cq)
    p = jnp.exp(s0 - (m_new - cq))
    acc = jnp.exp(m - m_new) * acc + _dot(vt, p.astype(BF16))
    return m_new, acc


def _attn_finish(acc_even, acc_odd):
    o_t = jnp.concatenate([acc_even[:HEAD_DIM] / acc_even[HEAD_DIM:],
                           acc_odd[HEAD_DIM:] / acc_odd[:HEAD_DIM]], axis=0)
    return o_t.T.astype(BF16)


def _page_copies(pt_ref, cache_refs, bufs, sem, it, slot):
    n_pages = pt_ref.shape[1]
    n_sub = n_pages // DEC_PPS
    b = it // n_sub
    t = it % n_sub
    copies = []
    for i in range(DEC_PPS):
        page = pt_ref[b, n_pages - 1 - (t * DEC_PPS + i)]
        for kind, (src, dst) in enumerate(zip(cache_refs, bufs)):
            copies.append(pltpu.make_async_copy(src.at[page], dst.at[slot, i], sem.at[slot, kind]))
    return copies


def _decode_substep(it, n_sub, qd_ref, low_ref, k_pages, v_pages, lf_pages, do_ref, dm_ref, dl_ref,
                    qc_scr, m_scr, l_scr, tail_scr, acc_scr):
    b = it // n_sub
    t = it % n_sub
    mine = lax.broadcasted_iota(jnp.int32, (1, LANES), 1) == b

    @pl.when(it == 0)
    def _():
        do_ref[...] = jnp.zeros_like(do_ref)
        dm_ref[...] = jnp.zeros_like(dm_ref)
        dl_ref[...] = jnp.zeros_like(dl_ref)

    @pl.when(t == 0)
    def _():
        qcol = jnp.sum(jnp.where(mine, qd_ref[...], 0.0), axis=1, keepdims=True)
        qc_scr[...] = jnp.broadcast_to(qcol, (D_MODEL, LANES))
        m_scr[...] = jnp.full(m_scr.shape, NEG_INF, F32)
        l_scr[...] = jnp.zeros_like(l_scr)
        tail_scr[...] = jnp.zeros_like(tail_scr)
        acc_scr[...] = jnp.zeros_like(acc_scr)

    low = low_ref[...]
    tail = tail_scr[...]
    scores = []
    tiles = [[] for _ in range(DEC_PPS)]
    for h in range(N_HEADS):
        tile = [None] * DEC_PPS
        for r in range(h * HEAD_DIM, (h + 1) * HEAD_DIM, SUBLANES):
            rows = slice(r, r + SUBLANES)
            q_rows = qc_scr[rows, :]
            for i in range(DEC_PPS):
                term = k_pages(i, rows) * q_rows
                tile[i] = term if tile[i] is None else tile[i] + term
        for i in range(DEC_PPS):
            tiles[i].append(tile[i])
    lf_all = jnp.concatenate([lf_pages(i) for i in range(DEC_PPS)], axis=0)
    later = _dot_split(_split3(lf_all), low)
    for i in range(DEC_PPS):
        hs = slice(i * N_HEADS, (i + 1) * N_HEADS)
        s = jnp.sum(jnp.concatenate(tiles[i], axis=0).reshape(N_HEADS, SUBLANES, LANES), axis=1)
        scores.append(s + tail + later[hs])
        tail = tail + jnp.sum(lf_all[hs], axis=1, keepdims=True)
    tail_scr[...] = tail
    m_old = m_scr[...]
    s_max = scores[0]
    for s in scores[1:]:
        s_max = jnp.maximum(s_max, s)
    m_new = jnp.maximum(m_old, jnp.max(s_max, axis=1, keepdims=True))
    alpha = jnp.exp(m_old - m_new)
    probs = [jnp.exp(s - m_new) for s in scores]
    p_sum = probs[0]
    for p in probs[1:]:
        p_sum = p_sum + p
    l_scr[...] = alpha * l_scr[...] + jnp.sum(p_sum, axis=1, keepdims=True)
    m_scr[...] = m_new
    for h in range(N_HEADS):
        rows = slice(h * HEAD_DIM, (h + 1) * HEAD_DIM)
        acc = acc_scr[rows, :] * alpha[h:h + 1, :]
        for i in range(DEC_PPS):
            acc = acc + v_pages(i, rows) * probs[i][h:h + 1, :]
        acc_scr[rows, :] = acc

    @pl.when(t == n_sub - 1)
    def _():
        do_ref[...] = jnp.where(mine, jnp.sum(acc_scr[...], axis=1, keepdims=True), do_ref[...])
        dm_ref[...] = jnp.where(mine, m_scr[...], dm_ref[...])
        dl_ref[...] = jnp.where(mine, l_scr[...], dl_ref[...])


def _attn_kernel(pt_ref, qt_ref, ka_ref, vt_ref, ct_ref, kam_ref, vtm_ref, qd_ref, low_ref,
                 ck_hbm, cv_hbm, clf_hbm, o_ref, do_ref, dm_ref, dl_ref,
                 kbuf, vbuf, lfbuf, sem, qc_scr, m_scr, l_scr, tail_scr, acc_scr):
    key = lax.broadcasted_iota(jnp.int32, (ATT_T, ATT_T), 0)
    qry = lax.broadcasted_iota(jnp.int32, (ATT_T, ATT_T), 1)

    n_qblk = SEQ // ATT_T
    n_sub = pt_ref.shape[1] // DEC_PPS
    step0 = (pl.program_id(0) * pl.num_programs(1) + pl.program_id(1)) * n_qblk
    total = pl.num_programs(0) * pl.num_programs(1) * n_qblk
    copies = functools.partial(_page_copies, pt_ref, (ck_hbm, cv_hbm, clf_hbm), (kbuf, vbuf, lfbuf), sem)

    @pl.when(step0 == 0)
    def _():
        for c in copies(0, 0):
            c.start()

    def qblock(qi, _):
        it = step0 + qi
        slot = it % 2

        @pl.when(it + 1 < total)
        def _():
            for c in copies(it + 1, 1 - slot):
                c.start()

        for c in copies(it, slot):
            c.wait()
        _decode_substep(it, n_sub, qd_ref, low_ref,
                        lambda i, rows: kbuf[slot, i, rows, :], lambda i, rows: vbuf[slot, i, rows, :],
                        lambda i: lfbuf[slot, i],
                        do_ref, dm_ref, dl_ref, qc_scr, m_scr, l_scr, tail_scr, acc_scr)

        r0 = pl.multiple_of(qi * ATT_T, ATT_T)
        qts = [qt_ref[h, qi] for h in range(ATT_HPS)]
        cqs = [ct_ref[0, h // 2, qi][h % 2:h % 2 + 1, :] for h in range(ATT_HPS)]

        def scores(h, blk):
            c0 = pl.multiple_of(blk * ATT_T, ATT_T)
            return _dot(ka_ref[h, pl.ds(c0, ATT_T), :], qts[h])

        state = []
        for h in range(ATT_HPS):
            state += [jnp.full((1, ATT_T), NEG_INF, F32), jnp.zeros((LANES, ATT_T), F32), scores(h, 0)]

        def kvstep(j, st):
            new = []
            for h in range(ATT_HPS):
                m, acc, s_cur = st[3 * h:3 * h + 3]
                s_next = scores(h, j + 1)
                m, acc = _softmax_step(s_cur, vt_ref[h, j], cqs[h], m, acc)
                new += [m, acc, s_next]
            return tuple(new)

        st = lax.fori_loop(0, qi, kvstep, tuple(state))
        accs = []
        for h in range(ATT_HPS):
            m, acc, s_diag = st[3 * h:3 * h + 3]
            s_diag = jnp.where(key <= qry, s_diag, NEG_INF)
            s_meta = _dot(kam_ref[h], qts[h])
            top = jnp.maximum(jnp.max(s_diag, axis=0, keepdims=True), jnp.max(s_meta, axis=0, keepdims=True))
            m_new = jnp.maximum(m, top + cqs[h])
            shift = m_new - cqs[h]
            p_diag = jnp.exp(s_diag - shift).astype(BF16)
            p_meta = jnp.exp(s_meta - shift).astype(BF16)
            accs.append(jnp.exp(m - m_new) * acc + _dot(vt_ref[h, qi], p_diag) + _dot(vtm_ref[h], p_meta))
        for p in range(ATT_HPS // 2):
            o_ref[p, pl.ds(r0, ATT_T), :] = _attn_finish(accs[2 * p], accs[2 * p + 1])
        return 0

    lax.fori_loop(0, n_qblk, qblock, 0)


def _attn(qt, ka, vt, ct, kam, vtm, page_table, qt_dec, low, cache_kt, cache_vt, cache_lft):
    nblk = SEQ // ATT_T
    hps = ATT_HPS
    n_groups = N_HEADS // hps
    n_pages = page_table.shape[1]
    assert N_BATCH * n_groups * nblk == N_SAMPLE * (n_pages // DEC_PPS)
    t_spec = pl.BlockSpec((hps, nblk, LANES, ATT_T), lambda n, g, pt: (g, n, 0, 0))
    const = lambda shape: pl.BlockSpec(shape, lambda n, g, pt: (0,) * len(shape))
    hbm = pl.BlockSpec(memory_space=pl.ANY)
    stat = lambda: pltpu.VMEM((N_HEADS, LANES), F32)
    grid_spec = pltpu.PrefetchScalarGridSpec(
        num_scalar_prefetch=1,
        grid=(N_BATCH, n_groups),
        in_specs=[t_spec,
                  pl.BlockSpec((hps, SEQ, LANES), lambda n, g, pt: (g, n, 0)),
                  t_spec,
                  pl.BlockSpec((1, hps // 2, nblk, SUBLANES, ATT_T), lambda n, g, pt: (n, g, 0, 0, 0)),
                  pl.BlockSpec((hps, N_META, LANES), lambda n, g, pt: (g, 0, 0)),
                  pl.BlockSpec((hps, LANES, N_META), lambda n, g, pt: (g, 0, 0)),
                  const((D_MODEL, N_SAMPLE)), const((PAGE, PAGE)), hbm, hbm, hbm],
        out_specs=(pl.BlockSpec((hps // 2, SEQ, LANES), lambda n, g, pt: (g, n, 0)),
                   const((D_MODEL, N_SAMPLE)), const((N_HEADS, N_SAMPLE)), const((N_HEADS, N_SAMPLE))),
        scratch_shapes=[pltpu.VMEM((2, DEC_PPS, D_MODEL, PAGE), F32),
                        pltpu.VMEM((2, DEC_PPS, D_MODEL, PAGE), F32),
                        pltpu.VMEM((2, DEC_PPS, N_HEADS, PAGE), F32),
                        pltpu.SemaphoreType.DMA((2, 3)),
                        pltpu.VMEM((D_MODEL, LANES), F32), stat(), stat(), stat(),
                        pltpu.VMEM((D_MODEL, LANES), F32)],
    )
    return pl.pallas_call(
        _attn_kernel,
        grid_spec=grid_spec,
        out_shape=(jax.ShapeDtypeStruct((N_PAIRS, N_BATCH * SEQ, LANES), BF16),
                   jax.ShapeDtypeStruct((D_MODEL, N_SAMPLE), F32),
                   jax.ShapeDtypeStruct((N_HEADS, N_SAMPLE), F32),
                   jax.ShapeDtypeStruct((N_HEADS, N_SAMPLE), F32)),
        compiler_params=_cparams(("arbitrary", "arbitrary")),
        name="attn_prompt_decode",
    )(page_table, qt, ka, vt, ct, kam, vtm, qt_dec, low, cache_kt, cache_vt, cache_lft)


def _meta_attn_kernel(qtm_ref, kam_ref, vtm_ref, cmt_ref, o_ref):
    key = lax.broadcasted_iota(jnp.int32, (N_META, N_META), 0)
    qry = lax.broadcasted_iota(jnp.int32, (N_META, N_META), 1)
    for p in range(N_PAIRS):
        accs = []
        for e in range(2):
            h = 2 * p + e
            cq = cmt_ref[p, e:e + 1, :]
            s0 = jnp.where(key <= qry, _dot(kam_ref[h], qtm_ref[h]), NEG_INF)
            m = jnp.max(s0, axis=0, keepdims=True) + cq
            pr = jnp.exp(s0 - (m - cq))
            accs.append(_dot(vtm_ref[h], pr.astype(BF16)))
        o_ref[p] = _attn_finish(*accs)


def _meta_attention(qtm, kam, vtm, cmt):
    pm = (N_PAIRS, N_META, LANES)
    t_shape = (N_HEADS, LANES, N_META)
    o = pl.pallas_call(
        _meta_attn_kernel,
        grid=(1,),
        in_specs=[_full_spec(t_shape), _full_spec((N_HEADS, N_META, LANES)), _full_spec(t_shape),
                  _full_spec((N_PAIRS, SUBLANES, N_META))],
        out_specs=_full_spec(pm),
        out_shape=jax.ShapeDtypeStruct(pm, BF16),
        name="attn_meta",
    )(qtm, kam, vtm, cmt)
    return _pairs_to_rows(o)


def _decode_combine_kernel(acc_ref, m_ref, l_ref, q_ref, kn_ref, vn_ref, lfn_ref, seg_ref, exp_ref, o_ref):
    expand = exp_ref[...]
    ex = lambda x: _dot_split(_split2(x), expand)
    s_new = _dot_split(_split2(q_ref[...].astype(F32) * kn_ref[...]), seg_ref[...])
    m_c = m_ref[...] + lfn_ref[...]
    m_all = jnp.maximum(m_c, s_new)
    w_c = jnp.exp(m_c - m_all)
    w_n = jnp.exp(s_new - m_all)
    den = l_ref[...] * w_c + w_n
    o = (acc_ref[...] * ex(w_c) + vn_ref[...] * ex(w_n)) / ex(den)
    o_ref[...] = o.astype(BF16)


def _decode_combine(acc, m, l, q, kn, vn, lfn, seg, expand):
    shapes = [(N_SAMPLE, D_MODEL), (N_SAMPLE, LANES), (N_SAMPLE, LANES), (N_SAMPLE, D_MODEL),
              (N_SAMPLE, D_MODEL), (N_SAMPLE, D_MODEL), (N_SAMPLE, LANES), (D_MODEL, LANES), (LANES, D_MODEL)]
    return pl.pallas_call(
        _decode_combine_kernel,
        grid=(1,),
        in_specs=[_full_spec(s) for s in shapes],
        out_specs=_full_spec((N_SAMPLE, D_MODEL)),
        out_shape=jax.ShapeDtypeStruct((N_SAMPLE, D_MODEL), BF16),
        name="attn_decode_combine",
    )(acc, m, l, q, kn, vn, lfn, seg, expand)


def _pairs_to_rows(xp):
    return jnp.transpose(xp, (1, 0, 2)).reshape(xp.shape[1], D_MODEL)


def kernel(x_prompt, x_sample, state_ssm_re, state_ssm_im, cache_k, cache_v, cache_logf, state_conv, page_table, meta_tokens, norm_mix_g, norm_ffn_g, ssm_a_re, ssm_a_im, ssm_b_re, ssm_b_im, ssm_c_re, ssm_c_im, ssm_d, ssm_log_step, ssm_w_glu, attn_w_qkv, attn_w_f, attn_b_f, attn_q_g, attn_k_g, attn_w_o, ffn_w_up, ffn_w_gate, ffn_conv_w, ffn_conv_b, ffn_w_down):
    n_body = N_BATCH * SEQ
    xb = x_prompt.reshape(n_body, D_MODEL)
    pad_rows = SMALL_ROWS - N_SAMPLE - N_META
    xs = jnp.concatenate([x_sample.reshape(N_SAMPLE, D_MODEL), meta_tokens.astype(F32),
                          jnp.zeros((pad_rows, D_MODEL), F32)], axis=0)
    meta_rows = slice(META_ROW0, META_ROW0 + N_META)

    g_mix = norm_mix_g.astype(F32).reshape(2, 1, D_MODEL)
    g_ffn = norm_ffn_g.astype(F32).reshape(2, 1, D_MODEL)
    w_glu = ssm_w_glu.astype(BF16)
    w_up = ffn_w_up.astype(BF16)
    w_gate = ffn_w_gate.astype(BF16)
    w_down = ffn_w_down.astype(BF16)
    conv_w = jnp.pad(ffn_conv_w.astype(F32), ((0, 0), (0, SUBLANES - CONV_W), (0, 0)))
    conv_b = ffn_conv_b.astype(F32).reshape(2, 1, D_FF)

    s5w = _s5_weights(ssm_a_re, ssm_a_im, ssm_b_re, ssm_b_im, ssm_c_re, ssm_c_im, ssm_d, ssm_log_step)
    ub = _norm_shuffle(xb, g_mix[0])
    u_small = _norm_rows(xs, g_mix[0])
    n_mc = N_META // S5_CHUNK
    um = u_small[meta_rows].reshape(n_mc, S5_CHUNK, S5_SETS, LANES)
    um = jnp.transpose(um, (2, 0, 1, 3)).reshape(S5_SETS, n_mc, 1, S5_CHUNK * LANES)
    um = jnp.broadcast_to(um, (S5_SETS, n_mc, N_BATCH, S5_CHUNK * LANES)).reshape(S5_SETS, N_META, S5_CHUNK * LANES)
    yb, ym, ys, hpr, hpi, hsr, hsi = _s5(
        ub, um, u_small, state_ssm_re.astype(F32).reshape(N_SAMPLE, -1),
        state_ssm_im.astype(F32).reshape(N_SAMPLE, -1), s5w)
    x1b = _glu_res(xb, yb, w_glu)
    ym_tok = ym.reshape(S5_SETS, n_mc, N_BATCH, S5_CHUNK, LANES)[:, :, 0]
    ym_tok = jnp.transpose(ym_tok, (1, 2, 0, 3)).reshape(N_META, D_MODEL)
    y_small = jnp.concatenate([ys, ym_tok, jnp.zeros((pad_rows, D_MODEL), BF16)], axis=0)
    x1s = _glu_res_rows(xs, y_small, w_glu)

    def ffn_layer(i, xb_in, xs_in, ab, a_small, wa):
        sc = state_conv[i].astype(F32)
        pad = ((0, SMALL_ROWS - N_SAMPLE), (0, 0))
        xs_out, h_small = _ffn_small(xs_in, a_small, wa, g_ffn[i], w_up[i], w_gate[i], conv_w[i], conv_b[i],
                                     w_down[i], jnp.pad(sc[:, 0], pad), jnp.pad(sc[:, 1], pad))
        init = jnp.concatenate([jnp.zeros((SUBLANES - 2, D_FF), F32),
                                h_small[META_ROW0 + N_META - 2:META_ROW0 + N_META]], axis=0)
        xb_out, hl = _ffn(xb_in, ab, wa, g_ffn[i], w_up[i], w_gate[i], conv_w[i], conv_b[i], w_down[i], init)
        tps = SEQ // FFN_TM
        conv_p = hl.reshape(N_BATCH, tps, SUBLANES, D_FF)[:, tps - 1, SUBLANES - 2:, :]
        conv_s = jnp.stack([sc[:, 1], h_small[:N_SAMPLE]], axis=1)
        return xb_out, xs_out, conv_p, conv_s

    x2b, x2s, conv_p0, conv_s0 = ffn_layer(0, x1b, x1s, None, None, None)

    w_qkv = attn_w_qkv.astype(BF16)
    wf = jnp.pad(attn_w_f.astype(BF16), ((0, 0), (0, LANES - N_HEADS)))
    bf = jnp.pad(attn_b_f.astype(F32), (0, LANES - N_HEADS)).reshape(1, LANES)
    qg = (jnp.tile(attn_q_g.astype(F32), N_HEADS) * (HEAD_DIM ** -0.5)).reshape(1, D_MODEL)
    kg = jnp.tile(attn_k_g.astype(F32), N_HEADS).reshape(1, D_MODEL)
    head_of_lane = jnp.arange(D_MODEL) // HEAD_DIM
    onehot = (head_of_lane[:, None] == jnp.arange(LANES)[None, :])
    seg = (onehot.astype(F32) / HEAD_DIM).astype(BF16)
    expand = onehot.T.astype(BF16)
    tri_body = (jnp.arange(FFN_TM)[:, None] >= jnp.arange(FFN_TM)[None, :]).astype(BF16)
    r = jnp.arange(SMALL_ROWS)
    tri_small = ((r[:, None] >= r[None, :]) & (r[None, :] >= META_ROW0) & (r[:, None] < META_ROW0 + N_META))
    tri_small = tri_small.astype(BF16)

    swp = expand.reshape(LANES // 2, 2, D_MODEL)[:, ::-1].reshape(LANES, D_MODEL)

    kts_f32, vts_f32, qts, kas, vts, lfs, cts = _qkv(x2s, g_mix[1], w_qkv, wf, bf, qg, kg, seg, expand, swp,
                                                     tri_small, jnp.zeros((1, LANES), F32), tm=SMALL_ROWS)
    c_meta_total = cts[:, META_ROW0 + N_META - 1].reshape(1, LANES)
    ktb_f32, vtb_f32, qtb, kab, vtb, lfb, ctb = _qkv(x2b, g_mix[1], w_qkv, wf, bf, qg, kg, seg, expand, swp,
                                                     tri_body, c_meta_total, tm=FFN_TM)
    kts_f32, vts_f32 = kts_f32[0], vts_f32[0]

    nblk = SEQ // ATT_T
    ct = ctb[:N_HEADS].reshape(N_PAIRS, 2, N_BATCH, nblk, ATT_T)
    ct = jnp.transpose(ct, (2, 0, 3, 1, 4))
    ct = jnp.pad(ct, ((0, 0), (0, 0), (0, 0), (0, SUBLANES - 2), (0, 0)))
    cmt = cts[:N_HEADS, meta_rows].reshape(N_PAIRS, 2, N_META)
    cmt = jnp.pad(cmt, ((0, 0), (0, SUBLANES - 2), (0, 0)))
    qtm = qts[:, 0, :, meta_rows]
    kam = kas[:, meta_rows]
    vtm = vts[:, 0, :, meta_rows]
    o_meta = _meta_attention(qtm, kam, vtm, cmt)

    n_pool = cache_k.shape[0]
    cache_kt = jnp.transpose(cache_k, (0, 2, 3, 1)).reshape(n_pool, D_MODEL, PAGE)
    cache_vt = jnp.transpose(cache_v, (0, 2, 3, 1)).reshape(n_pool, D_MODEL, PAGE)
    cache_lft = jnp.transpose(cache_logf.astype(F32), (0, 2, 1))
    qt_pairs = qts[:, 0, :, :N_SAMPLE].reshape(N_PAIRS, 2, 2, HEAD_DIM, N_SAMPLE)
    qt_dec = jnp.stack([qt_pairs[:, 0, 0], qt_pairs[:, 1, 1]], axis=1).reshape(D_MODEL, N_SAMPLE)
    low = (jnp.arange(PAGE)[:, None] > jnp.arange(PAGE)[None, :]).astype(BF16)
    ob, acc_t, m_t, l_t = _attn(qtb, kab, vtb, ct, kam, vtm, page_table.astype(jnp.int32), qt_dec.astype(F32), low,
                                cache_kt, cache_vt, cache_lft)
    pad_heads = lambda x, v: jnp.pad(x.T, ((0, 0), (0, LANES - N_HEADS)), constant_values=v)
    o_dec = _decode_combine(acc_t.T, pad_heads(m_t, 0.0), pad_heads(l_t, 1.0), qt_dec.T, kts_f32[:, :N_SAMPLE].T,
                            vts_f32[:, :N_SAMPLE].T, lfs[:N_SAMPLE], (onehot.astype(BF16)), expand)
    o_small = jnp.concatenate([o_dec, o_meta, jnp.zeros((pad_rows, D_MODEL), BF16)], axis=0)

    w_o = attn_w_o.astype(BF16)
    x4b, x4s, conv_p1, conv_s1 = ffn_layer(1, x2b, x2s, ob, o_small, w_o)

    def kv_with_meta(small_t, body_t):
        m = jnp.broadcast_to(small_t[None, :, meta_rows], (N_BATCH, D_MODEL, N_META))
        full = jnp.concatenate([m, body_t], axis=2).reshape(N_BATCH, N_HEADS, HEAD_DIM, SEQ + N_META)
        return jnp.transpose(full, (0, 3, 1, 2))

    def head_cols(x_t):
        return jnp.transpose(x_t[:, :N_SAMPLE].reshape(N_HEADS, HEAD_DIM, N_SAMPLE), (2, 0, 1))[:, None]

    k_p = kv_with_meta(kts_f32, ktb_f32)
    v_p = kv_with_meta(vts_f32, vtb_f32)
    lf_meta = jnp.broadcast_to(lfs[meta_rows, :N_HEADS][None], (N_BATCH, N_META, N_HEADS))
    lf_p = jnp.concatenate([lf_meta, lfb[:, :N_HEADS].reshape(N_BATCH, SEQ, N_HEADS)], axis=1)
    return (x4b.reshape(N_BATCH, SEQ, D_MODEL),
            x4s[:N_SAMPLE].reshape(N_SAMPLE, 1, D_MODEL),
            hpr.reshape(N_BATCH, N_GROUPS, STATE_P), hpi.reshape(N_BATCH, N_GROUPS, STATE_P),
            hsr.reshape(N_SAMPLE, N_GROUPS, STATE_P), hsi.reshape(N_SAMPLE, N_GROUPS, STATE_P),
            k_p, v_p, lf_p,
            head_cols(kts_f32), head_cols(vts_f32),
            lfs[:N_SAMPLE, :N_HEADS].reshape(N_SAMPLE, 1, N_HEADS),
            jnp.stack([conv_p0, conv_p1]), jnp.stack([conv_s0, conv_s1]))
```

```python
import functools

import jax
import jax.numpy as jnp
from jax import lax
from jax.experimental import pallas as pl
from jax.experimental.pallas import tpu as pltpu

F32 = jnp.float32
BF16 = jnp.bfloat16

D_MODEL = 1024
N_BATCH = 8
SEQ = 2048
N_SAMPLE = 128
N_META = 16
SMALL_ROWS = 256
META_ROW0 = 128
N_GROUPS = 64
STATE_P = 64
SSM_GROUP = 16
N_HEADS = 16
HEAD_DIM = 64
N_PAIRS = N_HEADS // 2
D_FF = 2816
CONV_W = 3
PAGE = 128
EPS = 1e-6
NEG_INF = -1e30

LANES = 128
SUBLANES = 8
S5_CHUNK = 8
S5_SETS = D_MODEL // LANES
SET_STATE = (LANES // SSM_GROUP) * STATE_P
N_CHUNKS = SEQ // S5_CHUNK
VMEM_LIMIT = 56 * 1024 * 1024


def _cparams(sem):
    return pltpu.CompilerParams(dimension_semantics=sem, vmem_limit_bytes=VMEM_LIMIT)


def _const_spec(shape):
    nd = len(shape)
    return pl.BlockSpec(shape, lambda *_: (0,) * nd, pipeline_mode=pl.Buffered(1))


def _full_spec(shape):
    nd = len(shape)
    return pl.BlockSpec(shape, lambda *_: (0,) * nd)


def _dot(a, b):
    return jnp.dot(a, b, preferred_element_type=F32)


def _dot_f32(a, b):
    return jnp.dot(a, b, preferred_element_type=F32, precision=lax.Precision.HIGHEST)


def _split2(x):
    hi = x.astype(BF16)
    lo = (x - hi.astype(F32)).astype(BF16)
    return hi, lo


def _split3(x):
    hi = x.astype(BF16)
    r = x - hi.astype(F32)
    mid = r.astype(BF16)
    lo = (r - mid.astype(F32)).astype(BF16)
    return hi, mid, lo


def _dot_split(parts, w):
    acc = _dot(parts[0], w)
    for p in parts[1:]:
        acc = acc + _dot(p, w)
    return acc


def _rmsnorm(x, g):
    ms = jnp.mean(x * x, axis=-1, keepdims=True)
    return x * lax.rsqrt(ms + EPS) * g


def _gelu_tanh(x):
    c = 0.7978845608028654
    return 0.5 * x * (1.0 + jnp.tanh(c * (x + 0.044715 * (x * x * x))))


def _sigmoid(x):
    return 1.0 / (1.0 + jnp.exp(-x))


def _norm_shuffle_kernel(x_ref, g_ref, o_ref, scr):
    rb = o_ref.shape[1]
    u = _rmsnorm(x_ref[...], g_ref[...])
    for j in range(S5_SETS):
        scr[j] = u[:, j * LANES:(j + 1) * LANES]
    for j in range(S5_SETS):
        for s in range(S5_CHUNK):
            o_ref[j, :, s * LANES:(s + 1) * LANES] = scr[j, pl.ds(s, rb, stride=S5_CHUNK), :].astype(BF16)


def _norm_shuffle(x, g):
    rows = x.shape[0] // S5_CHUNK
    rb = 128
    return pl.pallas_call(
        _norm_shuffle_kernel,
        grid=(rows // rb,),
        in_specs=[pl.BlockSpec((rb * S5_CHUNK, D_MODEL), lambda i: (i, 0)),
                  _const_spec((1, D_MODEL))],
        out_specs=pl.BlockSpec((S5_SETS, rb, S5_CHUNK * LANES), lambda i: (0, i, 0)),
        out_shape=jax.ShapeDtypeStruct((S5_SETS, rows, S5_CHUNK * LANES), BF16),
        scratch_shapes=[pltpu.VMEM((S5_SETS, rb * S5_CHUNK, LANES), F32)],
        compiler_params=_cparams(("parallel",)),
        name="norm_shuffle",
    )(x, g)


def _norm_rows_kernel(x_ref, g_ref, o_ref):
    o_ref[...] = _rmsnorm(x_ref[...], g_ref[...]).astype(BF16)


def _norm_rows(x, g):
    rows = x.shape[0]
    return pl.pallas_call(
        _norm_rows_kernel,
        grid=(1,),
        in_specs=[_full_spec((rows, D_MODEL)), _full_spec((1, D_MODEL))],
        out_specs=_full_spec((rows, D_MODEL)),
        out_shape=jax.ShapeDtypeStruct((rows, D_MODEL), BF16),
        name="norm_rows",
    )(x, g)


S5_ROW_BLOCK = 512
N_SLABS = SET_STATE // LANES
SCAN_PITCH = N_CHUNKS + SUBLANES


def _s5_kernel(ub_ref, um_ref, us_ref, h0r_ref, h0i_ref, wt_ref, wb_ref, wc_ref, a8_ref, dt_ref,
               wb1_ref, wc1_ref, a1_ref, d1_ref,
               yb_ref, ym_ref, ys_ref, hpr_ref, hpi_ref, hsr_ref, hsi_ref, xr_scr, xi_scr):
    rows = ub_ref.shape[1]
    wb = wb_ref[0]
    wt = wt_ref[0]
    wc = wc_ref[0]
    dt = dt_ref[0]

    seq_per_block = S5_ROW_BLOCK // N_CHUNKS

    def scratch_rows(n):
        return slice(n * SCAN_PITCH, n * SCAN_PITCH + N_CHUNKS)

    for r in range(rows // S5_ROW_BLOCK):
        x = _dot(ub_ref[0, r * S5_ROW_BLOCK:(r + 1) * S5_ROW_BLOCK, :], wb)
        for q in range(seq_per_block):
            xs = x[q * N_CHUNKS:(q + 1) * N_CHUNKS]
            ss = scratch_rows(r * seq_per_block + q)
            for k in range(N_SLABS):
                xr_scr[k, ss, :] = xs[:, k * LANES:(k + 1) * LANES]
                xi_scr[k, ss, :] = xs[:, SET_STATE + k * LANES:SET_STATE + (k + 1) * LANES]
    xm = _dot(um_ref[0], wb)

    a8r = jnp.broadcast_to(a8_ref[0, 0:1, :], (SUBLANES, SET_STATE))
    a8i = jnp.broadcast_to(a8_ref[0, 1:2, :], (SUBLANES, SET_STATE))

    hr = jnp.zeros((SUBLANES, SET_STATE), F32)
    hi = jnp.zeros((SUBLANES, SET_STATE), F32)
    hm_prev = []
    for cm in range(N_META // S5_CHUNK):
        hm_prev.append(jnp.concatenate([hr, hi], axis=1))
        xr = xm[cm * SUBLANES:(cm + 1) * SUBLANES, :SET_STATE]
        xi = xm[cm * SUBLANES:(cm + 1) * SUBLANES, SET_STATE:]
        hr, hi = a8r * hr - a8i * hi + xr, a8r * hi + a8i * hr + xi

    ar = [a8r[:, k * LANES:(k + 1) * LANES] for k in range(N_SLABS)]
    ai = [a8i[:, k * LANES:(k + 1) * LANES] for k in range(N_SLABS)]

    def step(c, carry):
        hrs, his = carry
        nr, ni = [], []
        for k in range(N_SLABS):
            idx = (k, pl.ds(c, N_BATCH, stride=SCAN_PITCH), slice(None))
            xr = xr_scr[idx]
            xi = xi_scr[idx]
            xr_scr[idx] = hrs[k]
            xi_scr[idx] = his[k]
            nr.append(ar[k] * hrs[k] - ai[k] * his[k] + xr)
            ni.append(ar[k] * his[k] + ai[k] * hrs[k] + xi)
        return tuple(nr), tuple(ni)

    init = (tuple(hr[:, k * LANES:(k + 1) * LANES] for k in range(N_SLABS)),
            tuple(hi[:, k * LANES:(k + 1) * LANES] for k in range(N_SLABS)))
    hrs, his = lax.fori_loop(0, N_CHUNKS, step, init)
    hpr_ref[...] = jnp.concatenate(hrs, axis=1)
    hpi_ref[...] = jnp.concatenate(his, axis=1)

    def emit(u, hprev):
        blk = 2 * LANES
        y_in = jnp.concatenate([_dot(u[:, :c + blk], wt[:c + blk, c:c + blk])
                                for c in range(0, S5_CHUNK * LANES, blk)], axis=1)
        y = y_in + _dot(hprev.astype(BF16), wc) + dt * u.astype(F32)
        return _gelu_tanh(y).astype(BF16)

    for r in range(rows // S5_ROW_BLOCK):
        rs = slice(r * S5_ROW_BLOCK, (r + 1) * S5_ROW_BLOCK)
        hprev = jnp.concatenate(
            [jnp.concatenate([xr_scr[k, scratch_rows(r * seq_per_block + q), :] for k in range(N_SLABS)]
                             + [xi_scr[k, scratch_rows(r * seq_per_block + q), :] for k in range(N_SLABS)], axis=1)
             for q in range(seq_per_block)], axis=0)
        yb_ref[0, rs, :] = emit(ub_ref[0, rs, :], hprev)
    ym_ref[0] = emit(um_ref[0], jnp.concatenate(hm_prev, axis=0))

    us = us_ref[...]
    xs = _dot(us, wb1_ref[0])
    a1r = a1_ref[0, 0:1, :]
    a1i = a1_ref[0, 1:2, :]
    h0r = h0r_ref[...]
    h0i = h0i_ref[...]
    hsr = a1r * h0r - a1i * h0i + xs[:, :SET_STATE]
    hsi = a1r * h0i + a1i * h0r + xs[:, SET_STATE:]
    hsr_ref[...] = hsr
    hsi_ref[...] = hsi
    hs = jnp.concatenate([hsr, hsi], axis=1).astype(BF16)
    ys = _dot(hs, wc1_ref[0]) + d1_ref[0] * us.astype(F32)
    ys_ref[...] = _gelu_tanh(ys).astype(BF16)


def _set_spec(shape):
    return pl.BlockSpec((1,) + shape, lambda j: (j,) + (0,) * len(shape))


def _s5(ub, um, u_small, h0r, h0i, w):
    rows = ub.shape[1]
    cw = S5_CHUNK * LANES
    out_shapes = (
        jax.ShapeDtypeStruct((S5_SETS, rows, cw), BF16),
        jax.ShapeDtypeStruct((S5_SETS, N_META, cw), BF16),
        jax.ShapeDtypeStruct((N_SAMPLE, D_MODEL), BF16),
        jax.ShapeDtypeStruct((N_BATCH, S5_SETS * SET_STATE), F32),
        jax.ShapeDtypeStruct((N_BATCH, S5_SETS * SET_STATE), F32),
        jax.ShapeDtypeStruct((N_SAMPLE, S5_SETS * SET_STATE), F32),
        jax.ShapeDtypeStruct((N_SAMPLE, S5_SETS * SET_STATE), F32),
    )
    return pl.pallas_call(
        _s5_kernel,
        grid=(S5_SETS,),
        in_specs=[
            _set_spec((rows, cw)),
            _set_spec((N_META, cw)),
            pl.BlockSpec((N_SAMPLE, LANES), lambda j: (0, j)),
            pl.BlockSpec((N_SAMPLE, SET_STATE), lambda j: (0, j)),
            pl.BlockSpec((N_SAMPLE, SET_STATE), lambda j: (0, j)),
            _set_spec((cw, cw)),
            _set_spec((cw, 2 * SET_STATE)),
            _set_spec((2 * SET_STATE, cw)),
            _set_spec((2, SET_STATE)),
            _set_spec((1, cw)),
            _set_spec((LANES, 2 * SET_STATE)),
            _set_spec((2 * SET_STATE, LANES)),
            _set_spec((2, SET_STATE)),
            _set_spec((1, LANES)),
        ],
        out_specs=(
            _set_spec((rows, cw)),
            _set_spec((N_META, cw)),
            pl.BlockSpec((N_SAMPLE, LANES), lambda j: (0, j)),
            pl.BlockSpec((N_BATCH, SET_STATE), lambda j: (0, j)),
            pl.BlockSpec((N_BATCH, SET_STATE), lambda j: (0, j)),
            pl.BlockSpec((N_SAMPLE, SET_STATE), lambda j: (0, j)),
            pl.BlockSpec((N_SAMPLE, SET_STATE), lambda j: (0, j)),
        ),
        out_shape=out_shapes,
        scratch_shapes=[pltpu.VMEM((N_SLABS, N_BATCH * SCAN_PITCH, LANES), F32),
                        pltpu.VMEM((N_SLABS, N_BATCH * SCAN_PITCH, LANES), F32)],
        compiler_params=_cparams(("parallel",)),
        name="s5_mixer",
    )(ub, um, u_small, h0r, h0i, w["wt"], w["wb"], w["wc"], w["a8"], w["dt"],
      w["wb1"], w["wc1"], w["a1"], w["d1"])


def _s5_discretize(a_re, a_im, log_step):
    dt = jnp.exp(log_step)
    mag = jnp.exp(a_re * dt)
    ab_re = mag * jnp.cos(a_im * dt)
    ab_im = mag * jnp.sin(a_im * dt)
    den = a_re * a_re + a_im * a_im
    n_re = ab_re - 1.0
    n_im = ab_im
    f_re = (n_re * a_re + n_im * a_im) / den
    f_im = (n_im * a_re - n_re * a_im) / den
    return ab_re, ab_im, f_re, f_im


def _s5_weights_kernel(arow_ref, acr_ref, aci_ref, acl_ref, br_ref, bi_ref, cr_ref, ci_ref,
                       wt_ref, wb_ref, wc_ref, a8_ref, wb1_ref, wc1_ref, a1_ref):
    arow = arow_ref[0]
    abr, abi, fr, fi = _s5_discretize(arow[0:1], arow[1:2], arow[2:3])
    abr_c, abi_c, _, _ = _s5_discretize(acr_ref[0], aci_ref[0], acl_ref[0])
    br = br_ref[0]
    bi = bi_ref[0]
    b0r = fr * br - fi * bi
    b0i = fr * bi + fi * br
    cr = cr_ref[0]
    ci = ci_ref[0]
    wt_ref[0] = jnp.zeros(wt_ref.shape[1:], BF16)
    pr = jnp.ones_like(abr)
    pi = jnp.zeros_like(abr)
    pcr, pci = abr_c, abi_c
    for k in range(S5_CHUNK):
        bkr = pr * b0r - pi * b0i
        bki = pr * b0i + pi * b0r
        s = S5_CHUNK - 1 - k
        wb_ref[0, s * LANES:(s + 1) * LANES, :SET_STATE] = bkr.astype(BF16)
        wb_ref[0, s * LANES:(s + 1) * LANES, SET_STATE:] = bki.astype(BF16)
        kern = (_dot_f32(bkr, cr) - _dot_f32(bki, ci)).astype(BF16)
        for s2 in range(S5_CHUNK - k):
            t2 = s2 + k
            wt_ref[0, s2 * LANES:(s2 + 1) * LANES, t2 * LANES:(t2 + 1) * LANES] = kern
        wc_ref[0, :SET_STATE, k * LANES:(k + 1) * LANES] = (pcr * cr - pci * ci).astype(BF16)
        wc_ref[0, SET_STATE:, k * LANES:(k + 1) * LANES] = (-(pcr * ci + pci * cr)).astype(BF16)
        pr, pi = pr * abr - pi * abi, pr * abi + pi * abr
        pcr, pci = pcr * abr_c - pci * abi_c, pcr * abi_c + pci * abr_c
    a8_ref[0, 0:1, :] = pr
    a8_ref[0, 1:2, :] = pi
    a1_ref[0, 0:1, :] = abr
    a1_ref[0, 1:2, :] = abi
    wb1_ref[0, :, :SET_STATE] = b0r.astype(BF16)
    wb1_ref[0, :, SET_STATE:] = b0i.astype(BF16)
    wc1_ref[0, :SET_STATE, :] = cr.astype(BF16)
    wc1_ref[0, SET_STATE:, :] = (-ci).astype(BF16)


def _s5_weights(a_re, a_im, b_re, b_im, c_re, c_im, d_skip, log_step):
    gs = LANES // SSM_GROUP
    cw = S5_CHUNK * LANES
    eye = jnp.eye(gs, dtype=F32)
    per_state = lambda v: v.astype(F32).reshape(S5_SETS, SET_STATE)
    a_re_s, a_im_s = per_state(a_re), per_state(a_im)
    ls_s = per_state(jnp.broadcast_to(log_step[:, None], (N_GROUPS, STATE_P)))
    arow = jnp.stack([a_re_s, a_im_s, ls_s] + [jnp.zeros_like(ls_s)] * (SUBLANES - 3), axis=1)
    col = lambda v: jnp.broadcast_to(v[:, :, None], (S5_SETS, SET_STATE, LANES))

    def blockdiag_b(b):
        b = b.astype(F32).reshape(S5_SETS, gs, STATE_P, SSM_GROUP)
        return jnp.einsum('jgpi,gh->jgihp', b, eye).reshape(S5_SETS, LANES, SET_STATE)

    def blockdiag_c(c):
        c = c.astype(F32).reshape(S5_SETS, gs, SSM_GROUP, STATE_P)
        return jnp.einsum('jgop,gh->jgpho', c, eye).reshape(S5_SETS, SET_STATE, LANES)

    out_shapes = (
        jax.ShapeDtypeStruct((S5_SETS, cw, cw), BF16),
        jax.ShapeDtypeStruct((S5_SETS, cw, 2 * SET_STATE), BF16),
        jax.ShapeDtypeStruct((S5_SETS, 2 * SET_STATE, cw), BF16),
        jax.ShapeDtypeStruct((S5_SETS, 2, SET_STATE), F32),
        jax.ShapeDtypeStruct((S5_SETS, LANES, 2 * SET_STATE), BF16),
        jax.ShapeDtypeStruct((S5_SETS, 2 * SET_STATE, LANES), BF16),
        jax.ShapeDtypeStruct((S5_SETS, 2, SET_STATE), F32),
    )
    wt, wb, wc, a8, wb1, wc1, a1 = pl.pallas_call(
        _s5_weights_kernel,
        grid=(S5_SETS,),
        in_specs=[_set_spec((SUBLANES, SET_STATE)), _set_spec((SET_STATE, LANES)), _set_spec((SET_STATE, LANES)),
                  _set_spec((SET_STATE, LANES)), _set_spec((LANES, SET_STATE)), _set_spec((LANES, SET_STATE)),
                  _set_spec((SET_STATE, LANES)), _set_spec((SET_STATE, LANES))],
        out_specs=tuple(_set_spec(s.shape[1:]) for s in out_shapes),
        out_shape=out_shapes,
        compiler_params=_cparams(("parallel",)),
        name="s5_weights",
    )(arow, col(a_re_s), col(a_im_s), col(ls_s), blockdiag_b(b_re), blockdiag_b(b_im),
      blockdiag_c(c_re), blockdiag_c(c_im))
    d1 = d_skip.astype(F32).reshape(S5_SETS, 1, LANES)
    return {"wt": wt, "wb": wb, "wc": wc, "a8": a8, "dt": jnp.tile(d1, (1, 1, S5_CHUNK)),
            "wb1": wb1, "wc1": wc1, "a1": a1, "d1": d1}


def _glu(y, w):
    z = _dot(y, w)
    return z[:, :D_MODEL] * _sigmoid(z[:, D_MODEL:])


def _glu_res_kernel(x_ref, y_ref, w_ref, o_ref, scr):
    rb = y_ref.shape[1]
    w = w_ref[...]
    for s in range(S5_CHUNK):
        y = jnp.concatenate([y_ref[j, :, s * LANES:(s + 1) * LANES] for j in range(S5_SETS)], axis=1)
        o = _glu(y, w)
        for j in range(S5_SETS):
            scr[j, pl.ds(s, rb, stride=S5_CHUNK), :] = o[:, j * LANES:(j + 1) * LANES]
    for j in range(S5_SETS):
        cs = slice(j * LANES, (j + 1) * LANES)
        o_ref[:, cs] = x_ref[:, cs] + scr[j]


def _glu_res(x, y, w_glu):
    rows = y.shape[1]
    rb = 128
    tok_spec = pl.BlockSpec((rb * S5_CHUNK, D_MODEL), lambda i: (i, 0))
    return pl.pallas_call(
        _glu_res_kernel,
        grid=(rows // rb,),
        in_specs=[tok_spec, pl.BlockSpec((S5_SETS, rb, S5_CHUNK * LANES), lambda i: (0, i, 0)),
                  _const_spec((D_MODEL, 2 * D_MODEL))],
        out_specs=tok_spec,
        out_shape=jax.ShapeDtypeStruct(x.shape, F32),
        scratch_shapes=[pltpu.VMEM((S5_SETS, rb * S5_CHUNK, LANES), F32)],
        compiler_params=_cparams(("parallel",)),
        name="glu_res",
    )(x, y, w_glu)


def _glu_res_rows_kernel(x_ref, y_ref, w_ref, o_ref):
    o_ref[...] = x_ref[...] + _glu(y_ref[...], w_ref[...])


def _glu_res_rows(x, y, w_glu):
    rows = x.shape[0]
    return pl.pallas_call(
        _glu_res_rows_kernel,
        grid=(1,),
        in_specs=[_full_spec((rows, D_MODEL)), _full_spec((rows, D_MODEL)), _full_spec((D_MODEL, 2 * D_MODEL))],
        out_specs=_full_spec((rows, D_MODEL)),
        out_shape=jax.ShapeDtypeStruct((rows, D_MODEL), F32),
        name="glu_res_rows",
    )(x, y, w_glu)


FFN_TM = 512
FFN_FC = D_FF // 2


def _conv_act(h, gt, p1, p2, cw, cb):
    hc = cb + cw[0:1, :] * p2
    hc = hc + cw[1:2, :] * p1
    hc = hc + cw[2:3, :] * h
    return (hc * _sigmoid(hc) * gt).astype(BF16)


def _ffn_kernel(*refs, proj, tiles_per_seq):
    if proj:
        (x_ref, a_ref, wa_ref, g_ref, wu_ref, wg_ref, cw_ref, cb_ref, wd_ref, init_ref,
         o_ref, hl_ref, carry_scr, act_scr) = refs
    else:
        (x_ref, g_ref, wu_ref, wg_ref, cw_ref, cb_ref, wd_ref, init_ref,
         o_ref, hl_ref, carry_scr, act_scr) = refs
    i = pl.program_id(0)

    @pl.when(i % tiles_per_seq == 0)
    def _():
        carry_scr[...] = init_ref[...]

    x = x_ref[...]
    if proj:
        a = jnp.concatenate([a_ref[p] for p in range(N_PAIRS)], axis=1)
        x = x + _dot(a, wa_ref[...])
    u = _rmsnorm(x, g_ref[...]).astype(BF16)
    tm = x.shape[0]
    row8 = lax.broadcasted_iota(jnp.int32, (SUBLANES, 1), 0)
    out = x
    for c in range(D_FF // FFN_FC):
        cs = slice(c * FFN_FC, (c + 1) * FFN_FC)
        h = _dot(u, wu_ref[:, cs])
        gt = _dot(u, wg_ref[:, cs])
        cw = cw_ref[:, cs]
        cb = cb_ref[:, cs]
        carry = carry_scr[:, cs]
        p1 = pltpu.roll(h, 1, axis=0)
        p2 = pltpu.roll(h, 2, axis=0)
        act_scr[...] = _conv_act(h, gt, p1, p2, cw, cb)
        c1 = pltpu.roll(carry, 1, axis=0)
        c2 = pltpu.roll(carry, 2, axis=0)
        p1h = jnp.where(row8 < 1, c1, p1[:SUBLANES])
        p2h = jnp.where(row8 < 2, c2, p2[:SUBLANES])
        act_scr[0:SUBLANES, :] = _conv_act(h[:SUBLANES], gt[:SUBLANES], p1h, p2h, cw, cb)
        last = h[tm - SUBLANES:, :]
        carry_scr[:, cs] = last
        hl_ref[0, :, cs] = last
        out = out + _dot(act_scr[...], wd_ref[cs, :])
    o_ref[...] = out


def _ffn(x, a, wa, g, wu, wg, cw, cb, wd, init):
    rows = x.shape[0]
    tm = FFN_TM
    proj = a is not None
    row_spec = pl.BlockSpec((tm, D_MODEL), lambda i: (i, 0))
    in_specs = [row_spec]
    args = [x]
    if proj:
        in_specs += [pl.BlockSpec((N_PAIRS, tm, LANES), lambda i: (0, i, 0)), _const_spec((D_MODEL, D_MODEL))]
        args += [a, wa]
    in_specs += [_const_spec((1, D_MODEL)), _const_spec((D_MODEL, D_FF)), _const_spec((D_MODEL, D_FF)),
                 _const_spec((SUBLANES, D_FF)), _const_spec((1, D_FF)), _const_spec((D_FF, D_MODEL)),
                 _const_spec((SUBLANES, D_FF))]
    args += [g, wu, wg, cw, cb, wd, init]
    return pl.pallas_call(
        functools.partial(_ffn_kernel, proj=proj, tiles_per_seq=SEQ // tm),
        grid=(rows // tm,),
        in_specs=in_specs,
        out_specs=(row_spec, pl.BlockSpec((1, SUBLANES, D_FF), lambda i: (i, 0, 0))),
        out_shape=(jax.ShapeDtypeStruct((rows, D_MODEL), F32),
                   jax.ShapeDtypeStruct((rows // tm, SUBLANES, D_FF), F32)),
        scratch_shapes=[pltpu.VMEM((SUBLANES, D_FF), F32), pltpu.VMEM((tm, FFN_FC), BF16)],
        compiler_params=_cparams(("arbitrary",)),
        name="ffn_proj" if proj else "ffn",
    )(*args)


def _ffn_small_kernel(*refs, proj):
    if proj:
        (x_ref, a_ref, wa_ref, g_ref, wu_ref, wg_ref, cw_ref, cb_ref, wd_ref, b0_ref, b1_ref,
         o_ref, h_ref) = refs
    else:
        (x_ref, g_ref, wu_ref, wg_ref, cw_ref, cb_ref, wd_ref, b0_ref, b1_ref, o_ref, h_ref) = refs
    x = x_ref[...]
    if proj:
        x = x + _dot(a_ref[...], wa_ref[...])
    u = _rmsnorm(x, g_ref[...]).astype(BF16)
    row = lax.broadcasted_iota(jnp.int32, (SMALL_ROWS, 1), 0)
    is_sample = row < N_SAMPLE
    meta_end = META_ROW0 + N_META
    out = x
    for c in range(D_FF // FFN_FC):
        cs = slice(c * FFN_FC, (c + 1) * FFN_FC)
        h = _dot(u, wu_ref[:, cs])
        gt = _dot(u, wg_ref[:, cs])
        p1 = jnp.where(is_sample, b1_ref[:, cs],
                       jnp.where((row >= META_ROW0 + 1) & (row < meta_end), pltpu.roll(h, 1, axis=0), 0.0))
        p2 = jnp.where(is_sample, b0_ref[:, cs],
                       jnp.where((row >= META_ROW0 + 2) & (row < meta_end), pltpu.roll(h, 2, axis=0), 0.0))
        act = _conv_act(h, gt, p1, p2, cw_ref[:, cs], cb_ref[:, cs])
        h_ref[:, cs] = h
        out = out + _dot(act, wd_ref[cs, :])
    o_ref[...] = out


def _ffn_small(x, a, wa, g, wu, wg, cw, cb, wd, b0, b1):
    proj = a is not None
    in_specs = [_full_spec((SMALL_ROWS, D_MODEL))]
    args = [x]
    if proj:
        in_specs += [_full_spec((SMALL_ROWS, D_MODEL)), _full_spec((D_MODEL, D_MODEL))]
        args += [a, wa]
    in_specs += [_full_spec((1, D_MODEL)), _full_spec((D_MODEL, D_FF)), _full_spec((D_MODEL, D_FF)),
                 _full_spec((SUBLANES, D_FF)), _full_spec((1, D_FF)), _full_spec((D_FF, D_MODEL)),
                 _full_spec((SMALL_ROWS, D_FF)), _full_spec((SMALL_ROWS, D_FF))]
    args += [g, wu, wg, cw, cb, wd, b0, b1]
    return pl.pallas_call(
        functools.partial(_ffn_small_kernel, proj=proj),
        grid=(1,),
        in_specs=in_specs,
        out_specs=(_full_spec((SMALL_ROWS, D_MODEL)), _full_spec((SMALL_ROWS, D_FF))),
        out_shape=(jax.ShapeDtypeStruct((SMALL_ROWS, D_MODEL), F32),
                   jax.ShapeDtypeStruct((SMALL_ROWS, D_FF), F32)),
        compiler_params=_cparams(("arbitrary",)),
        name="ffn_small_proj" if proj else "ffn_small",
    )(*args)


ATT_T = 256
ATT_HPS = 4
DEC_PPS = 8
N_BIAS_PARTS = 3


def _qkv_kernel(x_ref, g_ref, w_ref, wf_ref, bf_ref, qg_ref, kg_ref, seg_ref, exp_ref, swp_ref, tri_ref, cinit_ref,
                kt_ref, vtf_ref, qt_ref, ka_ref, vt_ref, lf_ref, ct_ref, carry_scr, *, tiles_per_seq):
    i = pl.program_id(0)

    @pl.when(i % tiles_per_seq == 0)
    def _():
        carry_scr[...] = jnp.broadcast_to(cinit_ref[...], carry_scr.shape)

    u = _rmsnorm(x_ref[...], g_ref[...]).astype(BF16)
    qkv = _dot(u, w_ref[...])
    seg = seg_ref[...]
    expand = exp_ref[...]

    def head_norm(t, gain):
        ms = _dot((t * t).astype(BF16), seg)
        rinv = lax.rsqrt(ms + EPS)
        return t * _dot(jnp.concatenate(_split2(rinv), axis=1), expand) * gain

    q = head_norm(qkv[:, :D_MODEL], qg_ref[...])
    k = head_norm(qkv[:, D_MODEL:2 * D_MODEL], kg_ref[...])
    v = qkv[:, 2 * D_MODEL:]
    tm = k.shape[0]

    z = _dot(u, wf_ref[...]) + bf_ref[...]
    lf = jnp.minimum(z, 0.0) - jnp.log(1.0 + jnp.exp(-jnp.abs(z)))
    lf_ref[...] = lf
    tri = tri_ref[...]
    c = carry_scr[SUBLANES - 1:SUBLANES, :]
    for part in _split3(lf):
        c = c + _dot(tri, part)
    carry_scr[...] = c[tm - SUBLANES:, :]
    ct_ref[...] = c.T

    k_bias = _dot(jnp.concatenate(_split3(-c), axis=1), swp_ref[...])
    lane = lax.broadcasted_iota(jnp.int32, (1, LANES), 1)
    row = lax.broadcasted_iota(jnp.int32, (LANES, 1), 0)
    q_ones = (row % HEAD_DIM < N_BIAS_PARTS).astype(F32)
    for p in range(N_PAIRS):
        ps = slice(p * LANES, (p + 1) * LANES)
        for e in range(2):
            own = (lane >= HEAD_DIM) if e else (lane < HEAD_DIM)
            ka_ref[2 * p + e] = jnp.where(own, k[:, ps], k_bias[:, ps]).astype(BF16)
        for b in range(tm // ATT_T):
            bs = slice(b * ATT_T, (b + 1) * ATT_T)
            q_t = q[bs, ps].T
            k_t = k[bs, ps].T
            v_t = v[bs, ps].T
            kt_ref[0, ps, bs] = k_t
            vtf_ref[0, ps, bs] = v_t
            for e in range(2):
                own = (row >= HEAD_DIM) if e else (row < HEAD_DIM)
                qt_ref[2 * p + e, b] = jnp.where(own, q_t, q_ones).astype(BF16)
                vt_ref[2 * p + e, b] = jnp.where(own, v_t, 1.0).astype(BF16)


def _qkv(x, g, w_qkv, wf, bf, qg, kg, seg, expand, swp, tri, cinit, *, tm):
    rows = x.shape[0]
    seq_rows = min(SEQ, rows)
    tps = seq_rows // tm
    row_spec = pl.BlockSpec((tm, D_MODEL), lambda i: (i, 0))
    kv_spec = pl.BlockSpec((1, D_MODEL, tm), lambda i: (i // tps, 0, i % tps))
    kv_shape = jax.ShapeDtypeStruct((rows // seq_rows, D_MODEL, seq_rows), F32)
    t_spec = pl.BlockSpec((N_HEADS, tm // ATT_T, LANES, ATT_T), lambda i: (0, i, 0, 0))
    t_shape = jax.ShapeDtypeStruct((N_HEADS, rows // ATT_T, LANES, ATT_T), BF16)
    return pl.pallas_call(
        functools.partial(_qkv_kernel, tiles_per_seq=tps),
        grid=(rows // tm,),
        in_specs=[row_spec, _const_spec((1, D_MODEL)), _const_spec((D_MODEL, 3 * D_MODEL)),
                  _const_spec((D_MODEL, LANES)), _const_spec((1, LANES)), _const_spec((1, D_MODEL)),
                  _const_spec((1, D_MODEL)), _const_spec((D_MODEL, LANES)), _const_spec((2 * LANES, D_MODEL)),
                  _const_spec((N_BIAS_PARTS * LANES, D_MODEL)), _const_spec((tm, tm)), _const_spec((1, LANES))],
        out_specs=(kv_spec, kv_spec, t_spec,
                   pl.BlockSpec((N_HEADS, tm, LANES), lambda i: (0, i, 0)), t_spec,
                   pl.BlockSpec((tm, LANES), lambda i: (i, 0)),
                   pl.BlockSpec((LANES, tm), lambda i: (0, i))),
        out_shape=(kv_shape, kv_shape,
                   t_shape, jax.ShapeDtypeStruct((N_HEADS, rows, LANES), BF16), t_shape,
                   jax.ShapeDtypeStruct((rows, LANES), F32), jax.ShapeDtypeStruct((LANES, rows), F32)),
        scratch_shapes=[pltpu.VMEM((SUBLANES, LANES), F32)],
        compiler_params=_cparams(("arbitrary",)),
        name="qkv_proj",
    )(x, g, w_qkv, wf, bf, qg, kg, seg, expand, swp, tri, cinit)


def _softmax_step(s0, vt, cq, m, acc):
    m_new = jnp.maximum(m, jnp.max(s0, axis=0, keepdims=True) + cq)
    p = jnp.exp(s0 - (m_new - cq))
    acc = jnp.exp(m - m_new) * acc + _dot(vt, p.astype(BF16))
    return m_new, acc


def _attn_finish(acc_even, acc_odd):
    o_t = jnp.concatenate([acc_even[:HEAD_DIM] / acc_even[HEAD_DIM:],
                           acc_odd[HEAD_DIM:] / acc_odd[:HEAD_DIM]], axis=0)
    return o_t.T.astype(BF16)


def _page_copies(pt_ref, cache_refs, bufs, sem, it, slot):
    n_pages = pt_ref.shape[1]
    n_sub = n_pages // DEC_PPS
    b = it // n_sub
    t = it % n_sub
    copies = []
    for i in range(DEC_PPS):
        page = pt_ref[b, n_pages - 1 - (t * DEC_PPS + i)]
        for kind, (src, dst) in enumerate(zip(cache_refs, bufs)):
            copies.append(pltpu.make_async_copy(src.at[page], dst.at[slot, i], sem.at[slot, kind]))
    return copies


def _decode_substep(it, n_sub, qd_ref, low_ref, k_pages, v_pages, lf_pages, do_ref, dm_ref, dl_ref,
                    qc_scr, m_scr, l_scr, tail_scr, acc_scr):
    b = it // n_sub
    t = it % n_sub
    mine = lax.broadcasted_iota(jnp.int32, (1, LANES), 1) == b

    @pl.when(it == 0)
    def _():
        do_ref[...] = jnp.zeros_like(do_ref)
        dm_ref[...] = jnp.zeros_like(dm_ref)
        dl_ref[...] = jnp.zeros_like(dl_ref)

    @pl.when(t == 0)
    def _():
        qcol = jnp.sum(jnp.where(mine, qd_ref[...], 0.0), axis=1, keepdims=True)
        qc_scr[...] = jnp.broadcast_to(qcol, (D_MODEL, LANES))
        m_scr[...] = jnp.full(m_scr.shape, NEG_INF, F32)
        l_scr[...] = jnp.zeros_like(l_scr)
        tail_scr[...] = jnp.zeros_like(tail_scr)
        acc_scr[...] = jnp.zeros_like(acc_scr)

    low = low_ref[...]
    tail = tail_scr[...]
    scores = []
    tiles = [[] for _ in range(DEC_PPS)]
    for h in range(N_HEADS):
        tile = [None] * DEC_PPS
        for r in range(h * HEAD_DIM, (h + 1) * HEAD_DIM, SUBLANES):
            rows = slice(r, r + SUBLANES)
            q_rows = qc_scr[rows, :]
            for i in range(DEC_PPS):
                term = k_pages(i, rows) * q_rows
                tile[i] = term if tile[i] is None else tile[i] + term
        for i in range(DEC_PPS):
            tiles[i].append(tile[i])
    lf_all = jnp.concatenate([lf_pages(i) for i in range(DEC_PPS)], axis=0)
    later = _dot_split(_split3(lf_all), low)
    for i in range(DEC_PPS):
        hs = slice(i * N_HEADS, (i + 1) * N_HEADS)
        s = jnp.sum(jnp.concatenate(tiles[i], axis=0).reshape(N_HEADS, SUBLANES, LANES), axis=1)
        scores.append(s + tail + later[hs])
        tail = tail + jnp.sum(lf_all[hs], axis=1, keepdims=True)
    tail_scr[...] = tail
    m_old = m_scr[...]
    s_max = scores[0]
    for s in scores[1:]:
        s_max = jnp.maximum(s_max, s)
    m_new = jnp.maximum(m_old, jnp.max(s_max, axis=1, keepdims=True))
    alpha = jnp.exp(m_old - m_new)
    probs = [jnp.exp(s - m_new) for s in scores]
    p_sum = probs[0]
    for p in probs[1:]:
        p_sum = p_sum + p
    l_scr[...] = alpha * l_scr[...] + jnp.sum(p_sum, axis=1, keepdims=True)
    m_scr[...] = m_new
    for h in range(N_HEADS):
        rows = slice(h * HEAD_DIM, (h + 1) * HEAD_DIM)
        acc = acc_scr[rows, :] * alpha[h:h + 1, :]
        for i in range(DEC_PPS):
            acc = acc + v_pages(i, rows) * probs[i][h:h + 1, :]
        acc_scr[rows, :] = acc

    @pl.when(t == n_sub - 1)
    def _():
        do_ref[...] = jnp.where(mine, jnp.sum(acc_scr[...], axis=1, keepdims=True), do_ref[...])
        dm_ref[...] = jnp.where(mine, m_scr[...], dm_ref[...])
        dl_ref[...] = jnp.where(mine, l_scr[...], dl_ref[...])


def _attn_kernel(pt_ref, qt_ref, ka_ref, vt_ref, ct_ref, kam_ref, vtm_ref, qd_ref, low_ref,
                 ck_hbm, cv_hbm, clf_hbm, o_ref, do_ref, dm_ref, dl_ref,
                 kbuf, vbuf, lfbuf, sem, qc_scr, m_scr, l_scr, tail_scr, acc_scr):
    key = lax.broadcasted_iota(jnp.int32, (ATT_T, ATT_T), 0)
    qry = lax.broadcasted_iota(jnp.int32, (ATT_T, ATT_T), 1)

    n_qblk = SEQ // ATT_T
    n_sub = pt_ref.shape[1] // DEC_PPS
    step0 = (pl.program_id(0) * pl.num_programs(1) + pl.program_id(1)) * n_qblk
    total = pl.num_programs(0) * pl.num_programs(1) * n_qblk
    copies = functools.partial(_page_copies, pt_ref, (ck_hbm, cv_hbm, clf_hbm), (kbuf, vbuf, lfbuf), sem)

    @pl.when(step0 == 0)
    def _():
        for c in copies(0, 0):
            c.start()

    def qblock(qi, _):
        it = step0 + qi
        slot = it % 2

        @pl.when(it + 1 < total)
        def _():
            for c in copies(it + 1, 1 - slot):
                c.start()

        for c in copies(it, slot):
            c.wait()
        _decode_substep(it, n_sub, qd_ref, low_ref,
                        lambda i, rows: kbuf[slot, i, rows, :], lambda i, rows: vbuf[slot, i, rows, :],
                        lambda i: lfbuf[slot, i],
                        do_ref, dm_ref, dl_ref, qc_scr, m_scr, l_scr, tail_scr, acc_scr)

        r0 = pl.multiple_of(qi * ATT_T, ATT_T)
        qts = [qt_ref[h, qi] for h in range(ATT_HPS)]
        cqs = [ct_ref[0, h // 2, qi][h % 2:h % 2 + 1, :] for h in range(ATT_HPS)]

        def scores(h, blk):
            c0 = pl.multiple_of(blk * ATT_T, ATT_T)
            return _dot(ka_ref[h, pl.ds(c0, ATT_T), :], qts[h])

        state = []
        for h in range(ATT_HPS):
            state += [jnp.full((1, ATT_T), NEG_INF, F32), jnp.zeros((LANES, ATT_T), F32), scores(h, 0)]

        def kvstep(j, st):
            new = []
            for h in range(ATT_HPS):
                m, acc, s_cur = st[3 * h:3 * h + 3]
                s_next = scores(h, j + 1)
                m, acc = _softmax_step(s_cur, vt_ref[h, j], cqs[h], m, acc)
                new += [m, acc, s_next]
            return tuple(new)

        st = lax.fori_loop(0, qi, kvstep, tuple(state))
        accs = []
        for h in range(ATT_HPS):
            m, acc, s_diag = st[3 * h:3 * h + 3]
            s_diag = jnp.where(key <= qry, s_diag, NEG_INF)
            s_meta = _dot(kam_ref[h], qts[h])
            top = jnp.maximum(jnp.max(s_diag, axis=0, keepdims=True), jnp.max(s_meta, axis=0, keepdims=True))
            m_new = jnp.maximum(m, top + cqs[h])
            shift = m_new - cqs[h]
            p_diag = jnp.exp(s_diag - shift).astype(BF16)
            p_meta = jnp.exp(s_meta - shift).astype(BF16)
            accs.append(jnp.exp(m - m_new) * acc + _dot(vt_ref[h, qi], p_diag) + _dot(vtm_ref[h], p_meta))
        for p in range(ATT_HPS // 2):
            o_ref[p, pl.ds(r0, ATT_T), :] = _attn_finish(accs[2 * p], accs[2 * p + 1])
        return 0

    lax.fori_loop(0, n_qblk, qblock, 0)


def _attn(qt, ka, vt, ct, kam, vtm, page_table, qt_dec, low, cache_kt, cache_vt, cache_lft):
    nblk = SEQ // ATT_T
    hps = ATT_HPS
    n_groups = N_HEADS // hps
    n_pages = page_table.shape[1]
    assert N_BATCH * n_groups * nblk == N_SAMPLE * (n_pages // DEC_PPS)
    t_spec = pl.BlockSpec((hps, nblk, LANES, ATT_T), lambda n, g, pt: (g, n, 0, 0))
    const = lambda shape: pl.BlockSpec(shape, lambda n, g, pt: (0,) * len(shape))
    hbm = pl.BlockSpec(memory_space=pl.ANY)
    stat = lambda: pltpu.VMEM((N_HEADS, LANES), F32)
    grid_spec = pltpu.PrefetchScalarGridSpec(
        num_scalar_prefetch=1,
        grid=(N_BATCH, n_groups),
        in_specs=[t_spec,
                  pl.BlockSpec((hps, SEQ, LANES), lambda n, g, pt: (g, n, 0)),
                  t_spec,
                  pl.BlockSpec((1, hps // 2, nblk, SUBLANES, ATT_T), lambda n, g, pt: (n, g, 0, 0, 0)),
                  pl.BlockSpec((hps, N_META, LANES), lambda n, g, pt: (g, 0, 0)),
                  pl.BlockSpec((hps, LANES, N_META), lambda n, g, pt: (g, 0, 0)),
                  const((D_MODEL, N_SAMPLE)), const((PAGE, PAGE)), hbm, hbm, hbm],
        out_specs=(pl.BlockSpec((hps // 2, SEQ, LANES), lambda n, g, pt: (g, n, 0)),
                   const((D_MODEL, N_SAMPLE)), const((N_HEADS, N_SAMPLE)), const((N_HEADS, N_SAMPLE))),
        scratch_shapes=[pltpu.VMEM((2, DEC_PPS, D_MODEL, PAGE), F32),
                        pltpu.VMEM((2, DEC_PPS, D_MODEL, PAGE), F32),
                        pltpu.VMEM((2, DEC_PPS, N_HEADS, PAGE), F32),
                        pltpu.SemaphoreType.DMA((2, 3)),
                        pltpu.VMEM((D_MODEL, LANES), F32), stat(), stat(), stat(),
                        pltpu.VMEM((D_MODEL, LANES), F32)],
    )
    return pl.pallas_call(
        _attn_kernel,
        grid_spec=grid_spec,
        out_shape=(jax.ShapeDtypeStruct((N_PAIRS, N_BATCH * SEQ, LANES), BF16),
                   jax.ShapeDtypeStruct((D_MODEL, N_SAMPLE), F32),
                   jax.ShapeDtypeStruct((N_HEADS, N_SAMPLE), F32),
                   jax.ShapeDtypeStruct((N_HEADS, N_SAMPLE), F32)),
        compiler_params=_cparams(("arbitrary", "arbitrary")),
        name="attn_prompt_decode",
    )(page_table, qt, ka, vt, ct, kam, vtm, qt_dec, low, cache_kt, cache_vt, cache_lft)


def _meta_attn_kernel(qtm_ref, kam_ref, vtm_ref, cmt_ref, o_ref):
    key = lax.broadcasted_iota(jnp.int32, (N_META, N_META), 0)
    qry = lax.broadcasted_iota(jnp.int32, (N_META, N_META), 1)
    for p in range(N_PAIRS):
        accs = []
        for e in range(2):
            h = 2 * p + e
            cq = cmt_ref[p, e:e + 1, :]
            s0 = jnp.where(key <= qry, _dot(kam_ref[h], qtm_ref[h]), NEG_INF)
            m = jnp.max(s0, axis=0, keepdims=True) + cq
            pr = jnp.exp(s0 - (m - cq))
            accs.append(_dot(vtm_ref[h], pr.astype(BF16)))
        o_ref[p] = _attn_finish(*accs)


def _meta_attention(qtm, kam, vtm, cmt):
    pm = (N_PAIRS, N_META, LANES)
    t_shape = (N_HEADS, LANES, N_META)
    o = pl.pallas_call(
        _meta_attn_kernel,
        grid=(1,),
        in_specs=[_full_spec(t_shape), _full_spec((N_HEADS, N_META, LANES)), _full_spec(t_shape),
                  _full_spec((N_PAIRS, SUBLANES, N_META))],
        out_specs=_full_spec(pm),
        out_shape=jax.ShapeDtypeStruct(pm, BF16),
        name="attn_meta",
    )(qtm, kam, vtm, cmt)
    return _pairs_to_rows(o)


def _decode_combine_kernel(acc_ref, m_ref, l_ref, q_ref, kn_ref, vn_ref, lfn_ref, seg_ref, exp_ref, o_ref):
    expand = exp_ref[...]
    ex = lambda x: _dot_split(_split2(x), expand)
    s_new = _dot_split(_split2(q_ref[...].astype(F32) * kn_ref[...]), seg_ref[...])
    m_c = m_ref[...] + lfn_ref[...]
    m_all = jnp.maximum(m_c, s_new)
    w_c = jnp.exp(m_c - m_all)
    w_n = jnp.exp(s_new - m_all)
    den = l_ref[...] * w_c + w_n
    o = (acc_ref[...] * ex(w_c) + vn_ref[...] * ex(w_n)) / ex(den)
    o_ref[...] = o.astype(BF16)


def _decode_combine(acc, m, l, q, kn, vn, lfn, seg, expand):
    shapes = [(N_SAMPLE, D_MODEL), (N_SAMPLE, LANES), (N_SAMPLE, LANES), (N_SAMPLE, D_MODEL),
              (N_SAMPLE, D_MODEL), (N_SAMPLE, D_MODEL), (N_SAMPLE, LANES), (D_MODEL, LANES), (LANES, D_MODEL)]
    return pl.pallas_call(
        _decode_combine_kernel,
        grid=(1,),
        in_specs=[_full_spec(s) for s in shapes],
        out_specs=_full_spec((N_SAMPLE, D_MODEL)),
        out_shape=jax.ShapeDtypeStruct((N_SAMPLE, D_MODEL), BF16),
        name="attn_decode_combine",
    )(acc, m, l, q, kn, vn, lfn, seg, expand)


def _pairs_to_rows(xp):
    return jnp.transpose(xp, (1, 0, 2)).reshape(xp.shape[1], D_MODEL)


def kernel(x_prompt, x_sample, state_ssm_re, state_ssm_im, cache_k, cache_v, cache_logf, state_conv, page_table, meta_tokens, norm_mix_g, norm_ffn_g, ssm_a_re, ssm_a_im, ssm_b_re, ssm_b_im, ssm_c_re, ssm_c_im, ssm_d, ssm_log_step, ssm_w_glu, attn_w_qkv, attn_w_f, attn_b_f, attn_q_g, attn_k_g, attn_w_o, ffn_w_up, ffn_w_gate, ffn_conv_w, ffn_conv_b, ffn_w_down):
    n_body = N_BATCH * SEQ
    xb = x_prompt.reshape(n_body, D_MODEL)
    pad_rows = SMALL_ROWS - N_SAMPLE - N_META
    xs = jnp.concatenate([x_sample.reshape(N_SAMPLE, D_MODEL), meta_tokens.astype(F32),
                          jnp.zeros((pad_rows, D_MODEL), F32)], axis=0)
    meta_rows = slice(META_ROW0, META_ROW0 + N_META)

    g_mix = norm_mix_g.astype(F32).reshape(2, 1, D_MODEL)
    g_ffn = norm_ffn_g.astype(F32).reshape(2, 1, D_MODEL)
    w_glu = ssm_w_glu.astype(BF16)
    w_up = ffn_w_up.astype(BF16)
    w_gate = ffn_w_gate.astype(BF16)
    w_down = ffn_w_down.astype(BF16)
    conv_w = jnp.pad(ffn_conv_w.astype(F32), ((0, 0), (0, SUBLANES - CONV_W), (0, 0)))
    conv_b = ffn_conv_b.astype(F32).reshape(2, 1, D_FF)

    s5w = _s5_weights(ssm_a_re, ssm_a_im, ssm_b_re, ssm_b_im, ssm_c_re, ssm_c_im, ssm_d, ssm_log_step)
    ub = _norm_shuffle(xb, g_mix[0])
    u_small = _norm_rows(xs, g_mix[0])
    n_mc = N_META // S5_CHUNK
    um = u_small[meta_rows].reshape(n_mc, S5_CHUNK, S5_SETS, LANES)
    um = jnp.transpose(um, (2, 0, 1, 3)).reshape(S5_SETS, n_mc, 1, S5_CHUNK * LANES)
    um = jnp.broadcast_to(um, (S5_SETS, n_mc, N_BATCH, S5_CHUNK * LANES)).reshape(S5_SETS, N_META, S5_CHUNK * LANES)
    yb, ym, ys, hpr, hpi, hsr, hsi = _s5(
        ub, um, u_small, state_ssm_re.astype(F32).reshape(N_SAMPLE, -1),
        state_ssm_im.astype(F32).reshape(N_SAMPLE, -1), s5w)
    x1b = _glu_res(xb, yb, w_glu)
    ym_tok = ym.reshape(S5_SETS, n_mc, N_BATCH, S5_CHUNK, LANES)[:, :, 0]
    ym_tok = jnp.transpose(ym_tok, (1, 2, 0, 3)).reshape(N_META, D_MODEL)
    y_small = jnp.concatenate([ys, ym_tok, jnp.zeros((pad_rows, D_MODEL), BF16)], axis=0)
    x1s = _glu_res_rows(xs, y_small, w_glu)

    def ffn_layer(i, xb_in, xs_in, ab, a_small, wa):
        sc = state_conv[i].astype(F32)
        pad = ((0, SMALL_ROWS - N_SAMPLE), (0, 0))
        xs_out, h_small = _ffn_small(xs_in, a_small, wa, g_ffn[i], w_up[i], w_gate[i], conv_w[i], conv_b[i],
                                     w_down[i], jnp.pad(sc[:, 0], pad), jnp.pad(sc[:, 1], pad))
        init = jnp.concatenate([jnp.zeros((SUBLANES - 2, D_FF), F32),
                                h_small[META_ROW0 + N_META - 2:META_ROW0 + N_META]], axis=0)
        xb_out, hl = _ffn(xb_in, ab, wa, g_ffn[i], w_up[i], w_gate[i], conv_w[i], conv_b[i], w_down[i], init)
        tps = SEQ // FFN_TM
        conv_p = hl.reshape(N_BATCH, tps, SUBLANES, D_FF)[:, tps - 1, SUBLANES - 2:, :]
        conv_s = jnp.stack([sc[:, 1], h_small[:N_SAMPLE]], axis=1)
        return xb_out, xs_out, conv_p, conv_s

    x2b, x2s, conv_p0, conv_s0 = ffn_layer(0, x1b, x1s, None, None, None)

    w_qkv = attn_w_qkv.astype(BF16)
    wf = jnp.pad(attn_w_f.astype(BF16), ((0, 0), (0, LANES - N_HEADS)))
    bf = jnp.pad(attn_b_f.astype(F32), (0, LANES - N_HEADS)).reshape(1, LANES)
    qg = (jnp.tile(attn_q_g.astype(F32), N_HEADS) * (HEAD_DIM ** -0.5)).reshape(1, D_MODEL)
    kg = jnp.tile(attn_k_g.astype(F32), N_HEADS).reshape(1, D_MODEL)
    head_of_lane = jnp.arange(D_MODEL) // HEAD_DIM
    onehot = (head_of_lane[:, None] == jnp.arange(LANES)[None, :])
    seg = (onehot.astype(F32) / HEAD_DIM).astype(BF16)
    expand = onehot.T.astype(BF16)
    tri_body = (jnp.arange(FFN_TM)[:, None] >= jnp.arange(FFN_TM)[None, :]).astype(BF16)
    r = jnp.arange(SMALL_ROWS)
    tri_small = ((r[:, None] >= r[None, :]) & (r[None, :] >= META_ROW0) & (r[:, None] < META_ROW0 + N_META))
    tri_small = tri_small.astype(BF16)

    expand2 = jnp.concatenate([expand, expand], axis=0)
    swp = expand.reshape(LANES // 2, 2, D_MODEL)[:, ::-1].reshape(LANES, D_MODEL)
    lane_in_head = jnp.arange(D_MODEL) % HEAD_DIM
    swp = jnp.concatenate([jnp.where(lane_in_head[None, :] == k, swp, jnp.zeros_like(swp))
                           for k in range(N_BIAS_PARTS)], axis=0)

    kts_f32, vts_f32, qts, kas, vts, lfs, cts = _qkv(x2s, g_mix[1], w_qkv, wf, bf, qg, kg, seg, expand2, swp,
                                                     tri_small, jnp.zeros((1, LANES), F32), tm=SMALL_ROWS)
    c_meta_total = cts[:, META_ROW0 + N_META - 1].reshape(1, LANES)
    ktb_f32, vtb_f32, qtb, kab, vtb, lfb, ctb = _qkv(x2b, g_mix[1], w_qkv, wf, bf, qg, kg, seg, expand2, swp,
                                                     tri_body, c_meta_total, tm=FFN_TM)
    kts_f32, vts_f32 = kts_f32[0], vts_f32[0]

    nblk = SEQ // ATT_T
    ct = ctb[:N_HEADS].reshape(N_PAIRS, 2, N_BATCH, nblk, ATT_T)
    ct = jnp.transpose(ct, (2, 0, 3, 1, 4))
    ct = jnp.pad(ct, ((0, 0), (0, 0), (0, 0), (0, SUBLANES - 2), (0, 0)))
    cmt = cts[:N_HEADS, meta_rows].reshape(N_PAIRS, 2, N_META)
    cmt = jnp.pad(cmt, ((0, 0), (0, SUBLANES - 2), (0, 0)))
    qtm = qts[:, 0, :, meta_rows]
    kam = kas[:, meta_rows]
    vtm = vts[:, 0, :, meta_rows]
    o_meta = _meta_attention(qtm, kam, vtm, cmt)

    n_pool = cache_k.shape[0]
    cache_kt = jnp.transpose(cache_k, (0, 2, 3, 1)).reshape(n_pool, D_MODEL, PAGE)
    cache_vt = jnp.transpose(cache_v, (0, 2, 3, 1)).reshape(n_pool, D_MODEL, PAGE)
    cache_lft = jnp.transpose(cache_logf.astype(F32), (0, 2, 1))
    qt_pairs = qts[:, 0, :, :N_SAMPLE].reshape(N_PAIRS, 2, 2, HEAD_DIM, N_SAMPLE)
    qt_dec = jnp.stack([qt_pairs[:, 0, 0], qt_pairs[:, 1, 1]], axis=1).reshape(D_MODEL, N_SAMPLE)
    low = (jnp.arange(PAGE)[:, None] > jnp.arange(PAGE)[None, :]).astype(BF16)
    ob, acc_t, m_t, l_t = _attn(qtb, kab, vtb, ct, kam, vtm, page_table.astype(jnp.int32), qt_dec.astype(F32), low,
                                cache_kt, cache_vt, cache_lft)
    pad_heads = lambda x, v: jnp.pad(x.T, ((0, 0), (0, LANES - N_HEADS)), constant_values=v)
    o_dec = _decode_combine(acc_t.T, pad_heads(m_t, 0.0), pad_heads(l_t, 1.0), qt_dec.T, kts_f32[:, :N_SAMPLE].T,
                            vts_f32[:, :N_SAMPLE].T, lfs[:N_SAMPLE], (onehot.astype(BF16)), expand)
    o_small = jnp.concatenate([o_dec, o_meta, jnp.zeros((pad_rows, D_MODEL), BF16)], axis=0)

    w_o = attn_w_o.astype(BF16)
    x4b, x4s, conv_p1, conv_s1 = ffn_layer(1, x2b, x2s, ob, o_small, w_o)

    def kv_with_meta(small_t, body_t):
        m = jnp.broadcast_to(small_t[None, :, meta_rows], (N_BATCH, D_MODEL, N_META))
        full = jnp.concatenate([m, body_t], axis=2).reshape(N_BATCH, N_HEADS, HEAD_DIM, SEQ + N_META)
        return jnp.transpose(full, (0, 3, 1, 2))

    def head_cols(x_t):
        return jnp.transpose(x_t[:, :N_SAMPLE].reshape(N_HEADS, HEAD_DIM, N_SAMPLE), (2, 0, 1))[:, None]

    k_p = kv_with_meta(kts_f32, ktb_f32)
    v_p = kv_with_meta(vts_f32, vtb_f32)
    lf_meta = jnp.broadcast_to(lfs[meta_rows, :N_HEADS][None], (N_BATCH, N_META, N_HEADS))
    lf_p = jnp.concatenate([lf_meta, lfb[:, :N_HEADS].reshape(N_BATCH, SEQ, N_HEADS)], axis=1)
    return (x4b.reshape(N_BATCH, SEQ, D_MODEL),
            x4s[:N_SAMPLE].reshape(N_SAMPLE, 1, D_MODEL),
            hpr.reshape(N_BATCH, N_GROUPS, STATE_P), hpi.reshape(N_BATCH, N_GROUPS, STATE_P),
            hsr.reshape(N_SAMPLE, N_GROUPS, STATE_P), hsi.reshape(N_SAMPLE, N_GROUPS, STATE_P),
            k_p, v_p, lf_p,
            head_cols(kts_f32), head_cols(vts_f32),
            lfs[:N_SAMPLE, :N_HEADS].reshape(N_SAMPLE, 1, N_HEADS),
            jnp.stack([conv_p0, conv_p1]), jnp.stack([conv_s0, conv_s1]))
```
